```python
import jax, jax.numpy as jnp
from jax import lax
import numpy as np

D_MODEL = 2048
BATCH = 2
SEQ = 8192
DEPTH = 4

N_MEM = 256
NUM_MIXERS = 2
N_A = (DEPTH + 1) // 2
N_B = DEPTH // 2
MIX_WIDTH = (3 * D_MODEL) // 4
XA_HEADS = 4
XA_WIDTH = D_MODEL - MIX_WIDTH
XA_DIM = XA_WIDTH // XA_HEADS
HGRN_HEAD_DIM = 128
HGRN_HEADS = MIX_WIDTH // HGRN_HEAD_DIM
GLA_HEADS = 4
GLA_KEY_WIDTH = MIX_WIDTH // 2
GLA_DK = GLA_KEY_WIDTH // GLA_HEADS
GLA_DV = MIX_WIDTH // GLA_HEADS
GLA_GATE_RANK = 16
GLA_GATE_NORMALIZER = 16.0
CHUNK = 64
D_FF = ((8 * D_MODEL // 3 + 255) // 256) * 256
EPS = 1e-6
HGRN_IN = 4 * MIX_WIDTH + XA_WIDTH
GLA_IN = 2 * GLA_KEY_WIDTH + 2 * MIX_WIDTH + GLA_GATE_RANK + XA_WIDTH

kernel_name = "hgrn2_gla_interleaved_memxattn_trunk"


def _rmsnorm(x, gain):
    xf = x.astype(jnp.float32)
    y = xf * lax.rsqrt(jnp.mean(xf * xf, axis=-1, keepdims=True) + EPS)
    return (y * gain.astype(jnp.float32)).astype(x.dtype)


def _split_heads(t, n_heads):
    return t.reshape(t.shape[:-1] + (n_heads, t.shape[-1] // n_heads))


def _chunk_gated_linear_attention(q, k, v, log_a):
    b, t, h, dk = q.shape
    dv = v.shape[-1]
    n = t // CHUNK

    def to_chunks(u):
        return u.reshape(b, n, CHUNK, h, u.shape[-1]).transpose(1, 0, 3, 2, 4)

    mask = jnp.tril(jnp.ones((CHUNK, CHUNK), dtype=bool))

    def step(state, inp):
        qc, kc, vc, gc = inp
        cum = jnp.cumsum(gc, axis=2)
        inter = jnp.einsum('bhtd,bhde->bhte', qc * jnp.exp(cum), state)
        diff = cum[:, :, :, None, :] - cum[:, :, None, :, :]
        decay = jnp.exp(jnp.where(mask[:, :, None], diff, -jnp.inf))
        scores = jnp.einsum('bhtd,bhsd,bhtsd->bhts', qc, kc, decay)
        intra = jnp.einsum('bhts,bhse->bhte', scores, vc)
        last = cum[:, :, -1, :]
        new_state = jnp.exp(last)[..., None] * state + jnp.einsum(
            'bhsd,bhse->bhde', kc * jnp.exp(last[:, :, None, :] - cum), vc)
        return new_state, inter + intra

    s0 = jnp.zeros((b, h, dk, dv), jnp.float32)
    _, out = lax.scan(step, s0, (to_chunks(q), to_chunks(k), to_chunks(v), to_chunks(log_a)))
    return out.transpose(1, 0, 3, 2, 4).reshape(b, t, h, dv)


def _gated_head_norm(o, gain, gate_raw, n_heads):
    of = o * lax.rsqrt(jnp.mean(o * o, axis=-1, keepdims=True) + EPS) * gain.astype(jnp.float32)
    gate = jax.nn.silu(_split_heads(gate_raw.astype(jnp.float32), n_heads))
    y = of * gate
    return y.reshape(y.shape[:2] + (n_heads * y.shape[-1],))


def _hgrn2_mixer(hn, w_in, lb, onorm):
    f32 = jnp.float32
    proj = hn @ w_in
    q, f, i, g, xq = jnp.split(proj, [MIX_WIDTH, 2 * MIX_WIDTH, 3 * MIX_WIDTH, 4 * MIX_WIDTH], axis=-1)
    f = lb + (1.0 - lb) * jax.nn.sigmoid(f.astype(f32))
    log_f = jnp.log(f)
    k = 1.0 - f
    q = jax.nn.silu(q.astype(f32))
    o = _chunk_gated_linear_attention(
        _split_heads(q, HGRN_HEADS), _split_heads(k, HGRN_HEADS),
        _split_heads(i.astype(f32), HGRN_HEADS), _split_heads(log_f, HGRN_HEADS))
    y = _gated_head_norm(o, onorm, g, HGRN_HEADS)
    return y.astype(hn.dtype), xq


def _gla_mixer(hn, w_in, w_gk, b_gk, onorm):
    f32 = jnp.float32
    proj = hn @ w_in
    kw, w, r = GLA_KEY_WIDTH, MIX_WIDTH, GLA_GATE_RANK
    q, k, v, g, gk_low, xq = jnp.split(
        proj, [kw, 2 * kw, 2 * kw + w, 2 * kw + 2 * w, 2 * kw + 2 * w + r], axis=-1)
    log_a = jax.nn.log_sigmoid((gk_low @ w_gk + b_gk).astype(f32)) / GLA_GATE_NORMALIZER
    q = q.astype(f32) * (GLA_DK ** -0.5)
    o = _chunk_gated_linear_attention(
        _split_heads(q, GLA_HEADS), _split_heads(k.astype(f32), GLA_HEADS),
        _split_heads(v.astype(f32), GLA_HEADS), _split_heads(log_a, GLA_HEADS))
    y = _gated_head_norm(o, onorm, g, GLA_HEADS)
    return y.astype(hn.dtype), xq


def _memory_attention(xq, mem_n, w_kv):
    f32 = jnp.float32
    k, v = jnp.split(mem_n @ w_kv, 2, axis=-1)
    q = _split_heads(xq, XA_HEADS).astype(f32)
    k = _split_heads(k, XA_HEADS).astype(f32)
    v = _split_heads(v, XA_HEADS).astype(f32)
    s = jnp.einsum('bthd,bmhd->bhtm', q, k) * (XA_DIM ** -0.5)
    p = jax.nn.softmax(s, axis=-1)
    o = jnp.einsum('bhtm,bmhd->bthd', p, v)
    return o.reshape(o.shape[:2] + (XA_WIDTH,)).astype(xq.dtype)


def setup_inputs(seed: int = 0) -> dict:
    key = jax.random.key(seed)
    ks = jax.random.split(key, 20)
    f32 = jnp.float32

    def nrm(k, shape, scale):
        return jax.random.normal(k, shape, f32) * scale

    def gain(k, shape):
        return 1.0 + 0.02 * jax.random.normal(k, shape, f32)

    return {
        "x": jax.random.normal(ks[0], (BATCH, SEQ, D_MODEL), f32),
        "mem": jax.random.normal(ks[1], (BATCH, N_MEM, D_MODEL), f32),
        "norm_mix": gain(ks[2], (DEPTH, D_MODEL)),
        "norm_ffn": gain(ks[3], (DEPTH, D_MODEL)),
        "norm_mem": gain(ks[4], (D_MODEL,)),
        "norm_final": gain(ks[5], (D_MODEL,)),
        "hgrn_w_in": nrm(ks[6], (N_A, D_MODEL, HGRN_IN), D_MODEL ** -0.5),
        "hgrn_lb_logits": nrm(ks[7], (N_A, MIX_WIDTH), 0.5),
        "hgrn_onorm": gain(ks[8], (N_A, HGRN_HEAD_DIM)),
        "gla_w_in": nrm(ks[9], (N_B, D_MODEL, GLA_IN), D_MODEL ** -0.5),
        "gla_w_gk": nrm(ks[10], (N_B, GLA_GATE_RANK, GLA_KEY_WIDTH), GLA_GATE_RANK ** -0.5),
        "gla_b_gk": nrm(ks[11], (N_B, GLA_KEY_WIDTH), 0.01),
        "gla_onorm": gain(ks[12], (N_B, GLA_DV)),
        "w_mem_kv": nrm(ks[13], (DEPTH, D_MODEL, 2 * XA_WIDTH), D_MODEL ** -0.5),
        "w_out": nrm(ks[14], (DEPTH, D_MODEL, D_MODEL), D_MODEL ** -0.5),
        "w_gate_up": nrm(ks[15], (DEPTH, D_MODEL, 2 * D_FF), D_MODEL ** -0.5),
        "w_down": nrm(ks[16], (DEPTH, D_FF, D_MODEL), D_FF ** -0.5),
    }


def reference(x, mem, norm_mix, norm_ffn, norm_mem, norm_final,
              hgrn_w_in, hgrn_lb_logits, hgrn_onorm,
              gla_w_in, gla_w_gk, gla_b_gk, gla_onorm,
              w_mem_kv, w_out, w_gate_up, w_down):
    mem_n = _rmsnorm(mem, norm_mem)
    cs = jnp.cumsum(jax.nn.softmax(hgrn_lb_logits.astype(jnp.float32), axis=0), axis=0)
    lbs = cs - cs[0:1]

    h = x
    for layer in range(DEPTH):
        hn = _rmsnorm(h, norm_mix[layer])
        j = layer // NUM_MIXERS
        if layer % NUM_MIXERS == 0:
            y, xq = _hgrn2_mixer(hn, hgrn_w_in[j], lbs[j], hgrn_onorm[j])
        else:
            y, xq = _gla_mixer(hn, gla_w_in[j], gla_w_gk[j], gla_b_gk[j], gla_onorm[j])
        xa = _memory_attention(xq, mem_n, w_mem_kv[layer])
        h = h + jnp.concatenate([y, xa], axis=-1) @ w_out[layer]

        hn = _rmsnorm(h, norm_ffn[layer])
        gate, up = jnp.split(hn @ w_gate_up[layer], 2, axis=-1)
        h = h + (jax.nn.silu(gate) * up) @ w_down[layer]
    return _rmsnorm(h, norm_final)
```

```python
import functools

import jax
import jax.numpy as jnp
from jax import lax
from jax.experimental import pallas as pl
from jax.experimental.pallas import tpu as pltpu

F32 = jnp.float32
BF16 = jnp.bfloat16

EPS = 1e-6
CHUNK = 64
SUB = 16
N_SUB = CHUNK // SUB
XA_HEADS = 4
HGRN_HEAD_DIM = 128
GLA_HEADS = 4
GLA_GATE_RANK = 16
GLA_GATE_NORMALIZER = 16.0
LANE = 128
GLA_DK_PAD = 256
V7X_VMEM_LIMIT = 56 * 1024 * 1024

NT_DIMS = (((1,), (1,)), ((), ()))
TN_DIMS = (((0,), (0,)), ((), ()))


def _params(*semantics):
    return pltpu.CompilerParams(dimension_semantics=semantics, vmem_limit_bytes=V7X_VMEM_LIMIT)


def _sigmoid(x):
    return 1.0 / (1.0 + jnp.exp(-x))


def _rmsnorm_rows(x, gain):
    ms = jnp.mean(x * x, axis=-1, keepdims=True)
    return x * lax.rsqrt(ms + EPS) * gain


def _norm_matmul_kernel(x_ref, g_ref, w_ref, o_ref, xn_ref):
    @pl.when(pl.program_id(1) == 0)
    def _():
        xn_ref[...] = _rmsnorm_rows(x_ref[...], g_ref[...]).astype(BF16)

    o_ref[...] = jnp.dot(xn_ref[...], w_ref[...], preferred_element_type=F32).astype(o_ref.dtype)


def _norm_matmul(x, gain, w, tm, tn):
    m, k = x.shape
    n = w.shape[1]
    return pl.pallas_call(
        _norm_matmul_kernel,
        grid=(m // tm, n // tn),
        in_specs=[
            pl.BlockSpec((tm, k), lambda i, j: (i, 0)),
            pl.BlockSpec((1, k), lambda i, j: (0, 0)),
            pl.BlockSpec((k, tn), lambda i, j: (0, j)),
        ],
        out_specs=pl.BlockSpec((tm, tn), lambda i, j: (i, j)),
        out_shape=jax.ShapeDtypeStruct((m, n), BF16),
        scratch_shapes=[pltpu.VMEM((tm, k), BF16)],
        compiler_params=_params("parallel", "arbitrary"),
        name="norm_matmul",
    )(x, gain.reshape(1, k), w)


def _ffn_up_kernel(x_ref, g_ref, wg_ref, wu_ref, o_ref, xn_ref):
    @pl.when(pl.program_id(1) == 0)
    def _():
        xn_ref[...] = _rmsnorm_rows(x_ref[...], g_ref[...]).astype(BF16)

    xn = xn_ref[...]
    gate = jnp.dot(xn, wg_ref[...], preferred_element_type=F32)
    up = jnp.dot(xn, wu_ref[...], preferred_element_type=F32)
    o_ref[...] = (gate * _sigmoid(gate) * up).astype(o_ref.dtype)


def _ffn_up(x, gain, w_gate_up, tm, tn):
    m, k = x.shape
    d_ff = w_gate_up.shape[1] // 2
    n_blocks = d_ff // tn
    return pl.pallas_call(
        _ffn_up_kernel,
        grid=(m // tm, n_blocks),
        in_specs=[
            pl.BlockSpec((tm, k), lambda i, j: (i, 0)),
            pl.BlockSpec((1, k), lambda i, j: (0, 0)),
            pl.BlockSpec((k, tn), lambda i, j: (0, j)),
            pl.BlockSpec((k, tn), lambda i, j: (0, j + n_blocks)),
        ],
        out_specs=pl.BlockSpec((tm, tn), lambda i, j: (i, j)),
        out_shape=jax.ShapeDtypeStruct((m, d_ff), BF16),
        scratch_shapes=[pltpu.VMEM((tm, k), BF16)],
        compiler_params=_params("parallel", "arbitrary"),
        name="ffn_up",
    )(x, gain.reshape(1, k), w_gate_up, w_gate_up)


def _matmul_residual_kernel(a_ref, w_ref, r_ref, o_ref):
    o_ref[...] = r_ref[...] + jnp.dot(a_ref[...], w_ref[...], preferred_element_type=F32)


def _matmul_residual(a, w, res, tm, tn):
    m, k = a.shape
    n = w.shape[1]
    return pl.pallas_call(
        _matmul_residual_kernel,
        grid=(m // tm, n // tn),
        in_specs=[
            pl.BlockSpec((tm, k), lambda i, j: (i, 0)),
            pl.BlockSpec((k, tn), lambda i, j: (0, j)),
            pl.BlockSpec((tm, tn), lambda i, j: (i, j)),
        ],
        out_specs=pl.BlockSpec((tm, tn), lambda i, j: (i, j)),
        out_shape=jax.ShapeDtypeStruct((m, n), F32),
        compiler_params=_params("parallel", "arbitrary"),
        name="matmul_residual",
    )(a, w, res)


def _out_proj_kernel(y_ref, xa_ref, w_ref, r_ref, o_ref):
    wy = y_ref.shape[1]
    acc = jnp.dot(y_ref[...], w_ref[:wy, :], preferred_element_type=F32)
    acc += jnp.dot(xa_ref[...], w_ref[wy:, :], preferred_element_type=F32)
    o_ref[...] = r_ref[...] + acc


def _out_proj(y, xa, w, res, tm):
    m, wy = y.shape
    wx = xa.shape[1]
    k, n = w.shape
    return pl.pallas_call(
        _out_proj_kernel,
        grid=(m // tm,),
        in_specs=[
            pl.BlockSpec((tm, wy), lambda i: (i, 0)),
            pl.BlockSpec((tm, wx), lambda i: (i, 0)),
            pl.BlockSpec((k, n), lambda i: (0, 0)),
            pl.BlockSpec((tm, n), lambda i: (i, 0)),
        ],
        out_specs=pl.BlockSpec((tm, n), lambda i: (i, 0)),
        out_shape=jax.ShapeDtypeStruct((m, n), F32),
        compiler_params=_params("parallel"),
        name="out_proj",
    )(y, xa, w, res)


def _rmsnorm_kernel(x_ref, g_ref, o_ref):
    o_ref[...] = _rmsnorm_rows(x_ref[...], g_ref[...])


def _rmsnorm(x, gain, tm):
    m, k = x.shape
    return pl.pallas_call(
        _rmsnorm_kernel,
        grid=(m // tm,),
        in_specs=[pl.BlockSpec((tm, k), lambda i: (i, 0)), pl.BlockSpec((1, k), lambda i: (0, 0))],
        out_specs=pl.BlockSpec((tm, k), lambda i: (i, 0)),
        out_shape=jax.ShapeDtypeStruct((m, k), F32),
        compiler_params=_params("parallel"),
        name="final_rmsnorm",
    )(x, gain.reshape(1, k))


def _xattn_kernel(q_ref, k_ref, v_ref, o_ref):
    d = q_ref.shape[1]
    s = lax.dot_general(q_ref[...], k_ref[0], NT_DIMS, preferred_element_type=F32) * (d ** -0.5)
    p = jnp.exp(s - jnp.max(s, axis=-1, keepdims=True))
    denom = jnp.sum(p, axis=-1, keepdims=True)
    o = jnp.dot(p.astype(BF16), v_ref[0], preferred_element_type=F32)
    o_ref[...] = (o / denom).astype(o_ref.dtype)


def _xattn(proj, xq_block0, kv, batch, tq):
    m = proj.shape[0]
    n_t = m // batch // tq
    n_mem = kv.shape[1]
    d = kv.shape[2] // (2 * XA_HEADS)
    return pl.pallas_call(
        _xattn_kernel,
        grid=(batch, n_t, XA_HEADS),
        in_specs=[
            pl.BlockSpec((tq, d), lambda b, t, h: (b * n_t + t, xq_block0 + h)),
            pl.BlockSpec((1, n_mem, d), lambda b, t, h: (b, 0, h)),
            pl.BlockSpec((1, n_mem, d), lambda b, t, h: (b, 0, XA_HEADS + h)),
        ],
        out_specs=pl.BlockSpec((tq, d), lambda b, t, h: (b * n_t + t, h)),
        out_shape=jax.ShapeDtypeStruct((m, XA_HEADS * d), BF16),
        compiler_params=_params("parallel", "parallel", "arbitrary"),
        name="mem_xattn",
    )(proj, kv, kv)


def _chunk_masks():
    t = lax.broadcasted_iota(jnp.int32, (CHUNK, CHUNK), 0)
    s = lax.broadcasted_iota(jnp.int32, (CHUNK, CHUNK), 1)
    tril = (s <= t)
    diag = tril & ((t // SUB) == (s // SUB))
    return tril.astype(F32), diag


def _chunk_core(q, k, v, g, st, tril, diag):
    dk = q.shape[1]
    cum = jnp.dot(tril, g, precision=lax.Precision.HIGHEST, preferred_element_type=F32)
    cs = [cum[SUB * i:SUB * (i + 1), :] for i in range(N_SUB)]
    end = [c[SUB - 1:SUB, :] for c in cs]
    mid = [c[SUB // 2 - 1:SUB // 2, :] for c in cs]
    beg = [jnp.zeros((1, dk), F32)] + end[:-1]
    qb = [q[SUB * i:SUB * (i + 1), :] * jnp.exp(cs[i] - beg[i]) for i in range(N_SUB)]
    ke = [k[SUB * i:SUB * (i + 1), :] * jnp.exp(end[i] - cs[i]) for i in range(N_SUB)]

    qm = jnp.concatenate([qb[i] * jnp.exp(beg[i] - mid[i]) for i in range(N_SUB)], axis=0)
    km = jnp.concatenate([ke[i] * jnp.exp(mid[i] - end[i]) for i in range(N_SUB)], axis=0)
    s_diag = lax.dot_general(qm.astype(BF16), km.astype(BF16), NT_DIMS, preferred_element_type=F32)

    zero = jnp.zeros((SUB, dk), F32)
    lhs, rhs = [], []
    for i in range(1, N_SUB):
        lhs.append(jnp.concatenate([qb[r] if r == i else zero for r in range(N_SUB)], axis=0))
        cols = []
        for j in range(N_SUB):
            if j >= i:
                cols.append(zero)
            elif j == i - 1:
                cols.append(ke[j])
            else:
                cols.append(ke[j] * jnp.exp(beg[i] - end[j]))
        rhs.append(jnp.concatenate(cols, axis=0))
    lhs = jnp.concatenate(lhs, axis=1).astype(BF16)
    rhs = jnp.concatenate(rhs, axis=1).astype(BF16)
    s_off = lax.dot_general(lhs, rhs, NT_DIMS, preferred_element_type=F32)

    scores = jnp.where(diag, s_diag, 0.0) + s_off
    vb = v.astype(BF16)
    intra = jnp.dot(scores.astype(BF16), vb, preferred_element_type=F32)

    qc = jnp.concatenate([qb[i] * jnp.exp(beg[i]) for i in range(N_SUB)], axis=0)
    inter = lax.dot_general(qc.astype(BF16), st.astype(BF16), NT_DIMS, preferred_element_type=F32)

    last = end[-1]
    kl = jnp.concatenate(
        [ke[j] * jnp.exp(last - end[j]) for j in range(N_SUB - 1)] + [ke[-1]], axis=0)
    upd = lax.dot_general(vb, kl.astype(BF16), TN_DIMS, preferred_element_type=F32)
    st_new = st * jnp.exp(last) + upd
    return inter + intra, st_new


def _gated_head_norm(o, gain, gate_raw):
    ms = jnp.mean(o * o, axis=-1, keepdims=True)
    return o * lax.rsqrt(ms + EPS) * gain * (gate_raw * _sigmoid(gate_raw))


def _hgrn_kernel(layer_j, q_ref, f_ref, i_ref, g_ref, lbl_ref, on_ref, y_ref, st_ref):
    @pl.when(pl.program_id(2) == 0)
    def _():
        st_ref[...] = jnp.zeros_like(st_ref)

    logits = lbl_ref[...]
    ex = jnp.exp(logits - jnp.max(logits, axis=0, keepdims=True))
    sm = ex / jnp.sum(ex, axis=0, keepdims=True)
    lb = jnp.zeros((1, sm.shape[1]), F32)
    for r in range(1, layer_j + 1):
        lb = lb + sm[r:r + 1, :]

    tril, diag = _chunk_masks()
    gain = on_ref[...]

    def body(c, carry):
        rows = pl.ds(pl.multiple_of(c * CHUNK, CHUNK), CHUNK)
        q_raw = q_ref[rows, :].astype(F32)
        f = lb + (1.0 - lb) * _sigmoid(f_ref[rows, :].astype(F32))
        o, st_new = _chunk_core(q_raw * _sigmoid(q_raw), 1.0 - f, i_ref[rows, :].astype(F32),
                                jnp.log(f), st_ref[...], tril, diag)
        st_ref[...] = st_new
        y_ref[rows, :] = _gated_head_norm(o, gain, g_ref[rows, :].astype(F32)).astype(y_ref.dtype)
        return carry

    lax.fori_loop(0, q_ref.shape[0] // CHUNK, body, 0)


def _hgrn_mixer(proj, lb_logits, onorm, layer_j, batch, t_blk):
    m = proj.shape[0]
    d = HGRN_HEAD_DIM
    mix_width = lb_logits.shape[1]
    heads = mix_width // d
    n_t = m // batch // t_blk
    n_a = lb_logits.shape[0]

    def col(group):
        return pl.BlockSpec((t_blk, d), lambda b, h, t: (b * n_t + t, group * heads + h))

    return pl.pallas_call(
        functools.partial(_hgrn_kernel, layer_j),
        grid=(batch, heads, n_t),
        in_specs=[col(0), col(1), col(2), col(3),
                  pl.BlockSpec((n_a, d), lambda b, h, t: (0, h)),
                  pl.BlockSpec((1, d), lambda b, h, t: (0, 0))],
        out_specs=pl.BlockSpec((t_blk, d), lambda b, h, t: (b * n_t + t, h)),
        out_shape=jax.ShapeDtypeStruct((m, mix_width), BF16),
        scratch_shapes=[pltpu.VMEM((d, d), F32)],
        compiler_params=_params("parallel", "parallel", "arbitrary"),
        name="hgrn2_mixer",
    )(proj, proj, proj, proj, lb_logits, onorm.reshape(1, d))


def _gla_kernel(dk_true, v_ref, g_ref, q_ref, k_ref, gl_ref, wgk_ref, bgk_ref, on_ref, y_ref, st_ref):
    @pl.when(pl.program_id(2) == 0)
    def _():
        st_ref[...] = jnp.zeros_like(st_ref)

    tril, diag = _chunk_masks()
    gain = on_ref[...]
    wgk = wgk_ref[...]
    bgk = bgk_ref[...]

    def body(c, carry):
        rows = pl.ds(pl.multiple_of(c * CHUNK, CHUNK), CHUNK)
        x = jnp.dot(gl_ref[rows, :], wgk, preferred_element_type=F32) + bgk
        log_a = (jnp.minimum(x, 0.0) - jnp.log1p(jnp.exp(-jnp.abs(x)))) / GLA_GATE_NORMALIZER
        q = q_ref[rows, :].astype(F32) * (dk_true ** -0.5)
        o, st_new = _chunk_core(q, k_ref[rows, :].astype(F32), v_ref[rows, :].astype(F32),
                                log_a, st_ref[...], tril, diag)
        st_ref[...] = st_new
        y_ref[rows, :] = _gated_head_norm(o, gain, g_ref[rows, :].astype(F32)).astype(y_ref.dtype)
        return carry

    lax.fori_loop(0, q_ref.shape[0] // CHUNK, body, 0)


def _gla_mixer(proj, w_gk_pad, b_gk_pad, onorm, dk_true, batch, t_blk):
    m = proj.shape[0]
    heads = GLA_HEADS
    dv = onorm.shape[0]
    dkp = GLA_DK_PAD
    n_t = m // batch // t_blk
    q0 = (2 * heads * dv) // dkp
    gl0 = (2 * heads * dv + 2 * heads * dkp) // LANE

    def rows(width, block0):
        return pl.BlockSpec((t_blk, width), lambda b, h, t: (b * n_t + t, block0 + h))

    return pl.pallas_call(
        functools.partial(_gla_kernel, dk_true),
        grid=(batch, heads, n_t),
        in_specs=[rows(dv, 0), rows(dv, heads), rows(dkp, q0), rows(dkp, q0 + heads),
                  pl.BlockSpec((t_blk, LANE), lambda b, h, t: (b * n_t + t, gl0)),
                  pl.BlockSpec((LANE, dkp), lambda b, h, t: (0, h)),
                  pl.BlockSpec((1, dkp), lambda b, h, t: (0, h)),
                  pl.BlockSpec((1, dv), lambda b, h, t: (0, 0))],
        out_specs=pl.BlockSpec((t_blk, dv), lambda b, h, t: (b * n_t + t, h)),
        out_shape=jax.ShapeDtypeStruct((m, heads * dv), BF16),
        scratch_shapes=[pltpu.VMEM((dv, dkp), F32)],
        compiler_params=_params("parallel", "parallel", "arbitrary"),
        name="gla_mixer",
    )(proj, proj, proj, proj, proj, w_gk_pad, b_gk_pad, onorm.reshape(1, dv))


def _gla_pack_weights(w_in, w_gk, b_gk, mix_width, xa_width):
    heads, rank, dkp = GLA_HEADS, GLA_GATE_RANK, GLA_DK_PAD
    d_model = w_in.shape[0]
    kw = w_gk.shape[1]
    dk = kw // heads
    q, k, v, g, gl, xq = jnp.split(
        w_in, [kw, 2 * kw, 2 * kw + mix_width, 2 * kw + 2 * mix_width, 2 * kw + 2 * mix_width + rank],
        axis=1)

    def pad_heads(w):
        w = w.reshape(w.shape[0], heads, dk)
        return jnp.pad(w, ((0, 0), (0, 0), (0, dkp - dk))).reshape(w.shape[0], heads * dkp)

    gl = jnp.pad(gl, ((0, 0), (0, LANE - rank)))
    w_packed = jnp.concatenate([v, g, pad_heads(q), pad_heads(k), gl, xq], axis=1).astype(BF16)
    w_gk_pad = jnp.pad(pad_heads(w_gk), ((0, LANE - rank), (0, 0))).astype(BF16)
    b_gk_pad = pad_heads(b_gk.reshape(1, kw)).astype(F32)
    assert w_packed.shape == (d_model, 2 * mix_width + 2 * heads * dkp + LANE + xa_width)
    return w_packed, w_gk_pad, b_gk_pad, dk


def kernel(x, mem, norm_mix, norm_ffn, norm_mem, norm_final, hgrn_w_in, hgrn_lb_logits, hgrn_onorm,
           gla_w_in, gla_w_gk, gla_b_gk, gla_onorm, w_mem_kv, w_out, w_gate_up, w_down):
    batch, seq, d_model = x.shape
    n_mem = mem.shape[1]
    depth = norm_mix.shape[0]
    mix_width = hgrn_lb_logits.shape[1]
    xa_width = d_model - mix_width
    m = batch * seq

    h = x.reshape(m, d_model)
    mem2 = mem.reshape(batch * n_mem, d_model)

    for layer in range(depth):
        j = layer // 2
        kv = _norm_matmul(mem2, norm_mem, w_mem_kv[layer].astype(BF16), tm=batch * n_mem, tn=2 * xa_width)
        kv = kv.reshape(batch, n_mem, 2 * xa_width)
        if layer % 2 == 0:
            proj = _norm_matmul(h, norm_mix[layer], hgrn_w_in[j].astype(BF16), tm=1024, tn=512)
            y = _hgrn_mixer(proj, hgrn_lb_logits, hgrn_onorm[j], j, batch, t_blk=512)
            xq_block0 = (4 * mix_width) // LANE
        else:
            w_packed, w_gk_pad, b_gk_pad, dk = _gla_pack_weights(
                gla_w_in[j], gla_w_gk[j], gla_b_gk[j], mix_width, xa_width)
            proj = _norm_matmul(h, norm_mix[layer], w_packed, tm=1024, tn=1152)
            y = _gla_mixer(proj, w_gk_pad, b_gk_pad, gla_onorm[j], dk, batch, t_blk=512)
            xq_block0 = (w_packed.shape[1] - xa_width) // LANE
        xa = _xattn(proj, xq_block0, kv, batch, tq=512)
        h = _out_proj(y, xa, w_out[layer].astype(BF16), h, tm=512)
        act = _ffn_up(h, norm_ffn[layer], w_gate_up[layer].astype(BF16), tm=1024, tn=512)
        h = _matmul_residual(act, w_down[layer].astype(BF16), h, tm=1024, tn=512)

    out = _rmsnorm(h, norm_final, tm=512)
    return out.reshape(batch, seq, d_model)
```

```python
import functools

import jax
import jax.numpy as jnp
from jax import lax
from jax.experimental import pallas as pl
from jax.experimental.pallas import tpu as pltpu

F32 = jnp.float32
BF16 = jnp.bfloat16

EPS = 1e-6
CHUNK = 64
SUB = 16
N_SUB = CHUNK // SUB
XA_HEADS = 4
HGRN_HEAD_DIM = 128
GLA_HEADS = 4
GLA_GATE_RANK = 16
GLA_GATE_NORMALIZER = 16.0
LANE = 128
MXU_COLS = 256
GLA_DK_PAD = 256
V7X_VMEM_LIMIT = 56 * 1024 * 1024

NT_DIMS = (((1,), (1,)), ((), ()))
TN_DIMS = (((0,), (0,)), ((), ()))

RESIDENT = pl.Buffered(1)


def _params(*semantics):
    return pltpu.CompilerParams(dimension_semantics=semantics, vmem_limit_bytes=V7X_VMEM_LIMIT)


def _sigmoid(x):
    return 1.0 / (1.0 + jnp.exp(-x))


def _rmsnorm_rows(x, gain):
    ms = jnp.mean(x * x, axis=-1, keepdims=True)
    return x * lax.rsqrt(ms + EPS) * gain


def _in_proj_kernel(col_splits, x_ref, g_ref, w_ref, o_ref):
    xn = _rmsnorm_rows(x_ref[...], g_ref[...]).astype(BF16)
    for lo, hi in zip(col_splits[:-1], col_splits[1:]):
        o_ref[:, lo:hi] = jnp.dot(xn, w_ref[:, lo:hi], preferred_element_type=F32).astype(o_ref.dtype)


def _in_proj(x, gain, w_stack, layer, col_splits, tm):
    m, k = x.shape
    n = w_stack.shape[2]
    return pl.pallas_call(
        functools.partial(_in_proj_kernel, col_splits),
        grid=(m // tm,),
        in_specs=[
            pl.BlockSpec((tm, k), lambda i: (i, 0)),
            pl.BlockSpec((1, k), lambda i: (0, 0)),
            pl.BlockSpec((None, k, n), lambda i: (layer, 0, 0), pipeline_mode=RESIDENT),
        ],
        out_specs=pl.BlockSpec((tm, n), lambda i: (i, 0)),
        out_shape=jax.ShapeDtypeStruct((m, n), BF16),
        compiler_params=_params("parallel"),
        name="in_proj",
    )(x, gain.reshape(1, k), w_stack)


def _mem_kv_kernel(x_ref, g_ref, w_ref, o_ref):
    xn = _rmsnorm_rows(x_ref[...], g_ref[...]).astype(BF16)
    o_ref[...] = jnp.dot(xn, w_ref[...], preferred_element_type=F32).astype(o_ref.dtype)


def _mem_kv(mem2, gain, w_stack):
    m, k = mem2.shape
    depth, _, n = w_stack.shape
    return pl.pallas_call(
        _mem_kv_kernel,
        grid=(depth,),
        in_specs=[
            pl.BlockSpec((m, k), lambda l: (0, 0)),
            pl.BlockSpec((1, k), lambda l: (0, 0)),
            pl.BlockSpec((None, k, n), lambda l: (l, 0, 0)),
        ],
        out_specs=pl.BlockSpec((None, m, n), lambda l: (l, 0, 0)),
        out_shape=jax.ShapeDtypeStruct((depth, m, n), BF16),
        compiler_params=_params("parallel"),
        name="mem_kv_proj",
    )(mem2, gain.reshape(1, k), w_stack)


def _ffn_up_kernel(xn_ref, wg_ref, wu_ref, o_ref):
    xn = xn_ref[...]
    gate = jnp.dot(xn, wg_ref[...], preferred_element_type=F32)
    up = jnp.dot(xn, wu_ref[...], preferred_element_type=F32)
    o_ref[...] = (gate * _sigmoid(gate) * up).astype(o_ref.dtype)


def _ffn_up(xn, w_stack, layer, tm, tn):
    m, k = xn.shape
    d_ff = w_stack.shape[2] // 2
    n_blocks = d_ff // tn
    return pl.pallas_call(
        _ffn_up_kernel,
        grid=(m // tm, n_blocks),
        in_specs=[
            pl.BlockSpec((tm, k), lambda i, j: (i, 0)),
            pl.BlockSpec((None, k, tn), lambda i, j: (layer, 0, j)),
            pl.BlockSpec((None, k, tn), lambda i, j: (layer, 0, j + n_blocks)),
        ],
        out_specs=pl.BlockSpec((tm, tn), lambda i, j: (i, j)),
        out_shape=jax.ShapeDtypeStruct((m, d_ff), BF16),
        compiler_params=_params("parallel", "arbitrary"),
        name="ffn_up",
    )(xn, w_stack, w_stack)


def _matmul_residual_kernel(a_ref, w_ref, r_ref, o_ref):
    o_ref[...] = r_ref[...] + jnp.dot(a_ref[...], w_ref[...], preferred_element_type=F32)


def _matmul_residual(a, w_stack, layer, res, tm, tn):
    m, k = a.shape
    n = w_stack.shape[2]
    return pl.pallas_call(
        _matmul_residual_kernel,
        grid=(m // tm, n // tn),
        in_specs=[
            pl.BlockSpec((tm, k), lambda i, j: (i, 0)),
            pl.BlockSpec((None, k, tn), lambda i, j: (layer, 0, j)),
            pl.BlockSpec((tm, tn), lambda i, j: (i, j)),
        ],
        out_specs=pl.BlockSpec((tm, tn), lambda i, j: (i, j)),
        out_shape=jax.ShapeDtypeStruct((m, n), F32),
        compiler_params=_params("parallel", "arbitrary"),
        name="matmul_residual",
    )(a, w_stack, res)


def _out_proj_kernel(y_ref, xa_ref, w_ref, r_ref, g_ref, h_ref, hn_ref):
    wy = y_ref.shape[1]
    acc = jnp.dot(y_ref[...], w_ref[:wy, :], preferred_element_type=F32)
    acc += jnp.dot(xa_ref[...], w_ref[wy:, :], preferred_element_type=F32)
    h = r_ref[...] + acc
    h_ref[...] = h
    hn_ref[...] = _rmsnorm_rows(h, g_ref[...]).astype(hn_ref.dtype)


def _out_proj(y, xa, w_stack, layer, res, gain, tm):
    m, wy = y.shape
    wx = xa.shape[1]
    _, k, n = w_stack.shape
    return pl.pallas_call(
        _out_proj_kernel,
        grid=(m // tm,),
        in_specs=[
            pl.BlockSpec((tm, wy), lambda i: (i, 0)),
            pl.BlockSpec((tm, wx), lambda i: (i, 0)),
            pl.BlockSpec((None, k, n), lambda i: (layer, 0, 0), pipeline_mode=RESIDENT),
            pl.BlockSpec((tm, n), lambda i: (i, 0)),
            pl.BlockSpec((1, n), lambda i: (0, 0)),
        ],
        out_specs=[pl.BlockSpec((tm, n), lambda i: (i, 0)), pl.BlockSpec((tm, n), lambda i: (i, 0))],
        out_shape=[jax.ShapeDtypeStruct((m, n), F32), jax.ShapeDtypeStruct((m, n), BF16)],
        compiler_params=_params("parallel"),
        name="out_proj",
    )(y, xa, w_stack, res, gain.reshape(1, n))


def _rmsnorm_kernel(x_ref, g_ref, o_ref):
    o_ref[...] = _rmsnorm_rows(x_ref[...], g_ref[...])


def _rmsnorm(x, gain, tm):
    m, k = x.shape
    return pl.pallas_call(
        _rmsnorm_kernel,
        grid=(m // tm,),
        in_specs=[pl.BlockSpec((tm, k), lambda i: (i, 0)), pl.BlockSpec((1, k), lambda i: (0, 0))],
        out_specs=pl.BlockSpec((tm, k), lambda i: (i, 0)),
        out_shape=jax.ShapeDtypeStruct((m, k), F32),
        compiler_params=_params("parallel"),
        name="final_rmsnorm",
    )(x, gain.reshape(1, k))


def _xattn_kernel(q_ref, k_ref, v_ref, o_ref):
    d = q_ref.shape[1]
    s = lax.dot_general(q_ref[...], k_ref[...], NT_DIMS, preferred_element_type=F32) * (d ** -0.5)
    p = jnp.exp(s - jnp.max(s, axis=-1, keepdims=True))
    denom = jnp.sum(p, axis=-1, keepdims=True)
    o = jnp.dot(p.astype(BF16), v_ref[...], preferred_element_type=F32)
    o_ref[...] = (o / denom).astype(o_ref.dtype)


def _xattn(proj, xq_block0, kv_stack, layer, batch, tq):
    m = proj.shape[0]
    n_t = m // batch // tq
    n_mem = kv_stack.shape[1]
    d = kv_stack.shape[2] // (2 * XA_HEADS)
    return pl.pallas_call(
        _xattn_kernel,
        grid=(batch, n_t, XA_HEADS),
        in_specs=[
            pl.BlockSpec((tq, d), lambda b, t, h: (b * n_t + t, xq_block0 + h)),
            pl.BlockSpec((None, n_mem, d), lambda b, t, h: (layer * batch + b, 0, h)),
            pl.BlockSpec((None, n_mem, d), lambda b, t, h: (layer * batch + b, 0, XA_HEADS + h)),
        ],
        out_specs=pl.BlockSpec((tq, d), lambda b, t, h: (b * n_t + t, h)),
        out_shape=jax.ShapeDtypeStruct((m, XA_HEADS * d), BF16),
        compiler_params=_params("parallel", "parallel", "arbitrary"),
        name="mem_xattn",
    )(proj, kv_stack, kv_stack)


def _chunk_masks():
    t = lax.broadcasted_iota(jnp.int32, (CHUNK, CHUNK), 0)
    s = lax.broadcasted_iota(jnp.int32, (CHUNK, CHUNK), 1)
    tril = (s <= t)
    diag = tril & ((t // SUB) == (s // SUB))
    return tril.astype(BF16), diag


def _chunk_cumsum(tril, g):
    dk = g.shape[1]
    hi = g.astype(BF16)
    lo = (g - hi.astype(F32)).astype(BF16)
    both = jnp.dot(tril, jnp.concatenate([hi, lo], axis=1), preferred_element_type=F32)
    return both[:, :dk] + both[:, dk:]


def _chunk_core(q, k, v, g, st, tril, diag):
    dk = q.shape[1]
    cum = _chunk_cumsum(tril, g)
    cs = [cum[SUB * i:SUB * (i + 1), :] for i in range(N_SUB)]
    end = [c[SUB - 1:SUB, :] for c in cs]
    mid = [c[SUB // 2 - 1:SUB // 2, :] for c in cs]
    beg = [jnp.zeros((1, dk), F32)] + end[:-1]
    qb = [q[SUB * i:SUB * (i + 1), :] * jnp.exp(cs[i] - beg[i]) for i in range(N_SUB)]
    ke = [k[SUB * i:SUB * (i + 1), :] * jnp.exp(end[i] - cs[i]) for i in range(N_SUB)]

    qm = jnp.concatenate([qb[i] * jnp.exp(beg[i] - mid[i]) for i in range(N_SUB)], axis=0)
    km = jnp.concatenate([ke[i] * jnp.exp(mid[i] - end[i]) for i in range(N_SUB)], axis=0)
    s_diag = lax.dot_general(qm.astype(BF16), km.astype(BF16), NT_DIMS, preferred_element_type=F32)

    zero = jnp.zeros((SUB, dk), F32)
    lhs, rhs = [], []
    for i in range(1, N_SUB):
        lhs.append(jnp.concatenate([qb[r] if r == i else zero for r in range(N_SUB)], axis=0))
        cols = []
        for j in range(N_SUB):
            if j >= i:
                cols.append(zero)
            elif j == i - 1:
                cols.append(ke[j])
            else:
                cols.append(ke[j] * jnp.exp(beg[i] - end[j]))
        rhs.append(jnp.concatenate(cols, axis=0))
    lhs = jnp.concatenate(lhs, axis=1).astype(BF16)
    rhs = jnp.concatenate(rhs, axis=1).astype(BF16)
    s_off = lax.dot_general(lhs, rhs, NT_DIMS, preferred_element_type=F32)

    scores = jnp.where(diag, s_diag, 0.0) + s_off
    vb = v.astype(BF16)
    intra = jnp.dot(scores.astype(BF16), vb, preferred_element_type=F32)

    qc = jnp.concatenate([qb[i] * jnp.exp(beg[i]) for i in range(N_SUB)], axis=0)
    inter = lax.dot_general(qc.astype(BF16), st.astype(BF16), NT_DIMS, preferred_element_type=F32)

    last = end[-1]
    kl = jnp.concatenate(
        [ke[j] * jnp.exp(last - end[j]) for j in range(N_SUB - 1)] + [ke[-1]], axis=0)
    upd = lax.dot_general(vb, kl.astype(BF16), TN_DIMS, preferred_element_type=F32)
    st_new = st * jnp.exp(last) + upd
    return inter + intra, st_new


def _gated_head_norm(o, gain, gate_raw):
    ms = jnp.mean(o * o, axis=-1, keepdims=True)
    return o * lax.rsqrt(ms + EPS) * gain * (gate_raw * _sigmoid(gate_raw))


def _hgrn_kernel(layer_j, q_ref, f_ref, i_ref, g_ref, lbl_ref, on_ref, y_ref, st_ref):
    @pl.when(pl.program_id(2) == 0)
    def _():
        st_ref[...] = jnp.zeros_like(st_ref)

    logits = lbl_ref[...]
    ex = jnp.exp(logits - jnp.max(logits, axis=0, keepdims=True))
    sm = ex / jnp.sum(ex, axis=0, keepdims=True)
    lb = jnp.zeros((1, sm.shape[1]), F32)
    for r in range(1, layer_j + 1):
        lb = lb + sm[r:r + 1, :]

    tril, diag = _chunk_masks()
    gain = on_ref[...]

    st = st_ref[...]
    for c in range(q_ref.shape[0] // CHUNK):
        rows = slice(c * CHUNK, (c + 1) * CHUNK)
        q_raw = q_ref[rows, :].astype(F32)
        f = lb + (1.0 - lb) * _sigmoid(f_ref[rows, :].astype(F32))
        o, st = _chunk_core(q_raw * _sigmoid(q_raw), 1.0 - f, i_ref[rows, :].astype(F32),
                            jnp.log(f), st, tril, diag)
        y_ref[rows, :] = _gated_head_norm(o, gain, g_ref[rows, :].astype(F32)).astype(y_ref.dtype)
    st_ref[...] = st


def _hgrn_mixer(proj, lb_logits, onorm, layer_j, batch, t_blk):
    m = proj.shape[0]
    d = HGRN_HEAD_DIM
    mix_width = lb_logits.shape[1]
    heads = mix_width // d
    n_t = m // batch // t_blk
    n_a = lb_logits.shape[0]

    def col(group):
        return pl.BlockSpec((t_blk, d), lambda b, h, t: (b * n_t + t, group * heads + h))

    return pl.pallas_call(
        functools.partial(_hgrn_kernel, layer_j),
        grid=(batch, heads, n_t),
        in_specs=[col(0), col(1), col(2), col(3),
                  pl.BlockSpec((n_a, d), lambda b, h, t: (0, h)),
                  pl.BlockSpec((1, d), lambda b, h, t: (0, 0))],
        out_specs=pl.BlockSpec((t_blk, d), lambda b, h, t: (b * n_t + t, h)),
        out_shape=jax.ShapeDtypeStruct((m, mix_width), BF16),
        scratch_shapes=[pltpu.VMEM((d, d), F32)],
        compiler_params=_params("parallel", "parallel", "arbitrary"),
        name="hgrn2_mixer",
    )(proj, proj, proj, proj, lb_logits, onorm.reshape(1, d))


def _gla_kernel(dk_true, v_ref, g_ref, q_ref, k_ref, gl_ref, wgk_ref, bgk_ref, on_ref, y_ref, st_ref):
    @pl.when(pl.program_id(2) == 0)
    def _():
        st_ref[...] = jnp.zeros_like(st_ref)

    tril, diag = _chunk_masks()
    gain = on_ref[...]
    wgk = wgk_ref[...]
    bgk = bgk_ref[...]

    st = st_ref[...]
    for c in range(q_ref.shape[0] // CHUNK):
        rows = slice(c * CHUNK, (c + 1) * CHUNK)
        x = jnp.dot(gl_ref[rows, :], wgk, preferred_element_type=F32) + bgk
        log_a = (jnp.minimum(x, 0.0) - jnp.log1p(jnp.exp(-jnp.abs(x)))) / GLA_GATE_NORMALIZER
        q = q_ref[rows, :].astype(F32) * (dk_true ** -0.5)
        o, st = _chunk_core(q, k_ref[rows, :].astype(F32), v_ref[rows, :].astype(F32),
                            log_a, st, tril, diag)
        y_ref[rows, :] = _gated_head_norm(o, gain, g_ref[rows, :].astype(F32)).astype(y_ref.dtype)
    st_ref[...] = st


def _gla_mixer(proj, w_gk_pad, b_gk_pad, onorm, dk_true, batch, t_blk):
    m = proj.shape[0]
    heads = GLA_HEADS
    dv = onorm.shape[0]
    dkp = GLA_DK_PAD
    n_t = m // batch // t_blk
    q0 = (2 * heads * dv) // dkp
    gl0 = (2 * heads * dv + 2 * heads * dkp) // LANE

    def rows(width, block0):
        return pl.BlockSpec((t_blk, width), lambda b, h, t: (b * n_t + t, block0 + h))

    return pl.pallas_call(
        functools.partial(_gla_kernel, dk_true),
        grid=(batch, heads, n_t),
        in_specs=[rows(dv, 0), rows(dv, heads), rows(dkp, q0), rows(dkp, q0 + heads),
                  pl.BlockSpec((t_blk, LANE), lambda b, h, t: (b * n_t + t, gl0)),
                  pl.BlockSpec((LANE, dkp), lambda b, h, t: (0, h)),
                  pl.BlockSpec((1, dkp), lambda b, h, t: (0, h)),
                  pl.BlockSpec((1, dv), lambda b, h, t: (0, 0))],
        out_specs=pl.BlockSpec((t_blk, dv), lambda b, h, t: (b * n_t + t, h)),
        out_shape=jax.ShapeDtypeStruct((m, heads * dv), BF16),
        scratch_shapes=[pltpu.VMEM((dv, dkp), F32)],
        compiler_params=_params("parallel", "parallel", "arbitrary"),
        name="gla_mixer",
    )(proj, proj, proj, proj, proj, w_gk_pad, b_gk_pad, onorm.reshape(1, dv))


def _gla_pack_weights(w_in, w_gk, b_gk, mix_width, xa_width):
    heads, rank, dkp = GLA_HEADS, GLA_GATE_RANK, GLA_DK_PAD
    d_model = w_in.shape[0]
    kw = w_gk.shape[1]
    dk = kw // heads
    q, k, v, g, gl, xq = jnp.split(
        w_in, [kw, 2 * kw, 2 * kw + mix_width, 2 * kw + 2 * mix_width, 2 * kw + 2 * mix_width + rank],
        axis=1)

    def pad_heads(w):
        w = w.reshape(w.shape[0], heads, dk)
        return jnp.pad(w, ((0, 0), (0, 0), (0, dkp - dk))).reshape(w.shape[0], heads * dkp)

    gl = jnp.pad(gl, ((0, 0), (0, LANE - rank)))
    w_packed = jnp.concatenate([v, g, pad_heads(q), pad_heads(k), gl, xq], axis=1).astype(BF16)
    w_gk_pad = jnp.pad(pad_heads(w_gk), ((0, LANE - rank), (0, 0))).astype(BF16)
    b_gk_pad = pad_heads(b_gk.reshape(1, kw)).astype(F32)
    col_splits = (0, mix_width, 2 * mix_width, 2 * mix_width + heads * dkp,
                  2 * mix_width + 2 * heads * dkp, w_packed.shape[1])
    assert w_packed.shape == (d_model, 2 * mix_width + 2 * heads * dkp + LANE + xa_width)
    return w_packed, w_gk_pad, b_gk_pad, dk, col_splits


def kernel(x, mem, norm_mix, norm_ffn, norm_mem, norm_final, hgrn_w_in, hgrn_lb_logits, hgrn_onorm,
           gla_w_in, gla_w_gk, gla_b_gk, gla_onorm, w_mem_kv, w_out, w_gate_up, w_down):
    batch, seq, d_model = x.shape
    n_mem = mem.shape[1]
    depth = norm_mix.shape[0]
    mix_width = hgrn_lb_logits.shape[1]
    xa_width = d_model - mix_width
    m = batch * seq

    h = x.reshape(m, d_model)
    kv = _mem_kv(mem.reshape(batch * n_mem, d_model), norm_mem, w_mem_kv.astype(BF16))
    kv = kv.reshape(depth * batch, n_mem, 2 * xa_width)

    hgrn_w = hgrn_w_in.astype(BF16)
    hgrn_splits = tuple(range(0, 4 * mix_width + 1, mix_width)) + (4 * mix_width + xa_width,)
    w_out_b = w_out.astype(BF16)
    w_gate_up_b = w_gate_up.astype(BF16)
    w_down_b = w_down.astype(BF16)

    for layer in range(depth):
        j = layer // 2
        if layer % 2 == 0:
            proj = _in_proj(h, norm_mix[layer], hgrn_w, j, hgrn_splits, tm=512)
            y = _hgrn_mixer(proj, hgrn_lb_logits, hgrn_onorm[j], j, batch, t_blk=512)
            xq_block0 = (4 * mix_width) // LANE
        else:
            w_packed, w_gk_pad, b_gk_pad, dk, gla_splits = _gla_pack_weights(
                gla_w_in[j], gla_w_gk[j], gla_b_gk[j], mix_width, xa_width)
            proj = _in_proj(h, norm_mix[layer], w_packed[None], 0, gla_splits, tm=512)
            y = _gla_mixer(proj, w_gk_pad, b_gk_pad, gla_onorm[j], dk, batch, t_blk=512)
            xq_block0 = (w_packed.shape[1] - xa_width) // LANE
        xa = _xattn(proj, xq_block0, kv, layer, batch, tq=512)
        h, hn = _out_proj(y, xa, w_out_b, layer, h, norm_ffn[layer], tm=512)
        act = _ffn_up(hn, w_gate_up_b, layer, tm=2048, tn=512)
        h = _matmul_residual(act, w_down_b, layer, h, tm=1024, tn=512)

    out = _rmsnorm(h, norm_final, tm=512)
    return out.reshape(batch, seq, d_model)
```

```python
import functools

import jax
import jax.numpy as jnp
from jax import lax
from jax.experimental import pallas as pl
from jax.experimental.pallas import tpu as pltpu

F32 = jnp.float32
BF16 = jnp.bfloat16

EPS = 1e-6
CHUNK = 64
HGRN_SUB = 32
GLA_SUB = 64
XA_HEADS = 4
HGRN_HEAD_DIM = 128
GLA_HEADS = 4
GLA_GATE_RANK = 16
GLA_GATE_NORMALIZER = 16.0
LANE = 128
GLA_DK_PAD = 256
V7X_VMEM_LIMIT = 56 * 1024 * 1024

NT_DIMS = (((1,), (1,)), ((), ()))
TN_DIMS = (((0,), (0,)), ((), ()))

RESIDENT = pl.Buffered(1)


def _params(*semantics):
    return pltpu.CompilerParams(dimension_semantics=semantics, vmem_limit_bytes=V7X_VMEM_LIMIT)


def _sigmoid(x):
    return 1.0 / (1.0 + jnp.exp(-x))


def _rmsnorm_rows(x, gain):
    ms = jnp.mean(x * x, axis=-1, keepdims=True)
    return x * lax.rsqrt(ms + EPS) * gain


def _in_proj_kernel(col_splits, x_ref, g_ref, w_ref, o_ref):
    xn = _rmsnorm_rows(x_ref[...], g_ref[...]).astype(BF16)
    for lo, hi in zip(col_splits[:-1], col_splits[1:]):
        o_ref[:, lo:hi] = jnp.dot(xn, w_ref[:, lo:hi], preferred_element_type=F32).astype(o_ref.dtype)


def _in_proj(x, gain, w_stack, layer, col_splits, tm):
    m, k = x.shape
    n = w_stack.shape[2]
    return pl.pallas_call(
        functools.partial(_in_proj_kernel, col_splits),
        grid=(m // tm,),
        in_specs=[
            pl.BlockSpec((tm, k), lambda i: (i, 0)),
            pl.BlockSpec((1, k), lambda i: (0, 0)),
            pl.BlockSpec((None, k, n), lambda i: (layer, 0, 0), pipeline_mode=RESIDENT),
        ],
        out_specs=pl.BlockSpec((tm, n), lambda i: (i, 0)),
        out_shape=jax.ShapeDtypeStruct((m, n), BF16),
        compiler_params=_params("parallel"),
        name="in_proj",
    )(x, gain.reshape(1, k), w_stack)


def _mem_kv_kernel(x_ref, g_ref, w_ref, o_ref):
    xn = _rmsnorm_rows(x_ref[...], g_ref[...]).astype(BF16)
    o_ref[...] = jnp.dot(xn, w_ref[...], preferred_element_type=F32).astype(o_ref.dtype)


def _mem_kv(mem2, gain, w_stack):
    m, k = mem2.shape
    depth, _, n = w_stack.shape
    return pl.pallas_call(
        _mem_kv_kernel,
        grid=(depth,),
        in_specs=[
            pl.BlockSpec((m, k), lambda l: (0, 0)),
            pl.BlockSpec((1, k), lambda l: (0, 0)),
            pl.BlockSpec((None, k, n), lambda l: (l, 0, 0)),
        ],
        out_specs=pl.BlockSpec((None, m, n), lambda l: (l, 0, 0)),
        out_shape=jax.ShapeDtypeStruct((depth, m, n), BF16),
        compiler_params=_params("parallel"),
        name="mem_kv_proj",
    )(mem2, gain.reshape(1, k), w_stack)


def _ffn_up_kernel(xn_ref, wg_ref, wu_ref, o_ref):
    xn = xn_ref[...]
    gate = jnp.dot(xn, wg_ref[...], preferred_element_type=F32)
    up = jnp.dot(xn, wu_ref[...], preferred_element_type=F32)
    o_ref[...] = (gate * _sigmoid(gate) * up).astype(o_ref.dtype)


def _ffn_up(xn, w_stack, layer, tm, tn):
    m, k = xn.shape
    d_ff = w_stack.shape[2] // 2
    n_blocks = d_ff // tn
    return pl.pallas_call(
        _ffn_up_kernel,
        grid=(m // tm, n_blocks),
        in_specs=[
            pl.BlockSpec((tm, k), lambda i, j: (i, 0)),
            pl.BlockSpec((None, k, tn), lambda i, j: (layer, 0, j)),
            pl.BlockSpec((None, k, tn), lambda i, j: (layer, 0, j + n_blocks)),
        ],
        out_specs=pl.BlockSpec((tm, tn), lambda i, j: (i, j)),
        out_shape=jax.ShapeDtypeStruct((m, d_ff), BF16),
        compiler_params=_params("parallel", "arbitrary"),
        name="ffn_up",
    )(xn, w_stack, w_stack)


def _matmul_residual_kernel(a_ref, w_ref, r_ref, o_ref):
    o_ref[...] = r_ref[...] + jnp.dot(a_ref[...], w_ref[...], preferred_element_type=F32)


def _matmul_residual(a, w_stack, layer, res, tm, tn):
    m, k = a.shape
    n = w_stack.shape[2]
    return pl.pallas_call(
        _matmul_residual_kernel,
        grid=(m // tm, n // tn),
        in_specs=[
            pl.BlockSpec((tm, k), lambda i, j: (i, 0)),
            pl.BlockSpec((None, k, tn), lambda i, j: (layer, 0, j)),
            pl.BlockSpec((tm, tn), lambda i, j: (i, j)),
        ],
        out_specs=pl.BlockSpec((tm, tn), lambda i, j: (i, j)),
        out_shape=jax.ShapeDtypeStruct((m, n), F32),
        compiler_params=_params("parallel", "arbitrary"),
        name="matmul_residual",
    )(a, w_stack, res)


def _out_proj_kernel(y_ref, xq_ref, kv_ref, w_ref, r_ref, g_ref, h_ref, hn_ref):
    wy = y_ref.shape[1]
    d = xq_ref.shape[1] // XA_HEADS
    acc = jnp.dot(y_ref[...], w_ref[:wy, :], preferred_element_type=F32)
    heads = []
    for hd in range(XA_HEADS):
        q = xq_ref[:, hd * d:(hd + 1) * d]
        k = kv_ref[:, hd * d:(hd + 1) * d]
        v = kv_ref[:, (XA_HEADS + hd) * d:(XA_HEADS + hd + 1) * d]
        s = lax.dot_general(q, k, NT_DIMS, preferred_element_type=F32) * (d ** -0.5)
        p = jnp.exp(s - jnp.max(s, axis=-1, keepdims=True))
        denom = jnp.sum(p, axis=-1, keepdims=True)
        o = jnp.dot(p.astype(BF16), v, preferred_element_type=F32)
        heads.append((o / denom).astype(BF16))
    acc += jnp.dot(jnp.concatenate(heads, axis=1), w_ref[wy:, :], preferred_element_type=F32)
    h = r_ref[...] + acc
    h_ref[...] = h
    hn_ref[...] = _rmsnorm_rows(h, g_ref[...]).astype(hn_ref.dtype)


def _out_proj(y, proj, xq_block, kv_stack, w_stack, layer, res, gain, batch, tm):
    m, wy = y.shape
    _, k, n = w_stack.shape
    wx = k - wy
    n_mem, wkv = kv_stack.shape[1:]
    steps_per_batch = m // batch // tm
    return pl.pallas_call(
        _out_proj_kernel,
        grid=(m // tm,),
        in_specs=[
            pl.BlockSpec((tm, wy), lambda i: (i, 0)),
            pl.BlockSpec((tm, wx), lambda i: (i, xq_block)),
            pl.BlockSpec((None, n_mem, wkv), lambda i: (layer * batch + i // steps_per_batch, 0, 0)),
            pl.BlockSpec((None, k, n), lambda i: (layer, 0, 0), pipeline_mode=RESIDENT),
            pl.BlockSpec((tm, n), lambda i: (i, 0)),
            pl.BlockSpec((1, n), lambda i: (0, 0)),
        ],
        out_specs=[pl.BlockSpec((tm, n), lambda i: (i, 0)), pl.BlockSpec((tm, n), lambda i: (i, 0))],
        out_shape=[jax.ShapeDtypeStruct((m, n), F32), jax.ShapeDtypeStruct((m, n), BF16)],
        compiler_params=_params("parallel"),
        name="xattn_out_proj",
    )(y, proj, kv_stack, w_stack, res, gain.reshape(1, n))


def _rmsnorm_kernel(x_ref, g_ref, o_ref):
    o_ref[...] = _rmsnorm_rows(x_ref[...], g_ref[...])


def _rmsnorm(x, gain, tm):
    m, k = x.shape
    return pl.pallas_call(
        _rmsnorm_kernel,
        grid=(m // tm,),
        in_specs=[pl.BlockSpec((tm, k), lambda i: (i, 0)), pl.BlockSpec((1, k), lambda i: (0, 0))],
        out_specs=pl.BlockSpec((tm, k), lambda i: (i, 0)),
        out_shape=jax.ShapeDtypeStruct((m, k), F32),
        compiler_params=_params("parallel"),
        name="final_rmsnorm",
    )(x, gain.reshape(1, k))


def _chunk_masks():
    t = lax.broadcasted_iota(jnp.int32, (CHUNK, CHUNK), 0)
    s = lax.broadcasted_iota(jnp.int32, (CHUNK, CHUNK), 1)
    tril = (s <= t)
    return tril.astype(BF16), tril


def _chunk_cumsum(tril_b, g):
    dk = g.shape[1]
    hi = g.astype(BF16)
    lo = (g - hi.astype(F32)).astype(BF16)
    both = jnp.dot(tril_b, jnp.concatenate([hi, lo], axis=1), preferred_element_type=F32)
    return both[:, :dk] + both[:, dk:]


def _chunk_scores(q, k, v, cum, sub):
    dk = q.shape[1]
    n_sub = CHUNK // sub
    blocks = [slice(sub * r, sub * (r + 1)) for r in range(n_sub)]
    mid = [cum[sub * j + sub // 2 - 1:sub * j + sub // 2, :] for j in range(n_sub)]
    last = cum[CHUNK - 1:CHUNK, :]
    zero = jnp.zeros((sub, dk), F32)

    q_slots, k_slots, k_last = [], [], []
    for j in range(n_sub):
        q_slots.append(jnp.concatenate(
            [zero if r < j else q[blocks[r], :] * jnp.exp(cum[blocks[r], :] - mid[j])
             for r in range(n_sub)], axis=0))
        k_mid = k[blocks[j], :] * jnp.exp(mid[j] - cum[blocks[j], :])
        k_slots.append(jnp.concatenate([k_mid if r == j else zero for r in range(n_sub)], axis=0))
        k_last.append(k_mid * jnp.exp(last - mid[j]))
    s_raw = lax.dot_general(jnp.concatenate(q_slots, axis=1).astype(BF16),
                            jnp.concatenate(k_slots, axis=1).astype(BF16),
                            NT_DIMS, preferred_element_type=F32)
    vb = v.astype(BF16)
    upd = lax.dot_general(vb, jnp.concatenate(k_last, axis=0).astype(BF16), TN_DIMS,
                          preferred_element_type=F32)
    q_dec = (q_slots[0] * jnp.exp(mid[0])).astype(BF16)
    return s_raw, upd, q_dec, jnp.exp(last), vb


def _chunk_readout(s_raw, tril, vb, q_dec, st):
    intra = jnp.dot(jnp.where(tril, s_raw, 0.0).astype(BF16), vb, preferred_element_type=F32)
    inter = lax.dot_general(q_dec, st.astype(BF16), NT_DIMS, preferred_element_type=F32)
    return intra + inter


def _gated_head_norm(o, gain, gate_raw):
    ms = jnp.mean(o * o, axis=-1, keepdims=True)
    return o * lax.rsqrt(ms + EPS) * gain * (gate_raw * _sigmoid(gate_raw))


def _run_staggered(n_chunks, stages):
    for slot in range(n_chunks + len(stages) - 1):
        for depth in reversed(range(len(stages))):
            c = slot - depth
            if 0 <= c < n_chunks:
                stages[depth](c)


def _hgrn_kernel(layer_j, q_ref, f_ref, i_ref, g_ref, lbl_ref, on_ref, y_ref, st_ref):
    @pl.when(pl.program_id(2) == 0)
    def _():
        st_ref[...] = jnp.zeros_like(st_ref)

    logits = lbl_ref[...]
    ex = jnp.exp(logits - jnp.max(logits, axis=0, keepdims=True))
    sm = ex / jnp.sum(ex, axis=0, keepdims=True)
    lb = jnp.zeros((1, sm.shape[1]), F32)
    for r in range(1, layer_j + 1):
        lb = lb + sm[r:r + 1, :]

    tril_b, tril = _chunk_masks()
    gain = on_ref[...]
    n_chunks = q_ref.shape[0] // CHUNK
    rows = [slice(c * CHUNK, (c + 1) * CHUNK) for c in range(n_chunks)]
    ctx = [{} for _ in range(n_chunks)]
    state = [st_ref[...]]

    def decay(c):
        q_raw = q_ref[rows[c], :].astype(F32)
        f = lb + (1.0 - lb) * _sigmoid(f_ref[rows[c], :].astype(F32))
        ctx[c].update(q=q_raw * _sigmoid(q_raw), k=1.0 - f, cum=_chunk_cumsum(tril_b, jnp.log(f)))

    def scores(c):
        x = ctx[c]
        x["parts"] = _chunk_scores(x.pop("q"), x.pop("k"), i_ref[rows[c], :].astype(F32),
                                   x.pop("cum"), HGRN_SUB)

    def readout(c):
        s_raw, upd, q_dec, e_last, vb = ctx[c].pop("parts")
        ctx[c]["o"] = _chunk_readout(s_raw, tril, vb, q_dec, state[0])
        state[0] = state[0] * e_last + upd

    def emit(c):
        o = ctx[c].pop("o")
        y_ref[rows[c], :] = _gated_head_norm(o, gain, g_ref[rows[c], :].astype(F32)).astype(y_ref.dtype)

    _run_staggered(n_chunks, [decay, scores, readout, emit])
    st_ref[...] = state[0]


def _hgrn_mixer(proj, lb_logits, onorm, layer_j, batch, t_blk):
    m = proj.shape[0]
    d = HGRN_HEAD_DIM
    mix_width = lb_logits.shape[1]
    heads = mix_width // d
    n_t = m // batch // t_blk
    n_a = lb_logits.shape[0]

    def col(group):
        return pl.BlockSpec((t_blk, d), lambda b, h, t: (b * n_t + t, group * heads + h))

    return pl.pallas_call(
        functools.partial(_hgrn_kernel, layer_j),
        grid=(batch, heads, n_t),
        in_specs=[col(0), col(1), col(2), col(3),
                  pl.BlockSpec((n_a, d), lambda b, h, t: (0, h)),
                  pl.BlockSpec((1, d), lambda b, h, t: (0, 0))],
        out_specs=pl.BlockSpec((t_blk, d), lambda b, h, t: (b * n_t + t, h)),
        out_shape=jax.ShapeDtypeStruct((m, mix_width), BF16),
        scratch_shapes=[pltpu.VMEM((d, d), F32)],
        compiler_params=_params("parallel", "parallel", "arbitrary"),
        name="hgrn2_mixer",
    )(proj, proj, proj, proj, lb_logits, onorm.reshape(1, d))


def _gla_kernel(dk_true, v_ref, g_ref, q_ref, k_ref, gl_ref, wgk_ref, bgk_ref, on_ref, y_ref, st_ref):
    @pl.when(pl.program_id(2) == 0)
    def _():
        st_ref[...] = jnp.zeros_like(st_ref)

    tril_b, tril = _chunk_masks()
    gain = on_ref[...]
    wgk = wgk_ref[...]
    bgk = bgk_ref[...]
    n_chunks = q_ref.shape[0] // CHUNK
    rows = [slice(c * CHUNK, (c + 1) * CHUNK) for c in range(n_chunks)]
    ctx = [{} for _ in range(n_chunks)]
    state = [st_ref[...]]

    def gate(c):
        ctx[c]["x"] = jnp.dot(gl_ref[rows[c], :], wgk, preferred_element_type=F32) + bgk

    def decay(c):
        x = ctx[c].pop("x")
        log_a = (jnp.minimum(x, 0.0) - jnp.log1p(jnp.exp(-jnp.abs(x)))) / GLA_GATE_NORMALIZER
        ctx[c]["cum"] = _chunk_cumsum(tril_b, log_a)

    def scores(c):
        q = q_ref[rows[c], :].astype(F32) * (dk_true ** -0.5)
        ctx[c]["parts"] = _chunk_scores(q, k_ref[rows[c], :].astype(F32), v_ref[rows[c], :].astype(F32),
                                        ctx[c].pop("cum"), GLA_SUB)

    def readout(c):
        s_raw, upd, q_dec, e_last, vb = ctx[c].pop("parts")
        ctx[c]["o"] = _chunk_readout(s_raw, tril, vb, q_dec, state[0])
        state[0] = state[0] * e_last + upd

    def emit(c):
        o = ctx[c].pop("o")
        y_ref[rows[c], :] = _gated_head_norm(o, gain, g_ref[rows[c], :].astype(F32)).astype(y_ref.dtype)

    _run_staggered(n_chunks, [gate, decay, scores, readout, emit])
    st_ref[...] = state[0]


def _gla_mixer(proj, w_gk_pad, b_gk_pad, onorm, dk_true, batch, t_blk):
    m = proj.shape[0]
    heads = GLA_HEADS
    dv = onorm.shape[0]
    dkp = GLA_DK_PAD
    n_t = m // batch // t_blk
    q0 = (2 * heads * dv) // dkp
    gl0 = proj.shape[1] // LANE - 1

    def rows(width, block0):
        return pl.BlockSpec((t_blk, width), lambda b, h, t: (b * n_t + t, block0 + h))

    return pl.pallas_call(
        functools.partial(_gla_kernel, dk_true),
        grid=(batch, heads, n_t),
        in_specs=[rows(dv, 0), rows(dv, heads), rows(dkp, q0), rows(dkp, q0 + heads),
                  pl.BlockSpec((t_blk, LANE), lambda b, h, t: (b * n_t + t, gl0)),
                  pl.BlockSpec((LANE, dkp), lambda b, h, t: (0, h)),
                  pl.BlockSpec((1, dkp), lambda b, h, t: (0, h)),
                  pl.BlockSpec((1, dv), lambda b, h, t: (0, 0))],
        out_specs=pl.BlockSpec((t_blk, dv), lambda b, h, t: (b * n_t + t, h)),
        out_shape=jax.ShapeDtypeStruct((m, heads * dv), BF16),
        scratch_shapes=[pltpu.VMEM((dv, dkp), F32)],
        compiler_params=_params("parallel", "parallel", "arbitrary"),
        name="gla_mixer",
    )(proj, proj, proj, proj, proj, w_gk_pad, b_gk_pad, onorm.reshape(1, dv))


def _gla_pack_weights(w_in, w_gk, b_gk, mix_width, xa_width):
    heads, rank, dkp = GLA_HEADS, GLA_GATE_RANK, GLA_DK_PAD
    d_model = w_in.shape[0]
    kw = w_gk.shape[1]
    dk = kw // heads
    q, k, v, g, gl, xq = jnp.split(
        w_in, [kw, 2 * kw, 2 * kw + mix_width, 2 * kw + 2 * mix_width, 2 * kw + 2 * mix_width + rank],
        axis=1)

    def pad_heads(w):
        w = w.reshape(w.shape[0], heads, dk)
        return jnp.pad(w, ((0, 0), (0, 0), (0, dkp - dk))).reshape(w.shape[0], heads * dkp)

    gl = jnp.pad(gl, ((0, 0), (0, LANE - rank)))
    w_packed = jnp.concatenate([v, g, pad_heads(q), pad_heads(k), xq, gl], axis=1).astype(BF16)
    w_gk_pad = jnp.pad(pad_heads(w_gk), ((0, LANE - rank), (0, 0))).astype(BF16)
    b_gk_pad = pad_heads(b_gk.reshape(1, kw)).astype(F32)
    col_splits = (0, mix_width, 2 * mix_width, 2 * mix_width + heads * dkp,
                  2 * mix_width + 2 * heads * dkp, w_packed.shape[1])
    assert w_packed.shape == (d_model, 2 * mix_width + 2 * heads * dkp + LANE + xa_width)
    return w_packed, w_gk_pad, b_gk_pad, dk, col_splits


def kernel(x, mem, norm_mix, norm_ffn, norm_mem, norm_final, hgrn_w_in, hgrn_lb_logits, hgrn_onorm,
           gla_w_in, gla_w_gk, gla_b_gk, gla_onorm, w_mem_kv, w_out, w_gate_up, w_down):
    batch, seq, d_model = x.shape
    n_mem = mem.shape[1]
    depth = norm_mix.shape[0]
    mix_width = hgrn_lb_logits.shape[1]
    xa_width = d_model - mix_width
    m = batch * seq

    h = x.reshape(m, d_model)
    kv = _mem_kv(mem.reshape(batch * n_mem, d_model), norm_mem, w_mem_kv.astype(BF16))
    kv = kv.reshape(depth * batch, n_mem, 2 * xa_width)

    hgrn_w = hgrn_w_in.astype(BF16)
    hgrn_splits = tuple(range(0, 4 * mix_width + 1, mix_width)) + (4 * mix_width + xa_width,)
    w_out_b = w_out.astype(BF16)
    w_gate_up_b = w_gate_up.astype(BF16)
    w_down_b = w_down.astype(BF16)

    for layer in range(depth):
        j = layer // 2
        if layer % 2 == 0:
            proj = _in_proj(h, norm_mix[layer], hgrn_w, j, hgrn_splits, tm=512)
            y = _hgrn_mixer(proj, hgrn_lb_logits, hgrn_onorm[j], j, batch, t_blk=512)
            xq_block = (4 * mix_width) // xa_width
        else:
            w_packed, w_gk_pad, b_gk_pad, dk, gla_splits = _gla_pack_weights(
                gla_w_in[j], gla_w_gk[j], gla_b_gk[j], mix_width, xa_width)
            proj = _in_proj(h, norm_mix[layer], w_packed[None], 0, gla_splits, tm=512)
            y = _gla_mixer(proj, w_gk_pad, b_gk_pad, gla_onorm[j], dk, batch, t_blk=512)
            xq_block = gla_splits[-2] // xa_width
        h, hn = _out_proj(y, proj, xq_block, kv, w_out_b, layer, h, norm_ffn[layer], batch, tm=512)
        act = _ffn_up(hn, w_gate_up_b, layer, tm=2048, tn=512)
        h = _matmul_residual(act, w_down_b, layer, h, tm=1024, tn=512)

    out = _rmsnorm(h, norm_final, tm=512)
    return out.reshape(batch, seq, d_model)
```

```python
import functools

import jax
import jax.numpy as jnp
from jax import lax
from jax.experimental import pallas as pl
from jax.experimental.pallas import tpu as pltpu

F32 = jnp.float32
BF16 = jnp.bfloat16

EPS = 1e-6
CHUNK = 64
STAGGER_GROUP = 2
ROW_SUBTILE = 512
HGRN_SUB = 32
GLA_SUB = 64
XA_HEADS = 4
HGRN_HEAD_DIM = 128
GLA_HEADS = 4
GLA_GATE_RANK = 16
GLA_GATE_NORMALIZER = 16.0
LANE = 128
GLA_DK_PAD = 256
V7X_VMEM_LIMIT = 56 * 1024 * 1024

NT_DIMS = (((1,), (1,)), ((), ()))
TN_DIMS = (((0,), (0,)), ((), ()))

RESIDENT = pl.Buffered(1)


def _params(*semantics):
    return pltpu.CompilerParams(dimension_semantics=semantics, vmem_limit_bytes=V7X_VMEM_LIMIT)


def _sigmoid(x):
    return 1.0 / (1.0 + jnp.exp(-x))


def _rmsnorm_rows(x, gain):
    ms = jnp.mean(x * x, axis=-1, keepdims=True)
    return x * lax.rsqrt(ms + EPS) * gain


def _in_proj_kernel(col_splits, x_ref, g_ref, w_ref, o_ref):
    xn = _rmsnorm_rows(x_ref[...], g_ref[...]).astype(BF16)
    for lo, hi in zip(col_splits[:-1], col_splits[1:]):
        o_ref[:, lo:hi] = jnp.dot(xn, w_ref[:, lo:hi], preferred_element_type=F32).astype(o_ref.dtype)


def _in_proj(x, gain, w_stack, layer, col_splits, tm):
    m, k = x.shape
    n = w_stack.shape[2]
    return pl.pallas_call(
        functools.partial(_in_proj_kernel, col_splits),
        grid=(m // tm,),
        in_specs=[
            pl.BlockSpec((tm, k), lambda i: (i, 0)),
            pl.BlockSpec((1, k), lambda i: (0, 0)),
            pl.BlockSpec((None, k, n), lambda i: (layer, 0, 0), pipeline_mode=RESIDENT),
        ],
        out_specs=pl.BlockSpec((tm, n), lambda i: (i, 0)),
        out_shape=jax.ShapeDtypeStruct((m, n), BF16),
        compiler_params=_params("parallel"),
        name="in_proj",
    )(x, gain.reshape(1, k), w_stack)


def _mem_kv_kernel(x_ref, g_ref, w_ref, o_ref):
    xn = _rmsnorm_rows(x_ref[...], g_ref[...]).astype(BF16)
    o_ref[...] = jnp.dot(xn, w_ref[...], preferred_element_type=F32).astype(o_ref.dtype)


def _mem_kv(mem2, gain, w_stack):
    m, k = mem2.shape
    depth, _, n = w_stack.shape
    return pl.pallas_call(
        _mem_kv_kernel,
        grid=(depth,),
        in_specs=[
            pl.BlockSpec((m, k), lambda l: (0, 0)),
            pl.BlockSpec((1, k), lambda l: (0, 0)),
            pl.BlockSpec((None, k, n), lambda l: (l, 0, 0)),
        ],
        out_specs=pl.BlockSpec((None, m, n), lambda l: (l, 0, 0)),
        out_shape=jax.ShapeDtypeStruct((depth, m, n), BF16),
        compiler_params=_params("parallel"),
        name="mem_kv_proj",
    )(mem2, gain.reshape(1, k), w_stack)


def _ffn_up_kernel(xn_ref, wg_ref, wu_ref, o_ref):
    for r in range(0, xn_ref.shape[0], ROW_SUBTILE):
        xn = xn_ref[r:r + ROW_SUBTILE, :]
        gate = jnp.dot(xn, wg_ref[...], preferred_element_type=F32)
        up = jnp.dot(xn, wu_ref[...], preferred_element_type=F32)
        o_ref[r:r + ROW_SUBTILE, :] = (gate * _sigmoid(gate) * up).astype(o_ref.dtype)


def _ffn_up(xn, w_stack, layer, tm, tn):
    m, k = xn.shape
    d_ff = w_stack.shape[2] // 2
    n_blocks = d_ff // tn
    return pl.pallas_call(
        _ffn_up_kernel,
        grid=(m // tm, n_blocks),
        in_specs=[
            pl.BlockSpec((tm, k), lambda i, j: (i, 0)),
            pl.BlockSpec((None, k, tn), lambda i, j: (layer, 0, j)),
            pl.BlockSpec((None, k, tn), lambda i, j: (layer, 0, j + n_blocks)),
        ],
        out_specs=pl.BlockSpec((tm, tn), lambda i, j: (i, j)),
        out_shape=jax.ShapeDtypeStruct((m, d_ff), BF16),
        compiler_params=_params("parallel", "arbitrary"),
        name="ffn_up",
    )(xn, w_stack, w_stack)


def _matmul_residual_kernel(a_ref, w_ref, r_ref, o_ref):
    for r in range(0, a_ref.shape[0], ROW_SUBTILE):
        rows = slice(r, r + ROW_SUBTILE)
        o_ref[rows, :] = r_ref[rows, :] + jnp.dot(a_ref[rows, :], w_ref[...], preferred_element_type=F32)


def _matmul_residual(a, w_stack, layer, res, tm, tn):
    m, k = a.shape
    n = w_stack.shape[2]
    return pl.pallas_call(
        _matmul_residual_kernel,
        grid=(m // tm, n // tn),
        in_specs=[
            pl.BlockSpec((tm, k), lambda i, j: (i, 0)),
            pl.BlockSpec((None, k, tn), lambda i, j: (layer, 0, j)),
            pl.BlockSpec((tm, tn), lambda i, j: (i, j)),
        ],
        out_specs=pl.BlockSpec((tm, tn), lambda i, j: (i, j)),
        out_shape=jax.ShapeDtypeStruct((m, n), F32),
        compiler_params=_params("parallel", "arbitrary"),
        name="matmul_residual",
    )(a, w_stack, res)


def _out_proj_kernel(y_ref, xq_ref, kv_ref, w_ref, r_ref, g_ref, h_ref, hn_ref):
    wy = y_ref.shape[1]
    d = xq_ref.shape[1] // XA_HEADS
    acc = jnp.dot(y_ref[...], w_ref[:wy, :], preferred_element_type=F32)
    heads = []
    for hd in range(XA_HEADS):
        q = xq_ref[:, hd * d:(hd + 1) * d]
        k = kv_ref[:, hd * d:(hd + 1) * d]
        v = kv_ref[:, (XA_HEADS + hd) * d:(XA_HEADS + hd + 1) * d]
        s = lax.dot_general(q, k, NT_DIMS, preferred_element_type=F32) * (d ** -0.5)
        p = jnp.exp(s - jnp.max(s, axis=-1, keepdims=True))
        denom = jnp.sum(p, axis=-1, keepdims=True)
        o = jnp.dot(p.astype(BF16), v, preferred_element_type=F32)
        heads.append((o / denom).astype(BF16))
    acc += jnp.dot(jnp.concatenate(heads, axis=1), w_ref[wy:, :], preferred_element_type=F32)
    h = r_ref[...] + acc
    h_ref[...] = h
    hn_ref[...] = _rmsnorm_rows(h, g_ref[...]).astype(hn_ref.dtype)


def _out_proj(y, proj, xq_block, kv_stack, w_stack, layer, res, gain, batch, tm):
    m, wy = y.shape
    _, k, n = w_stack.shape
    wx = k - wy
    n_mem, wkv = kv_stack.shape[1:]
    steps_per_batch = m // batch // tm
    return pl.pallas_call(
        _out_proj_kernel,
        grid=(m // tm,),
        in_specs=[
            pl.BlockSpec((tm, wy), lambda i: (i, 0)),
            pl.BlockSpec((tm, wx), lambda i: (i, xq_block)),
            pl.BlockSpec((None, n_mem, wkv), lambda i: (layer * batch + i // steps_per_batch, 0, 0)),
            pl.BlockSpec((None, k, n), lambda i: (layer, 0, 0), pipeline_mode=RESIDENT),
            pl.BlockSpec((tm, n), lambda i: (i, 0)),
            pl.BlockSpec((1, n), lambda i: (0, 0)),
        ],
        out_specs=[pl.BlockSpec((tm, n), lambda i: (i, 0)), pl.BlockSpec((tm, n), lambda i: (i, 0))],
        out_shape=[jax.ShapeDtypeStruct((m, n), F32), jax.ShapeDtypeStruct((m, n), BF16)],
        compiler_params=_params("parallel"),
        name="xattn_out_proj",
    )(y, proj, kv_stack, w_stack, res, gain.reshape(1, n))


def _rmsnorm_kernel(x_ref, g_ref, o_ref):
    o_ref[...] = _rmsnorm_rows(x_ref[...], g_ref[...])


def _rmsnorm(x, gain, tm):
    m, k = x.shape
    return pl.pallas_call(
        _rmsnorm_kernel,
        grid=(m // tm,),
        in_specs=[pl.BlockSpec((tm, k), lambda i: (i, 0)), pl.BlockSpec((1, k), lambda i: (0, 0))],
        out_specs=pl.BlockSpec((tm, k), lambda i: (i, 0)),
        out_shape=jax.ShapeDtypeStruct((m, k), F32),
        compiler_params=_params("parallel"),
        name="final_rmsnorm",
    )(x, gain.reshape(1, k))


def _chunk_masks():
    t = lax.broadcasted_iota(jnp.int32, (CHUNK, CHUNK), 0)
    s = lax.broadcasted_iota(jnp.int32, (CHUNK, CHUNK), 1)
    tril = (s <= t)
    return tril.astype(BF16), tril


def _chunk_cumsum(tril_b, g):
    dk = g.shape[1]
    hi = g.astype(BF16)
    lo = (g - hi.astype(F32)).astype(BF16)
    both = jnp.dot(tril_b, jnp.concatenate([hi, lo], axis=1), preferred_element_type=F32)
    return both[:, :dk] + both[:, dk:]


def _chunk_scores(q, k, vb, cum, sub):
    dk = q.shape[1]
    n_sub = CHUNK // sub
    blocks = [slice(sub * r, sub * (r + 1)) for r in range(n_sub)]
    mid = [cum[sub * j + sub // 2 - 1:sub * j + sub // 2, :] for j in range(n_sub)]
    last = cum[CHUNK - 1:CHUNK, :]
    zero = jnp.zeros((sub, dk), F32)

    q_slots, k_slots, k_last = [], [], []
    for j in range(n_sub):
        q_slots.append(jnp.concatenate(
            [zero if r < j else q[blocks[r], :] * jnp.exp(cum[blocks[r], :] - mid[j])
             for r in range(n_sub)], axis=0))
        k_mid = k[blocks[j], :] * jnp.exp(mid[j] - cum[blocks[j], :])
        k_slots.append(jnp.concatenate([k_mid if r == j else zero for r in range(n_sub)], axis=0))
        k_last.append(k_mid * jnp.exp(last - mid[j]))
    s_raw = lax.dot_general(jnp.concatenate(q_slots, axis=1).astype(BF16),
                            jnp.concatenate(k_slots, axis=1).astype(BF16),
                            NT_DIMS, preferred_element_type=F32)
    upd = lax.dot_general(vb, jnp.concatenate(k_last, axis=0).astype(BF16), TN_DIMS,
                          preferred_element_type=F32)
    q_dec = (q_slots[0] * jnp.exp(mid[0])).astype(BF16)
    return s_raw, upd, q_dec, jnp.exp(last), vb


def _chunk_readout(s_raw, tril, vb, q_dec, st):
    intra = jnp.dot(jnp.where(tril, s_raw, 0.0).astype(BF16), vb, preferred_element_type=F32)
    inter = lax.dot_general(q_dec, st.astype(BF16), NT_DIMS, preferred_element_type=F32)
    return intra + inter


def _gated_head_norm(o, gain, gate_raw):
    ms = jnp.mean(o * o, axis=-1, keepdims=True)
    return o * lax.rsqrt(ms + EPS) * gain * (gate_raw * _sigmoid(gate_raw))


def _run_staggered(n_chunks, stages):
    n_groups = n_chunks // STAGGER_GROUP
    for slot in range(n_groups + len(stages) - 1):
        for depth in reversed(range(len(stages))):
            group = slot - depth
            if 0 <= group < n_groups:
                for c in range(group * STAGGER_GROUP, (group + 1) * STAGGER_GROUP):
                    stages[depth](c)


def _hgrn_kernel(layer_j, q_ref, f_ref, i_ref, g_ref, lbl_ref, on_ref, y_ref, st_ref):
    @pl.when(pl.program_id(2) == 0)
    def _():
        st_ref[...] = jnp.zeros_like(st_ref)

    logits = lbl_ref[...]
    ex = jnp.exp(logits - jnp.max(logits, axis=0, keepdims=True))
    sm = ex / jnp.sum(ex, axis=0, keepdims=True)
    lb = jnp.zeros((1, sm.shape[1]), F32)
    for r in range(1, layer_j + 1):
        lb = lb + sm[r:r + 1, :]

    tril_b, tril = _chunk_masks()
    gain = on_ref[...]
    n_chunks = q_ref.shape[0] // CHUNK
    rows = [slice(c * CHUNK, (c + 1) * CHUNK) for c in range(n_chunks)]
    ctx = [{} for _ in range(n_chunks)]
    state = [st_ref[...]]

    def decay(c):
        q_raw = q_ref[rows[c], :].astype(F32)
        f = lb + (1.0 - lb) * _sigmoid(f_ref[rows[c], :].astype(F32))
        ctx[c].update(q=q_raw * _sigmoid(q_raw), k=1.0 - f, cum=_chunk_cumsum(tril_b, jnp.log(f)))

    def scores(c):
        x = ctx[c]
        x["parts"] = _chunk_scores(x.pop("q"), x.pop("k"), i_ref[rows[c], :],
                                   x.pop("cum"), HGRN_SUB)

    def readout(c):
        s_raw, upd, q_dec, e_last, vb = ctx[c].pop("parts")
        ctx[c]["o"] = _chunk_readout(s_raw, tril, vb, q_dec, state[0])
        state[0] = state[0] * e_last + upd

    def emit(c):
        o = ctx[c].pop("o")
        y_ref[rows[c], :] = _gated_head_norm(o, gain, g_ref[rows[c], :].astype(F32)).astype(y_ref.dtype)

    _run_staggered(n_chunks, [decay, scores, readout, emit])
    st_ref[...] = state[0]


def _hgrn_mixer(proj, lb_logits, onorm, layer_j, batch, t_blk):
    m = proj.shape[0]
    d = HGRN_HEAD_DIM
    mix_width = lb_logits.shape[1]
    heads = mix_width // d
    n_t = m // batch // t_blk
    n_a = lb_logits.shape[0]

    def col(group):
        return pl.BlockSpec((t_blk, d), lambda b, h, t: (b * n_t + t, group * heads + h))

    return pl.pallas_call(
        functools.partial(_hgrn_kernel, layer_j),
        grid=(batch, heads, n_t),
        in_specs=[col(0), col(1), col(2), col(3),
                  pl.BlockSpec((n_a, d), lambda b, h, t: (0, h)),
                  pl.BlockSpec((1, d), lambda b, h, t: (0, 0))],
        out_specs=pl.BlockSpec((t_blk, d), lambda b, h, t: (b * n_t + t, h)),
        out_shape=jax.ShapeDtypeStruct((m, mix_width), BF16),
        scratch_shapes=[pltpu.VMEM((d, d), F32)],
        compiler_params=_params("parallel", "parallel", "arbitrary"),
        name="hgrn2_mixer",
    )(proj, proj, proj, proj, lb_logits, onorm.reshape(1, d))


def _gla_kernel(dk_true, v_ref, g_ref, q_ref, k_ref, gl_ref, wgk_ref, bgk_ref, on_ref, y_ref, st_ref):
    @pl.when(pl.program_id(2) == 0)
    def _():
        st_ref[...] = jnp.zeros_like(st_ref)

    tril_b, tril = _chunk_masks()
    gain = on_ref[...]
    wgk = wgk_ref[...]
    bgk = bgk_ref[...]
    n_chunks = q_ref.shape[0] // CHUNK
    rows = [slice(c * CHUNK, (c + 1) * CHUNK) for c in range(n_chunks)]
    ctx = [{} for _ in range(n_chunks)]
    state = [st_ref[...]]

    def gate(c):
        ctx[c]["x"] = jnp.dot(gl_ref[rows[c], :], wgk, preferred_element_type=F32) + bgk

    def decay(c):
        x = ctx[c].pop("x")
        log_a = (jnp.minimum(x, 0.0) - jnp.log1p(jnp.exp(-jnp.abs(x)))) / GLA_GATE_NORMALIZER
        ctx[c]["cum"] = _chunk_cumsum(tril_b, log_a)

    def scores(c):
        q = q_ref[rows[c], :].astype(F32) * (dk_true ** -0.5)
        ctx[c]["parts"] = _chunk_scores(q, k_ref[rows[c], :].astype(F32), v_ref[rows[c], :],
                                        ctx[c].pop("cum"), GLA_SUB)

    def readout(c):
        s_raw, upd, q_dec, e_last, vb = ctx[c].pop("parts")
        ctx[c]["o"] = _chunk_readout(s_raw, tril, vb, q_dec, state[0])
        state[0] = state[0] * e_last + upd

    def emit(c):
        o = ctx[c].pop("o")
        y_ref[rows[c], :] = _gated_head_norm(o, gain, g_ref[rows[c], :].astype(F32)).astype(y_ref.dtype)

    _run_staggered(n_chunks, [gate, decay, scores, readout, emit])
    st_ref[...] = state[0]


def _gla_mixer(proj, w_gk_pad, b_gk_pad, onorm, dk_true, batch, t_blk):
    m = proj.shape[0]
    heads = GLA_HEADS
    dv = onorm.shape[0]
    dkp = GLA_DK_PAD
    n_t = m // batch // t_blk
    q0 = (2 * heads * dv) // dkp
    gl0 = proj.shape[1] // LANE - 1

    def rows(width, block0):
        return pl.BlockSpec((t_blk, width), lambda b, h, t: (b * n_t + t, block0 + h))

    return pl.pallas_call(
        functools.partial(_gla_kernel, dk_true),
        grid=(batch, heads, n_t),
        in_specs=[rows(dv, 0), rows(dv, heads), rows(dkp, q0), rows(dkp, q0 + heads),
                  pl.BlockSpec((t_blk, LANE), lambda b, h, t: (b * n_t + t, gl0)),
                  pl.BlockSpec((LANE, dkp), lambda b, h, t: (0, h)),
                  pl.BlockSpec((1, dkp), lambda b, h, t: (0, h)),
                  pl.BlockSpec((1, dv), lambda b, h, t: (0, 0))],
        out_specs=pl.BlockSpec((t_blk, dv), lambda b, h, t: (b * n_t + t, h)),
        out_shape=jax.ShapeDtypeStruct((m, heads * dv), BF16),
        scratch_shapes=[pltpu.VMEM((dv, dkp), F32)],
        compiler_params=_params("parallel", "parallel", "arbitrary"),
        name="gla_mixer",
    )(proj, proj, proj, proj, proj, w_gk_pad, b_gk_pad, onorm.reshape(1, dv))


def _gla_pack_weights(w_in, w_gk, b_gk, mix_width, xa_width):
    heads, rank, dkp = GLA_HEADS, GLA_GATE_RANK, GLA_DK_PAD
    d_model = w_in.shape[0]
    kw = w_gk.shape[1]
    dk = kw // heads
    q, k, v, g, gl, xq = jnp.split(
        w_in, [kw, 2 * kw, 2 * kw + mix_width, 2 * kw + 2 * mix_width, 2 * kw + 2 * mix_width + rank],
        axis=1)

    def pad_heads(w):
        w = w.reshape(w.shape[0], heads, dk)
        return jnp.pad(w, ((0, 0), (0, 0), (0, dkp - dk))).reshape(w.shape[0], heads * dkp)

    gl = jnp.pad(gl, ((0, 0), (0, LANE - rank)))
    w_packed = jnp.concatenate([v, g, pad_heads(q), pad_heads(k), xq, gl], axis=1).astype(BF16)
    w_gk_pad = jnp.pad(pad_heads(w_gk), ((0, LANE - rank), (0, 0))).astype(BF16)
    b_gk_pad = pad_heads(b_gk.reshape(1, kw)).astype(F32)
    col_splits = (0, mix_width, 2 * mix_width, 2 * mix_width + heads * dkp,
                  2 * mix_width + 2 * heads * dkp, w_packed.shape[1])
    assert w_packed.shape == (d_model, 2 * mix_width + 2 * heads * dkp + LANE + xa_width)
    return w_packed, w_gk_pad, b_gk_pad, dk, col_splits


def kernel(x, mem, norm_mix, norm_ffn, norm_mem, norm_final, hgrn_w_in, hgrn_lb_logits, hgrn_onorm,
           gla_w_in, gla_w_gk, gla_b_gk, gla_onorm, w_mem_kv, w_out, w_gate_up, w_down):
    batch, seq, d_model = x.shape
    n_mem = mem.shape[1]
    depth = norm_mix.shape[0]
    mix_width = hgrn_lb_logits.shape[1]
    xa_width = d_model - mix_width
    m = batch * seq

    h = x.reshape(m, d_model)
    kv = _mem_kv(mem.reshape(batch * n_mem, d_model), norm_mem, w_mem_kv.astype(BF16))
    kv = kv.reshape(depth * batch, n_mem, 2 * xa_width)

    hgrn_w = hgrn_w_in.astype(BF16)
    hgrn_splits = tuple(range(0, 4 * mix_width + 1, mix_width)) + (4 * mix_width + xa_width,)
    w_out_b = w_out.astype(BF16)
    w_gate_up_b = w_gate_up.astype(BF16)
    w_down_b = w_down.astype(BF16)

    for layer in range(depth):
        j = layer // 2
        if layer % 2 == 0:
            proj = _in_proj(h, norm_mix[layer], hgrn_w, j, hgrn_splits, tm=512)
            y = _hgrn_mixer(proj, hgrn_lb_logits, hgrn_onorm[j], j, batch, t_blk=1024)
            xq_block = (4 * mix_width) // xa_width
        else:
            w_packed, w_gk_pad, b_gk_pad, dk, gla_splits = _gla_pack_weights(
                gla_w_in[j], gla_w_gk[j], gla_b_gk[j], mix_width, xa_width)
            proj = _in_proj(h, norm_mix[layer], w_packed[None], 0, gla_splits, tm=512)
            y = _gla_mixer(proj, w_gk_pad, b_gk_pad, gla_onorm[j], dk, batch, t_blk=1024)
            xq_block = gla_splits[-2] // xa_width
        h, hn = _out_proj(y, proj, xq_block, kv, w_out_b, layer, h, norm_ffn[layer], batch, tm=512)
        act = _ffn_up(hn, w_gate_up_b, layer, tm=2048, tn=512)
        h = _matmul_residual(act, w_down_b, layer, h, tm=1024, tn=512)

    out = _rmsnorm(h, norm_final, tm=512)
    return out.reshape(batch, seq, d_model)
```

```python
import functools

import jax
import jax.numpy as jnp
from jax import lax
from jax.experimental import pallas as pl
from jax.experimental.pallas import tpu as pltpu

F32 = jnp.float32
BF16 = jnp.bfloat16

EPS = 1e-6
CHUNK = 64
STAGGER_GROUP = 8
ROW_SUBTILE = 512
HGRN_SUB = 32
GLA_SUB = 64
XA_HEADS = 4
HGRN_HEAD_DIM = 128
GLA_HEADS = 4
GLA_GATE_RANK = 16
GLA_GATE_NORMALIZER = 16.0
LANE = 128
GLA_DK_PAD = 256
V7X_VMEM_LIMIT = 56 * 1024 * 1024

NT_DIMS = (((1,), (1,)), ((), ()))
TN_DIMS = (((0,), (0,)), ((), ()))

RESIDENT = pl.Buffered(1)


def _params(*semantics):
    return pltpu.CompilerParams(dimension_semantics=semantics, vmem_limit_bytes=V7X_VMEM_LIMIT)


def _sigmoid(x):
    return 1.0 / (1.0 + jnp.exp(-x))


def _rmsnorm_rows(x, gain):
    ms = jnp.mean(x * x, axis=-1, keepdims=True)
    return x * lax.rsqrt(ms + EPS) * gain


def _split_hi_lo(g):
    hi = g.astype(BF16)
    return hi, (g - hi.astype(F32)).astype(BF16)


def _hgrn_in_proj_kernel(layer_j, xn_ref, w_ref, lbl_ref, o_ref):
    xn = xn_ref[...]
    w = lbl_ref.shape[1]

    logits = lbl_ref[...]
    ex = jnp.exp(logits - jnp.max(logits, axis=0, keepdims=True))
    sm = ex / jnp.sum(ex, axis=0, keepdims=True)
    lb = jnp.zeros((1, w), F32)
    for r in range(1, layer_j + 1):
        lb = lb + sm[r:r + 1, :]

    def proj(group):
        return jnp.dot(xn, w_ref[:, group * w:(group + 1) * w], preferred_element_type=F32)

    f = lb + (1.0 - lb) * _sigmoid(proj(1))
    hi, lo = _split_hi_lo(jnp.log(f))
    o_ref[:, w:2 * w] = hi
    o_ref[:, 2 * w:3 * w] = lo
    o_ref[:, 3 * w:4 * w] = (1.0 - f).astype(BF16)
    q = proj(0)
    o_ref[:, 0:w] = (q * _sigmoid(q)).astype(BF16)
    g = proj(3)
    o_ref[:, 5 * w:6 * w] = (g * _sigmoid(g)).astype(BF16)
    o_ref[:, 4 * w:5 * w] = proj(2).astype(BF16)
    o_ref[:, 6 * w:] = jnp.dot(xn, w_ref[:, 4 * w:], preferred_element_type=F32).astype(BF16)


def _hgrn_in_proj(xn, w_stack, layer_j, lb_logits, tm):
    m, k = xn.shape
    n_in = w_stack.shape[2]
    n_a, w = lb_logits.shape
    n_out = n_in + 2 * w
    return pl.pallas_call(
        functools.partial(_hgrn_in_proj_kernel, layer_j),
        grid=(m // tm,),
        in_specs=[
            pl.BlockSpec((tm, k), lambda i: (i, 0)),
            pl.BlockSpec((None, k, n_in), lambda i: (layer_j, 0, 0), pipeline_mode=RESIDENT),
            pl.BlockSpec((n_a, w), lambda i: (0, 0)),
        ],
        out_specs=pl.BlockSpec((tm, n_out), lambda i: (i, 0)),
        out_shape=jax.ShapeDtypeStruct((m, n_out), BF16),
        compiler_params=_params("parallel"),
        name="hgrn_in_proj",
    )(xn, w_stack, lb_logits)


def _gla_in_proj_kernel(dk_true, wv, wx, xn_ref, w_ref, wgk_ref, bgk_ref, o_ref):
    xn = xn_ref[...]
    wk = wgk_ref.shape[1]

    def proj(lo, hi):
        return jnp.dot(xn, w_ref[:, lo:hi], preferred_element_type=F32)

    c = 2 * wv + 2 * wk
    low = proj(c + wx, c + wx + LANE).astype(BF16)
    x = jnp.dot(low, wgk_ref[...], preferred_element_type=F32) + bgk_ref[...]
    log_a = (jnp.minimum(x, 0.0) - jnp.log1p(jnp.exp(-jnp.abs(x)))) / GLA_GATE_NORMALIZER
    hi, lo = _split_hi_lo(log_a)
    o_ref[:, c + wx:c + wx + wk] = hi
    o_ref[:, c + wx + wk:] = lo
    g = proj(wv, 2 * wv)
    o_ref[:, wv:2 * wv] = (g * _sigmoid(g)).astype(BF16)
    c = 2 * wv
    o_ref[:, c:c + wk] = (proj(c, c + wk) * (dk_true ** -0.5)).astype(BF16)
    o_ref[:, c + wk:c + 2 * wk] = proj(c + wk, c + 2 * wk).astype(BF16)
    o_ref[:, 0:wv] = proj(0, wv).astype(BF16)
    c += 2 * wk
    o_ref[:, c:c + wx] = proj(c, c + wx).astype(BF16)


def _gla_in_proj(xn, w_packed, w_gk_pad, b_gk_pad, mix_width, xa_width, dk_true, tm):
    m, k = xn.shape
    n_in = w_packed.shape[1]
    wk = w_gk_pad.shape[1]
    assert n_in == 2 * mix_width + 2 * wk + xa_width + LANE
    n_out = n_in - LANE + 2 * wk
    return pl.pallas_call(
        functools.partial(_gla_in_proj_kernel, dk_true, mix_width, xa_width),
        grid=(m // tm,),
        in_specs=[
            pl.BlockSpec((tm, k), lambda i: (i, 0)),
            pl.BlockSpec((k, n_in), lambda i: (0, 0), pipeline_mode=RESIDENT),
            pl.BlockSpec((LANE, wk), lambda i: (0, 0)),
            pl.BlockSpec((1, wk), lambda i: (0, 0)),
        ],
        out_specs=pl.BlockSpec((tm, n_out), lambda i: (i, 0)),
        out_shape=jax.ShapeDtypeStruct((m, n_out), BF16),
        compiler_params=_params("parallel"),
        name="gla_in_proj",
    )(xn, w_packed, w_gk_pad, b_gk_pad)


def _mem_kv_kernel(x_ref, g_ref, w_ref, o_ref):
    xn = _rmsnorm_rows(x_ref[...], g_ref[...]).astype(BF16)
    o_ref[...] = jnp.dot(xn, w_ref[...], preferred_element_type=F32).astype(o_ref.dtype)


def _mem_kv(mem2, gain, w_stack):
    m, k = mem2.shape
    depth, _, n = w_stack.shape
    return pl.pallas_call(
        _mem_kv_kernel,
        grid=(depth,),
        in_specs=[
            pl.BlockSpec((m, k), lambda l: (0, 0)),
            pl.BlockSpec((1, k), lambda l: (0, 0)),
            pl.BlockSpec((None, k, n), lambda l: (l, 0, 0)),
        ],
        out_specs=pl.BlockSpec((None, m, n), lambda l: (l, 0, 0)),
        out_shape=jax.ShapeDtypeStruct((depth, m, n), BF16),
        compiler_params=_params("parallel"),
        name="mem_kv_proj",
    )(mem2, gain.reshape(1, k), w_stack)


def _ffn_up_kernel(xn_ref, wg_ref, wu_ref, o_ref):
    for r in range(0, xn_ref.shape[0], ROW_SUBTILE):
        xn = xn_ref[r:r + ROW_SUBTILE, :]
        gate = jnp.dot(xn, wg_ref[...], preferred_element_type=F32)
        up = jnp.dot(xn, wu_ref[...], preferred_element_type=F32)
        o_ref[r:r + ROW_SUBTILE, :] = (gate * _sigmoid(gate) * up).astype(o_ref.dtype)


def _ffn_up(xn, w_stack, layer, tm, tn):
    m, k = xn.shape
    d_ff = w_stack.shape[2] // 2
    n_blocks = d_ff // tn
    return pl.pallas_call(
        _ffn_up_kernel,
        grid=(m // tm, n_blocks),
        in_specs=[
            pl.BlockSpec((tm, k), lambda i, j: (i, 0)),
            pl.BlockSpec((None, k, tn), lambda i, j: (layer, 0, j)),
            pl.BlockSpec((None, k, tn), lambda i, j: (layer, 0, j + n_blocks)),
        ],
        out_specs=pl.BlockSpec((tm, tn), lambda i, j: (i, j)),
        out_shape=jax.ShapeDtypeStruct((m, d_ff), BF16),
        compiler_params=_params("parallel", "arbitrary"),
        name="ffn_up",
    )(xn, w_stack, w_stack)


def _down_proj_kernel(a_ref, w_ref, r_ref, g_ref, *out_refs):
    h = r_ref[...] + jnp.dot(a_ref[...], w_ref[...], preferred_element_type=F32)
    hn_ref = out_refs[-1]
    hn_ref[...] = _rmsnorm_rows(h, g_ref[...]).astype(hn_ref.dtype)
    if len(out_refs) == 2:
        out_refs[0][...] = h


def _down_proj(a, w_stack, layer, res, gain, final, tm):
    m, k = a.shape
    n = w_stack.shape[2]
    rows = pl.BlockSpec((tm, n), lambda i: (i, 0))
    if final:
        out_specs, out_shape = rows, jax.ShapeDtypeStruct((m, n), F32)
    else:
        out_specs = [rows, rows]
        out_shape = [jax.ShapeDtypeStruct((m, n), F32), jax.ShapeDtypeStruct((m, n), BF16)]
    return pl.pallas_call(
        _down_proj_kernel,
        grid=(m // tm,),
        in_specs=[
            pl.BlockSpec((tm, k), lambda i: (i, 0)),
            pl.BlockSpec((None, k, n), lambda i: (layer, 0, 0), pipeline_mode=RESIDENT),
            rows,
            pl.BlockSpec((1, n), lambda i: (0, 0)),
        ],
        out_specs=out_specs,
        out_shape=out_shape,
        compiler_params=_params("parallel"),
        name="down_proj",
    )(a, w_stack, res, gain.reshape(1, n))


def _out_proj_kernel(y_ref, xq_ref, kv_ref, w_ref, r_ref, g_ref, h_ref, hn_ref):
    wy = y_ref.shape[1]
    d = xq_ref.shape[1] // XA_HEADS
    acc = jnp.dot(y_ref[...], w_ref[:wy, :], preferred_element_type=F32)
    heads = []
    for hd in range(XA_HEADS):
        q = xq_ref[:, hd * d:(hd + 1) * d]
        k = kv_ref[:, hd * d:(hd + 1) * d]
        v = kv_ref[:, (XA_HEADS + hd) * d:(XA_HEADS + hd + 1) * d]
        s = lax.dot_general(q, k, NT_DIMS, preferred_element_type=F32) * (d ** -0.5)
        p = jnp.exp(s - jnp.max(s, axis=-1, keepdims=True))
        denom = jnp.sum(p, axis=-1, keepdims=True)
        o = jnp.dot(p.astype(BF16), v, preferred_element_type=F32)
        heads.append((o / denom).astype(BF16))
    acc += jnp.dot(jnp.concatenate(heads, axis=1), w_ref[wy:, :], preferred_element_type=F32)
    h = r_ref[...] + acc
    h_ref[...] = h
    hn_ref[...] = _rmsnorm_rows(h, g_ref[...]).astype(hn_ref.dtype)


def _out_proj(y, proj, xq_block, kv_stack, w_stack, layer, res, gain, batch, tm):
    m, wy = y.shape
    _, k, n = w_stack.shape
    wx = k - wy
    n_mem, wkv = kv_stack.shape[1:]
    steps_per_batch = m // batch // tm
    return pl.pallas_call(
        _out_proj_kernel,
        grid=(m // tm,),
        in_specs=[
            pl.BlockSpec((tm, wy), lambda i: (i, 0)),
            pl.BlockSpec((tm, wx), lambda i: (i, xq_block)),
            pl.BlockSpec((None, n_mem, wkv), lambda i: (layer * batch + i // steps_per_batch, 0, 0)),
            pl.BlockSpec((None, k, n), lambda i: (layer, 0, 0), pipeline_mode=RESIDENT),
            pl.BlockSpec((tm, n), lambda i: (i, 0)),
            pl.BlockSpec((1, n), lambda i: (0, 0)),
        ],
        out_specs=[pl.BlockSpec((tm, n), lambda i: (i, 0)), pl.BlockSpec((tm, n), lambda i: (i, 0))],
        out_shape=[jax.ShapeDtypeStruct((m, n), F32), jax.ShapeDtypeStruct((m, n), BF16)],
        compiler_params=_params("parallel"),
        name="xattn_out_proj",
    )(y, proj, kv_stack, w_stack, res, gain.reshape(1, n))


def _rmsnorm_kernel(x_ref, g_ref, o_ref):
    o_ref[...] = _rmsnorm_rows(x_ref[...], g_ref[...]).astype(o_ref.dtype)


def _rmsnorm(x, gain, tm):
    m, k = x.shape
    return pl.pallas_call(
        _rmsnorm_kernel,
        grid=(m // tm,),
        in_specs=[pl.BlockSpec((tm, k), lambda i: (i, 0)), pl.BlockSpec((1, k), lambda i: (0, 0))],
        out_specs=pl.BlockSpec((tm, k), lambda i: (i, 0)),
        out_shape=jax.ShapeDtypeStruct((m, k), BF16),
        compiler_params=_params("parallel"),
        name="input_rmsnorm",
    )(x, gain.reshape(1, k))


def _chunk_masks():
    t = lax.broadcasted_iota(jnp.int32, (CHUNK, CHUNK), 0)
    s = lax.broadcasted_iota(jnp.int32, (CHUNK, CHUNK), 1)
    tril = (s <= t)
    return tril.astype(BF16), tril


def _chunk_cumsum(tril_b, hi, lo):
    dk = hi.shape[1]
    both = jnp.dot(tril_b, jnp.concatenate([hi, lo], axis=1), preferred_element_type=F32)
    return both[:, :dk] + both[:, dk:]


def _chunk_scores(q, k, vb, cum, sub):
    dk = q.shape[1]
    n_sub = CHUNK // sub
    blocks = [slice(sub * r, sub * (r + 1)) for r in range(n_sub)]
    mid = [cum[sub * j + sub // 2 - 1:sub * j + sub // 2, :] for j in range(n_sub)]
    last = cum[CHUNK - 1:CHUNK, :]
    zero = jnp.zeros((sub, dk), F32)

    q_slots, k_slots, k_last = [], [], []
    for j in range(n_sub):
        q_slots.append(jnp.concatenate(
            [zero if r < j else q[blocks[r], :] * jnp.exp(cum[blocks[r], :] - mid[j])
             for r in range(n_sub)], axis=0))
        k_mid = k[blocks[j], :] * jnp.exp(mid[j] - cum[blocks[j], :])
        k_slots.append(jnp.concatenate([k_mid if r == j else zero for r in range(n_sub)], axis=0))
        k_last.append(k_mid * jnp.exp(last - mid[j]))
    s_raw = lax.dot_general(jnp.concatenate(q_slots, axis=1).astype(BF16),
                            jnp.concatenate(k_slots, axis=1).astype(BF16),
                            NT_DIMS, preferred_element_type=F32)
    upd = lax.dot_general(vb, jnp.concatenate(k_last, axis=0).astype(BF16), TN_DIMS,
                          preferred_element_type=F32)
    q_dec = (q_slots[0] * jnp.exp(mid[0])).astype(BF16)
    return s_raw, upd, q_dec, jnp.exp(last), vb


def _chunk_readout(s_raw, tril, vb, q_dec, st):
    intra = jnp.dot(jnp.where(tril, s_raw, 0.0).astype(BF16), vb, preferred_element_type=F32)
    inter = lax.dot_general(q_dec, st.astype(BF16), NT_DIMS, preferred_element_type=F32)
    return intra + inter


def _gated_head_norm(o, gain, gate):
    ms = jnp.mean(o * o, axis=-1, keepdims=True)
    return o * lax.rsqrt(ms + EPS) * gain * gate


def _run_staggered(n_chunks, stages):
    n_groups = n_chunks // STAGGER_GROUP
    for slot in range(n_groups + len(stages) - 1):
        for depth in reversed(range(len(stages))):
            group = slot - depth
            if 0 <= group < n_groups:
                for c in range(group * STAGGER_GROUP, (group + 1) * STAGGER_GROUP):
                    stages[depth](c)


def _mixer_kernel(sub, q_ref, k_ref, v_ref, gate_ref, hi_ref, lo_ref, on_ref, y_ref, st_ref):
    @pl.when(pl.program_id(2) == 0)
    def _():
        st_ref[...] = jnp.zeros_like(st_ref)

    tril_b, tril = _chunk_masks()
    gain = on_ref[...]
    n_chunks = q_ref.shape[0] // CHUNK
    rows = [slice(c * CHUNK, (c + 1) * CHUNK) for c in range(n_chunks)]
    ctx = [{} for _ in range(n_chunks)]
    state = [st_ref[...]]

    def decay(c):
        ctx[c]["cum"] = _chunk_cumsum(tril_b, hi_ref[rows[c], :], lo_ref[rows[c], :])

    def scores(c):
        ctx[c]["parts"] = _chunk_scores(q_ref[rows[c], :].astype(F32), k_ref[rows[c], :].astype(F32),
                                        v_ref[rows[c], :], ctx[c].pop("cum"), sub)

    def readout(c):
        s_raw, upd, q_dec, e_last, vb = ctx[c].pop("parts")
        ctx[c]["o"] = _chunk_readout(s_raw, tril, vb, q_dec, state[0])
        state[0] = state[0] * e_last + upd

    def emit(c):
        gate = gate_ref[rows[c], :].astype(F32)
        y_ref[rows[c], :] = _gated_head_norm(ctx[c].pop("o"), gain, gate).astype(y_ref.dtype)

    _run_staggered(n_chunks, [decay, scores, readout, emit])
    st_ref[...] = state[0]


def _mixer(proj, onorm, cols, heads, dk, dv, sub, batch, t_blk, name):
    m = proj.shape[0]
    n_t = m // batch // t_blk

    def group(key, width):
        block0, rem = divmod(cols[key], width)
        assert rem == 0
        return pl.BlockSpec((t_blk, width), lambda b, h, t: (b * n_t + t, block0 + h))

    return pl.pallas_call(
        functools.partial(_mixer_kernel, sub),
        grid=(batch, heads, n_t),
        in_specs=[group("q", dk), group("k", dk), group("v", dv), group("gate", dv),
                  group("hi", dk), group("lo", dk),
                  pl.BlockSpec((1, dv), lambda b, h, t: (0, 0))],
        out_specs=pl.BlockSpec((t_blk, dv), lambda b, h, t: (b * n_t + t, h)),
        out_shape=jax.ShapeDtypeStruct((m, heads * dv), BF16),
        scratch_shapes=[pltpu.VMEM((dv, dk), F32)],
        compiler_params=_params("parallel", "parallel", "arbitrary"),
        name=name,
    )(proj, proj, proj, proj, proj, proj, onorm.reshape(1, dv))


def _gla_pack_weights(w_in, w_gk, b_gk, mix_width):
    heads, rank, dkp = GLA_HEADS, GLA_GATE_RANK, GLA_DK_PAD
    kw = w_gk.shape[1]
    dk = kw // heads
    q, k, v, g, gl, xq = jnp.split(
        w_in, [kw, 2 * kw, 2 * kw + mix_width, 2 * kw + 2 * mix_width, 2 * kw + 2 * mix_width + rank],
        axis=1)

    def pad_heads(w):
        w = w.reshape(w.shape[0], heads, dk)
        return jnp.pad(w, ((0, 0), (0, 0), (0, dkp - dk))).reshape(w.shape[0], heads * dkp)

    gl = jnp.pad(gl, ((0, 0), (0, LANE - rank)))
    w_packed = jnp.concatenate([v, g, pad_heads(q), pad_heads(k), xq, gl], axis=1).astype(BF16)
    w_gk_pad = jnp.pad(pad_heads(w_gk), ((0, LANE - rank), (0, 0))).astype(BF16)
    b_gk_pad = pad_heads(b_gk.reshape(1, kw)).astype(F32)
    return w_packed, w_gk_pad, b_gk_pad, dk


def kernel(x, mem, norm_mix, norm_ffn, norm_mem, norm_final, hgrn_w_in, hgrn_lb_logits, hgrn_onorm,
           gla_w_in, gla_w_gk, gla_b_gk, gla_onorm, w_mem_kv, w_out, w_gate_up, w_down):
    batch, seq, d_model = x.shape
    n_mem = mem.shape[1]
    depth = norm_mix.shape[0]
    mix_width = hgrn_lb_logits.shape[1]
    xa_width = d_model - mix_width
    m = batch * seq

    h = x.reshape(m, d_model)
    kv = _mem_kv(mem.reshape(batch * n_mem, d_model), norm_mem, w_mem_kv.astype(BF16))
    kv = kv.reshape(depth * batch, n_mem, 2 * xa_width)

    hgrn_w = hgrn_w_in.astype(BF16)
    w_out_b = w_out.astype(BF16)
    w_gate_up_b = w_gate_up.astype(BF16)
    w_down_b = w_down.astype(BF16)

    hn = _rmsnorm(h, norm_mix[0], tm=512)
    for layer in range(depth):
        j = layer // 2
        if layer % 2 == 0:
            proj = _hgrn_in_proj(hn, hgrn_w, j, hgrn_lb_logits, tm=256)
            w = mix_width
            cols = dict(q=0, hi=w, lo=2 * w, k=3 * w, v=4 * w, gate=5 * w)
            y = _mixer(proj, hgrn_onorm[j], cols, mix_width // HGRN_HEAD_DIM, HGRN_HEAD_DIM,
                       HGRN_HEAD_DIM, HGRN_SUB, batch, t_blk=1024, name="hgrn2_mixer")
            xq_col = 6 * w
        else:
            w_packed, w_gk_pad, b_gk_pad, dk = _gla_pack_weights(
                gla_w_in[j], gla_w_gk[j], gla_b_gk[j], mix_width)
            proj = _gla_in_proj(hn, w_packed, w_gk_pad, b_gk_pad, mix_width, xa_width, dk, tm=256)
            wk = GLA_HEADS * GLA_DK_PAD
            xq_col = 2 * mix_width + 2 * wk
            cols = dict(v=0, gate=mix_width, q=2 * mix_width, k=2 * mix_width + wk,
                        hi=xq_col + xa_width, lo=xq_col + xa_width + wk)
            y = _mixer(proj, gla_onorm[j], cols, GLA_HEADS, GLA_DK_PAD, mix_width // GLA_HEADS,
                       GLA_SUB, batch, t_blk=1024, name="gla_mixer")
        assert xq_col % xa_width == 0
        h, hn = _out_proj(y, proj, xq_col // xa_width, kv, w_out_b, layer, h, norm_ffn[layer], batch,
                          tm=512)
        act = _ffn_up(hn, w_gate_up_b, layer, tm=2048, tn=512)
        if layer + 1 < depth:
            h, hn = _down_proj(act, w_down_b, layer, h, norm_mix[layer + 1], final=False, tm=256)
        else:
            out = _down_proj(act, w_down_b, layer, h, norm_final, final=True, tm=256)
    return out.reshape(batch, seq, d_model)
```

```python
import functools

import jax
import jax.numpy as jnp
from jax import lax
from jax.experimental import pallas as pl
from jax.experimental.pallas import tpu as pltpu

F32 = jnp.float32
BF16 = jnp.bfloat16

EPS = 1e-6
CHUNK = 64
STAGGER_GROUP = 8
ROW_SUBTILE = 256
OUT_PROJ_SUBTILE = 512
HGRN_SUB = 32
GLA_SUB = 64
XA_HEADS = 4
HGRN_HEAD_DIM = 128
GLA_HEADS = 4
GLA_GATE_RANK = 16
GLA_GATE_NORMALIZER = 16.0
LANE = 128
GLA_DK_PAD = 256
V7X_VMEM_LIMIT = 56 * 1024 * 1024

NT_DIMS = (((1,), (1,)), ((), ()))
TN_DIMS = (((0,), (0,)), ((), ()))

RESIDENT = pl.Buffered(1)


def _params(*semantics):
    return pltpu.CompilerParams(dimension_semantics=semantics, vmem_limit_bytes=V7X_VMEM_LIMIT)


def _sigmoid(x):
    return 1.0 / (1.0 + jnp.exp(-x))


def _rmsnorm_rows(x, gain):
    ms = jnp.mean(x * x, axis=-1, keepdims=True)
    return x * lax.rsqrt(ms + EPS) * gain


def _split_hi_lo(g):
    hi = g.astype(BF16)
    return hi, (g - hi.astype(F32)).astype(BF16)


def _hgrn_in_proj_kernel(layer_j, xn_ref, w_ref, lbl_ref, o_ref):
    xn = xn_ref[...]
    w = lbl_ref.shape[1]

    logits = lbl_ref[...]
    ex = jnp.exp(logits - jnp.max(logits, axis=0, keepdims=True))
    sm = ex / jnp.sum(ex, axis=0, keepdims=True)
    lb = jnp.zeros((1, w), F32)
    for r in range(1, layer_j + 1):
        lb = lb + sm[r:r + 1, :]

    def proj(group):
        return jnp.dot(xn, w_ref[:, group * w:(group + 1) * w], preferred_element_type=F32)

    f = lb + (1.0 - lb) * _sigmoid(proj(1))
    hi, lo = _split_hi_lo(jnp.log(f))
    o_ref[:, w:2 * w] = hi
    o_ref[:, 2 * w:3 * w] = lo
    o_ref[:, 3 * w:4 * w] = (1.0 - f).astype(BF16)
    q = proj(0)
    o_ref[:, 0:w] = (q * _sigmoid(q)).astype(BF16)
    g = proj(3)
    o_ref[:, 5 * w:6 * w] = (g * _sigmoid(g)).astype(BF16)
    o_ref[:, 4 * w:5 * w] = proj(2).astype(BF16)
    o_ref[:, 6 * w:] = jnp.dot(xn, w_ref[:, 4 * w:], preferred_element_type=F32).astype(BF16)


def _hgrn_in_proj(xn, w_stack, layer_j, lb_logits, tm):
    m, k = xn.shape
    n_in = w_stack.shape[2]
    n_a, w = lb_logits.shape
    n_out = n_in + 2 * w
    return pl.pallas_call(
        functools.partial(_hgrn_in_proj_kernel, layer_j),
        grid=(m // tm,),
        in_specs=[
            pl.BlockSpec((tm, k), lambda i: (i, 0)),
            pl.BlockSpec((None, k, n_in), lambda i: (layer_j, 0, 0), pipeline_mode=RESIDENT),
            pl.BlockSpec((n_a, w), lambda i: (0, 0)),
        ],
        out_specs=pl.BlockSpec((tm, n_out), lambda i: (i, 0)),
        out_shape=jax.ShapeDtypeStruct((m, n_out), BF16),
        compiler_params=_params("parallel"),
        name="hgrn_in_proj",
    )(xn, w_stack, lb_logits)


def _gla_in_proj_kernel(dk_true, wv, wx, xn_ref, w_ref, wgk_ref, bgk_ref, o_ref):
    xn = xn_ref[...]
    wk = wgk_ref.shape[1]

    def proj(lo, hi):
        return jnp.dot(xn, w_ref[:, lo:hi], preferred_element_type=F32)

    c = 2 * wv + 2 * wk
    low = proj(c + wx, c + wx + LANE).astype(BF16)
    x = jnp.dot(low, wgk_ref[...], preferred_element_type=F32) + bgk_ref[...]
    log_a = (jnp.minimum(x, 0.0) - jnp.log1p(jnp.exp(-jnp.abs(x)))) / GLA_GATE_NORMALIZER
    hi, lo = _split_hi_lo(log_a)
    o_ref[:, c + wx:c + wx + wk] = hi
    o_ref[:, c + wx + wk:] = lo
    g = proj(wv, 2 * wv)
    o_ref[:, wv:2 * wv] = (g * _sigmoid(g)).astype(BF16)
    c = 2 * wv
    o_ref[:, c:c + wk] = (proj(c, c + wk) * (dk_true ** -0.5)).astype(BF16)
    o_ref[:, c + wk:c + 2 * wk] = proj(c + wk, c + 2 * wk).astype(BF16)
    o_ref[:, 0:wv] = proj(0, wv).astype(BF16)
    c += 2 * wk
    o_ref[:, c:c + wx] = proj(c, c + wx).astype(BF16)


def _gla_in_proj(xn, w_packed, w_gk_pad, b_gk_pad, mix_width, xa_width, dk_true, tm):
    m, k = xn.shape
    n_in = w_packed.shape[1]
    wk = w_gk_pad.shape[1]
    assert n_in == 2 * mix_width + 2 * wk + xa_width + LANE
    n_out = n_in - LANE + 2 * wk
    return pl.pallas_call(
        functools.partial(_gla_in_proj_kernel, dk_true, mix_width, xa_width),
        grid=(m // tm,),
        in_specs=[
            pl.BlockSpec((tm, k), lambda i: (i, 0)),
            pl.BlockSpec((k, n_in), lambda i: (0, 0), pipeline_mode=RESIDENT),
            pl.BlockSpec((LANE, wk), lambda i: (0, 0)),
            pl.BlockSpec((1, wk), lambda i: (0, 0)),
        ],
        out_specs=pl.BlockSpec((tm, n_out), lambda i: (i, 0)),
        out_shape=jax.ShapeDtypeStruct((m, n_out), BF16),
        compiler_params=_params("parallel"),
        name="gla_in_proj",
    )(xn, w_packed, w_gk_pad, b_gk_pad)


def _mem_kv_kernel(x_ref, g_ref, w_ref, o_ref):
    xn = _rmsnorm_rows(x_ref[...], g_ref[...]).astype(BF16)
    o_ref[...] = jnp.dot(xn, w_ref[...], preferred_element_type=F32).astype(o_ref.dtype)


def _mem_kv(mem2, gain, w_stack):
    m, k = mem2.shape
    depth, _, n = w_stack.shape
    return pl.pallas_call(
        _mem_kv_kernel,
        grid=(depth,),
        in_specs=[
            pl.BlockSpec((m, k), lambda l: (0, 0)),
            pl.BlockSpec((1, k), lambda l: (0, 0)),
            pl.BlockSpec((None, k, n), lambda l: (l, 0, 0)),
        ],
        out_specs=pl.BlockSpec((None, m, n), lambda l: (l, 0, 0)),
        out_shape=jax.ShapeDtypeStruct((depth, m, n), BF16),
        compiler_params=_params("parallel"),
        name="mem_kv_proj",
    )(mem2, gain.reshape(1, k), w_stack)


def _ffn_up_kernel(xn_ref, wg_ref, wu_ref, o_ref):
    for r in range(0, xn_ref.shape[0], ROW_SUBTILE):
        xn = xn_ref[r:r + ROW_SUBTILE, :]
        gate = jnp.dot(xn, wg_ref[...], preferred_element_type=F32)
        up = jnp.dot(xn, wu_ref[...], preferred_element_type=F32)
        o_ref[r:r + ROW_SUBTILE, :] = (gate * _sigmoid(gate) * up).astype(o_ref.dtype)


def _ffn_up(xn, w_stack, layer, tm, tn):
    m, k = xn.shape
    d_ff = w_stack.shape[2] // 2
    n_blocks = d_ff // tn
    return pl.pallas_call(
        _ffn_up_kernel,
        grid=(m // tm, n_blocks),
        in_specs=[
            pl.BlockSpec((tm, k), lambda i, j: (i, 0)),
            pl.BlockSpec((None, k, tn), lambda i, j: (layer, 0, j)),
            pl.BlockSpec((None, k, tn), lambda i, j: (layer, 0, j + n_blocks)),
        ],
        out_specs=pl.BlockSpec((tm, tn), lambda i, j: (i, j)),
        out_shape=jax.ShapeDtypeStruct((m, d_ff), BF16),
        compiler_params=_params("parallel", "arbitrary"),
        name="ffn_up",
    )(xn, w_stack, w_stack)


def _down_proj_kernel(a_ref, w_ref, r_ref, g_ref, *out_refs):
    h = r_ref[...] + jnp.dot(a_ref[...], w_ref[...], preferred_element_type=F32)
    hn_ref = out_refs[-1]
    hn_ref[...] = _rmsnorm_rows(h, g_ref[...]).astype(hn_ref.dtype)
    if len(out_refs) == 2:
        out_refs[0][...] = h


def _down_proj(a, w_stack, layer, res, gain, final, tm):
    m, k = a.shape
    n = w_stack.shape[2]
    rows = pl.BlockSpec((tm, n), lambda i: (i, 0))
    if final:
        out_specs, out_shape = rows, jax.ShapeDtypeStruct((m, n), F32)
    else:
        out_specs = [rows, rows]
        out_shape = [jax.ShapeDtypeStruct((m, n), F32), jax.ShapeDtypeStruct((m, n), BF16)]
    return pl.pallas_call(
        _down_proj_kernel,
        grid=(m // tm,),
        in_specs=[
            pl.BlockSpec((tm, k), lambda i: (i, 0)),
            pl.BlockSpec((None, k, n), lambda i: (layer, 0, 0), pipeline_mode=RESIDENT),
            rows,
            pl.BlockSpec((1, n), lambda i: (0, 0)),
        ],
        out_specs=out_specs,
        out_shape=out_shape,
        compiler_params=_params("parallel"),
        name="down_proj",
    )(a, w_stack, res, gain.reshape(1, n))


def _out_proj_kernel(y_ref, xq_ref, kv_ref, w_ref, r_ref, g_ref, h_ref, hn_ref):
    wy = y_ref.shape[1]
    d = xq_ref.shape[1] // XA_HEADS
    for r in range(0, y_ref.shape[0], OUT_PROJ_SUBTILE):
        rows = slice(r, r + OUT_PROJ_SUBTILE)
        acc = jnp.dot(y_ref[rows, :], w_ref[:wy, :], preferred_element_type=F32)
        heads = []
        for hd in range(XA_HEADS):
            q = xq_ref[rows, hd * d:(hd + 1) * d]
            k = kv_ref[:, hd * d:(hd + 1) * d]
            v = kv_ref[:, (XA_HEADS + hd) * d:(XA_HEADS + hd + 1) * d]
            s = lax.dot_general(q, k, NT_DIMS, preferred_element_type=F32) * (d ** -0.5)
            p = jnp.exp(s - jnp.max(s, axis=-1, keepdims=True))
            denom = jnp.sum(p, axis=-1, keepdims=True)
            o = jnp.dot(p.astype(BF16), v, preferred_element_type=F32)
            heads.append((o / denom).astype(BF16))
        acc += jnp.dot(jnp.concatenate(heads, axis=1), w_ref[wy:, :], preferred_element_type=F32)
        h = r_ref[rows, :] + acc
        h_ref[rows, :] = h
        hn_ref[rows, :] = _rmsnorm_rows(h, g_ref[...]).astype(hn_ref.dtype)


def _out_proj(y, proj, xq_block, kv_stack, w_stack, layer, res, gain, batch, tm):
    m, wy = y.shape
    _, k, n = w_stack.shape
    wx = k - wy
    n_mem, wkv = kv_stack.shape[1:]
    steps_per_batch = m // batch // tm
    return pl.pallas_call(
        _out_proj_kernel,
        grid=(m // tm,),
        in_specs=[
            pl.BlockSpec((tm, wy), lambda i: (i, 0)),
            pl.BlockSpec((tm, wx), lambda i: (i, xq_block)),
            pl.BlockSpec((None, n_mem, wkv), lambda i: (layer * batch + i // steps_per_batch, 0, 0)),
            pl.BlockSpec((None, k, n), lambda i: (layer, 0, 0), pipeline_mode=RESIDENT),
            pl.BlockSpec((tm, n), lambda i: (i, 0)),
            pl.BlockSpec((1, n), lambda i: (0, 0)),
        ],
        out_specs=[pl.BlockSpec((tm, n), lambda i: (i, 0)), pl.BlockSpec((tm, n), lambda i: (i, 0))],
        out_shape=[jax.ShapeDtypeStruct((m, n), F32), jax.ShapeDtypeStruct((m, n), BF16)],
        compiler_params=_params("parallel"),
        name="xattn_out_proj",
    )(y, proj, kv_stack, w_stack, res, gain.reshape(1, n))


def _rmsnorm_kernel(x_ref, g_ref, o_ref):
    o_ref[...] = _rmsnorm_rows(x_ref[...], g_ref[...]).astype(o_ref.dtype)


def _rmsnorm(x, gain, tm):
    m, k = x.shape
    return pl.pallas_call(
        _rmsnorm_kernel,
        grid=(m // tm,),
        in_specs=[pl.BlockSpec((tm, k), lambda i: (i, 0)), pl.BlockSpec((1, k), lambda i: (0, 0))],
        out_specs=pl.BlockSpec((tm, k), lambda i: (i, 0)),
        out_shape=jax.ShapeDtypeStruct((m, k), BF16),
        compiler_params=_params("parallel"),
        name="input_rmsnorm",
    )(x, gain.reshape(1, k))


def _chunk_masks():
    t = lax.broadcasted_iota(jnp.int32, (CHUNK, CHUNK), 0)
    s = lax.broadcasted_iota(jnp.int32, (CHUNK, CHUNK), 1)
    tril = (s <= t)
    return tril.astype(BF16), tril


def _chunk_cumsum(tril_b, hi, lo):
    dk = hi.shape[1]
    both = jnp.dot(tril_b, jnp.concatenate([hi, lo], axis=1), preferred_element_type=F32)
    return both[:, :dk] + both[:, dk:]


def _chunk_scores(q, k, vb, cum, sub):
    dk = q.shape[1]
    n_sub = CHUNK // sub
    blocks = [slice(sub * r, sub * (r + 1)) for r in range(n_sub)]
    mid = [cum[sub * j + sub // 2 - 1:sub * j + sub // 2, :] for j in range(n_sub)]
    last = cum[CHUNK - 1:CHUNK, :]
    zero = jnp.zeros((sub, dk), F32)

    q_slots, k_slots, k_last = [], [], []
    for j in range(n_sub):
        q_slots.append(jnp.concatenate(
            [zero if r < j else q[blocks[r], :] * jnp.exp(cum[blocks[r], :] - mid[j])
             for r in range(n_sub)], axis=0))
        k_mid = k[blocks[j], :] * jnp.exp(mid[j] - cum[blocks[j], :])
        k_slots.append(jnp.concatenate([k_mid if r == j else zero for r in range(n_sub)], axis=0))
        k_last.append(k_mid * jnp.exp(last - mid[j]))
    s_raw = lax.dot_general(jnp.concatenate(q_slots, axis=1).astype(BF16),
                            jnp.concatenate(k_slots, axis=1).astype(BF16),
                            NT_DIMS, preferred_element_type=F32)
    upd = lax.dot_general(vb, jnp.concatenate(k_last, axis=0).astype(BF16), TN_DIMS,
                          preferred_element_type=F32)
    q_dec = (q_slots[0] * jnp.exp(mid[0])).astype(BF16)
    return s_raw, upd, q_dec, jnp.exp(last), vb


def _chunk_readout(s_raw, tril, vb, q_dec, st):
    intra = jnp.dot(jnp.where(tril, s_raw, 0.0).astype(BF16), vb, preferred_element_type=F32)
    inter = lax.dot_general(q_dec, st.astype(BF16), NT_DIMS, preferred_element_type=F32)
    return intra + inter


def _gated_head_norm(o, gain, gate):
    ms = jnp.mean(o * o, axis=-1, keepdims=True)
    return o * lax.rsqrt(ms + EPS) * gain * gate


def _run_staggered(n_chunks, stages):
    n_groups = n_chunks // STAGGER_GROUP
    for slot in range(n_groups + len(stages) - 1):
        for depth in reversed(range(len(stages))):
            group = slot - depth
            if 0 <= group < n_groups:
                for c in range(group * STAGGER_GROUP, (group + 1) * STAGGER_GROUP):
                    stages[depth](c)


def _mixer_kernel(sub, q_ref, k_ref, v_ref, gate_ref, hi_ref, lo_ref, on_ref, y_ref, st_ref):
    @pl.when(pl.program_id(2) == 0)
    def _():
        st_ref[...] = jnp.zeros_like(st_ref)

    tril_b, tril = _chunk_masks()
    gain = on_ref[...]
    n_chunks = q_ref.shape[0] // CHUNK
    rows = [slice(c * CHUNK, (c + 1) * CHUNK) for c in range(n_chunks)]
    ctx = [{} for _ in range(n_chunks)]
    state = [st_ref[...]]

    def decay(c):
        ctx[c]["cum"] = _chunk_cumsum(tril_b, hi_ref[rows[c], :], lo_ref[rows[c], :])

    def scores(c):
        ctx[c]["parts"] = _chunk_scores(q_ref[rows[c], :].astype(F32), k_ref[rows[c], :].astype(F32),
                                        v_ref[rows[c], :], ctx[c].pop("cum"), sub)

    def readout(c):
        s_raw, upd, q_dec, e_last, vb = ctx[c].pop("parts")
        ctx[c]["o"] = _chunk_readout(s_raw, tril, vb, q_dec, state[0])
        state[0] = state[0] * e_last + upd

    def emit(c):
        gate = gate_ref[rows[c], :].astype(F32)
        y_ref[rows[c], :] = _gated_head_norm(ctx[c].pop("o"), gain, gate).astype(y_ref.dtype)

    _run_staggered(n_chunks, [decay, scores, readout, emit])
    st_ref[...] = state[0]


def _mixer(proj, onorm, cols, heads, dk, dv, sub, batch, t_blk, name):
    m = proj.shape[0]
    n_t = m // batch // t_blk

    def group(key, width):
        block0, rem = divmod(cols[key], width)
        assert rem == 0
        return pl.BlockSpec((t_blk, width), lambda b, h, t: (b * n_t + t, block0 + h))

    return pl.pallas_call(
        functools.partial(_mixer_kernel, sub),
        grid=(batch, heads, n_t),
        in_specs=[group("q", dk), group("k", dk), group("v", dv), group("gate", dv),
                  group("hi", dk), group("lo", dk),
                  pl.BlockSpec((1, dv), lambda b, h, t: (0, 0))],
        out_specs=pl.BlockSpec((t_blk, dv), lambda b, h, t: (b * n_t + t, h)),
        out_shape=jax.ShapeDtypeStruct((m, heads * dv), BF16),
        scratch_shapes=[pltpu.VMEM((dv, dk), F32)],
        compiler_params=_params("parallel", "parallel", "arbitrary"),
        name=name,
    )(proj, proj, proj, proj, proj, proj, onorm.reshape(1, dv))


def _gla_pack_weights(w_in, w_gk, b_gk, mix_width):
    heads, rank, dkp = GLA_HEADS, GLA_GATE_RANK, GLA_DK_PAD
    kw = w_gk.shape[1]
    dk = kw // heads
    q, k, v, g, gl, xq = jnp.split(
        w_in, [kw, 2 * kw, 2 * kw + mix_width, 2 * kw + 2 * mix_width, 2 * kw + 2 * mix_width + rank],
        axis=1)

    def pad_heads(w):
        w = w.reshape(w.shape[0], heads, dk)
        return jnp.pad(w, ((0, 0), (0, 0), (0, dkp - dk))).reshape(w.shape[0], heads * dkp)

    gl = jnp.pad(gl, ((0, 0), (0, LANE - rank)))
    w_packed = jnp.concatenate([v, g, pad_heads(q), pad_heads(k), xq, gl], axis=1).astype(BF16)
    w_gk_pad = jnp.pad(pad_heads(w_gk), ((0, LANE - rank), (0, 0))).astype(BF16)
    b_gk_pad = pad_heads(b_gk.reshape(1, kw)).astype(F32)
    return w_packed, w_gk_pad, b_gk_pad, dk


def kernel(x, mem, norm_mix, norm_ffn, norm_mem, norm_final, hgrn_w_in, hgrn_lb_logits, hgrn_onorm,
           gla_w_in, gla_w_gk, gla_b_gk, gla_onorm, w_mem_kv, w_out, w_gate_up, w_down):
    batch, seq, d_model = x.shape
    n_mem = mem.shape[1]
    depth = norm_mix.shape[0]
    mix_width = hgrn_lb_logits.shape[1]
    xa_width = d_model - mix_width
    m = batch * seq

    h = x.reshape(m, d_model)
    kv = _mem_kv(mem.reshape(batch * n_mem, d_model), norm_mem, w_mem_kv.astype(BF16))
    kv = kv.reshape(depth * batch, n_mem, 2 * xa_width)

    hgrn_w = hgrn_w_in.astype(BF16)
    w_out_b = w_out.astype(BF16)
    w_gate_up_b = w_gate_up.astype(BF16)
    w_down_b = w_down.astype(BF16)

    hn = _rmsnorm(h, norm_mix[0], tm=512)
    for layer in range(depth):
        j = layer // 2
        if layer % 2 == 0:
            proj = _hgrn_in_proj(hn, hgrn_w, j, hgrn_lb_logits, tm=256)
            w = mix_width
            cols = dict(q=0, hi=w, lo=2 * w, k=3 * w, v=4 * w, gate=5 * w)
            y = _mixer(proj, hgrn_onorm[j], cols, mix_width // HGRN_HEAD_DIM, HGRN_HEAD_DIM,
                       HGRN_HEAD_DIM, HGRN_SUB, batch, t_blk=2048, name="hgrn2_mixer")
            xq_col = 6 * w
        else:
            w_packed, w_gk_pad, b_gk_pad, dk = _gla_pack_weights(
                gla_w_in[j], gla_w_gk[j], gla_b_gk[j], mix_width)
            proj = _gla_in_proj(hn, w_packed, w_gk_pad, b_gk_pad, mix_width, xa_width, dk, tm=512)
            wk = GLA_HEADS * GLA_DK_PAD
            xq_col = 2 * mix_width + 2 * wk
            cols = dict(v=0, gate=mix_width, q=2 * mix_width, k=2 * mix_width + wk,
                        hi=xq_col + xa_width, lo=xq_col + xa_width + wk)
            y = _mixer(proj, gla_onorm[j], cols, GLA_HEADS, GLA_DK_PAD, mix_width // GLA_HEADS,
                       GLA_SUB, batch, t_blk=2048, name="gla_mixer")
        assert xq_col % xa_width == 0
        h, hn = _out_proj(y, proj, xq_col // xa_width, kv, w_out_b, layer, h, norm_ffn[layer], batch,
                          tm=512)
        act = _ffn_up(hn, w_gate_up_b, layer, tm=2048, tn=512)
        if layer + 1 < depth:
            h, hn = _down_proj(act, w_down_b, layer, h, norm_mix[layer + 1], final=False, tm=256)
        else:
            out = _down_proj(act, w_down_b, layer, h, norm_final, final=True, tm=256)
    return out.reshape(batch, seq, d_model)
```

```python
import functools

import jax
import jax.numpy as jnp
from jax import lax
from jax.experimental import pallas as pl
from jax.experimental.pallas import tpu as pltpu

F32 = jnp.float32
BF16 = jnp.bfloat16

EPS = 1e-6
CHUNK = 64
STAGGER_GROUP = 8
ROW_SUBTILE = 256
OUT_PROJ_SUBTILE = 512
HGRN_SUB = 32
GLA_SUB = 64
XA_HEADS = 4
HGRN_HEAD_DIM = 128
GLA_HEADS = 4
GLA_GATE_RANK = 16
GLA_GATE_NORMALIZER = 16.0
LANE = 128
GLA_DK_PAD = 256
V7X_VMEM_LIMIT = 56 * 1024 * 1024

NT_DIMS = (((1,), (1,)), ((), ()))
TN_DIMS = (((0,), (0,)), ((), ()))

RESIDENT = pl.Buffered(1)


def _params(*semantics):
    return pltpu.CompilerParams(dimension_semantics=semantics, vmem_limit_bytes=V7X_VMEM_LIMIT)


def _sigmoid(x):
    return 1.0 / (1.0 + jnp.exp(-x))


def _rmsnorm_rows(x, gain):
    ms = jnp.mean(x * x, axis=-1, keepdims=True)
    return x * lax.rsqrt(ms + EPS) * gain


def _split_hi_lo(g):
    hi = g.astype(BF16)
    return hi, (g - hi.astype(F32)).astype(BF16)


def _hgrn_in_proj_kernel(layer_j, xn_ref, w_ref, lbl_ref, o_ref):
    xn = xn_ref[...]
    w = lbl_ref.shape[1]

    logits = lbl_ref[...]
    ex = jnp.exp(logits - jnp.max(logits, axis=0, keepdims=True))
    sm = ex / jnp.sum(ex, axis=0, keepdims=True)
    lb = jnp.zeros((1, w), F32)
    for r in range(1, layer_j + 1):
        lb = lb + sm[r:r + 1, :]

    def proj(group):
        return jnp.dot(xn, w_ref[:, group * w:(group + 1) * w], preferred_element_type=F32)

    o_ref[:, 4 * w:5 * w] = proj(2).astype(BF16)
    f = lb + (1.0 - lb) * _sigmoid(proj(1))
    hi, lo = _split_hi_lo(jnp.log(f))
    o_ref[:, w:2 * w] = hi
    o_ref[:, 2 * w:3 * w] = lo
    o_ref[:, 3 * w:4 * w] = (1.0 - f).astype(BF16)
    q = proj(0)
    o_ref[:, 0:w] = (q * _sigmoid(q)).astype(BF16)
    g = proj(3)
    o_ref[:, 5 * w:6 * w] = (g * _sigmoid(g)).astype(BF16)
    o_ref[:, 6 * w:] = jnp.dot(xn, w_ref[:, 4 * w:], preferred_element_type=F32).astype(BF16)


def _hgrn_in_proj(xn, w_stack, layer_j, lb_logits, tm):
    m, k = xn.shape
    n_in = w_stack.shape[2]
    n_a, w = lb_logits.shape
    n_out = n_in + 2 * w
    return pl.pallas_call(
        functools.partial(_hgrn_in_proj_kernel, layer_j),
        grid=(m // tm,),
        in_specs=[
            pl.BlockSpec((tm, k), lambda i: (i, 0)),
            pl.BlockSpec((None, k, n_in), lambda i: (layer_j, 0, 0), pipeline_mode=RESIDENT),
            pl.BlockSpec((n_a, w), lambda i: (0, 0)),
        ],
        out_specs=pl.BlockSpec((tm, n_out), lambda i: (i, 0)),
        out_shape=jax.ShapeDtypeStruct((m, n_out), BF16),
        compiler_params=_params("parallel"),
        name="hgrn_in_proj",
    )(xn, w_stack, lb_logits)


def _gla_in_proj_kernel(dk_true, wv, wx, xn_ref, w_ref, wgk_ref, bgk_ref, o_ref):
    xn = xn_ref[...]
    wk = wgk_ref.shape[1]

    def proj(lo, hi):
        return jnp.dot(xn, w_ref[:, lo:hi], preferred_element_type=F32)

    c = 2 * wv + 2 * wk
    low = proj(c + wx, c + wx + LANE).astype(BF16)
    o_ref[:, 0:wv] = proj(0, wv).astype(BF16)
    x = jnp.dot(low, wgk_ref[...], preferred_element_type=F32) + bgk_ref[...]
    log_a = (jnp.minimum(x, 0.0) - jnp.log1p(jnp.exp(-jnp.abs(x)))) / GLA_GATE_NORMALIZER
    hi, lo = _split_hi_lo(log_a)
    o_ref[:, c + wx:c + wx + wk] = hi
    o_ref[:, c + wx + wk:] = lo
    g = proj(wv, 2 * wv)
    o_ref[:, wv:2 * wv] = (g * _sigmoid(g)).astype(BF16)
    c = 2 * wv
    o_ref[:, c:c + wk] = (proj(c, c + wk) * (dk_true ** -0.5)).astype(BF16)
    o_ref[:, c + wk:c + 2 * wk] = proj(c + wk, c + 2 * wk).astype(BF16)
    c += 2 * wk
    o_ref[:, c:c + wx] = proj(c, c + wx).astype(BF16)


def _gla_in_proj(xn, w_packed, w_gk_pad, b_gk_pad, mix_width, xa_width, dk_true, tm):
    m, k = xn.shape
    n_in = w_packed.shape[1]
    wk = w_gk_pad.shape[1]
    assert n_in == 2 * mix_width + 2 * wk + xa_width + LANE
    n_out = n_in - LANE + 2 * wk
    return pl.pallas_call(
        functools.partial(_gla_in_proj_kernel, dk_true, mix_width, xa_width),
        grid=(m // tm,),
        in_specs=[
            pl.BlockSpec((tm, k), lambda i: (i, 0)),
            pl.BlockSpec((k, n_in), lambda i: (0, 0), pipeline_mode=RESIDENT),
            pl.BlockSpec((LANE, wk), lambda i: (0, 0)),
            pl.BlockSpec((1, wk), lambda i: (0, 0)),
        ],
        out_specs=pl.BlockSpec((tm, n_out), lambda i: (i, 0)),
        out_shape=jax.ShapeDtypeStruct((m, n_out), BF16),
        compiler_params=_params("parallel"),
        name="gla_in_proj",
    )(xn, w_packed, w_gk_pad, b_gk_pad)


def _mem_kv_kernel(x_ref, g_ref, w_ref, o_ref):
    xn = _rmsnorm_rows(x_ref[...], g_ref[...]).astype(BF16)
    o_ref[...] = jnp.dot(xn, w_ref[...], preferred_element_type=F32).astype(o_ref.dtype)


def _mem_kv(mem2, gain, w_stack):
    m, k = mem2.shape
    depth, _, n = w_stack.shape
    return pl.pallas_call(
        _mem_kv_kernel,
        grid=(depth,),
        in_specs=[
            pl.BlockSpec((m, k), lambda l: (0, 0)),
            pl.BlockSpec((1, k), lambda l: (0, 0)),
            pl.BlockSpec((None, k, n), lambda l: (l, 0, 0)),
        ],
        out_specs=pl.BlockSpec((None, m, n), lambda l: (l, 0, 0)),
        out_shape=jax.ShapeDtypeStruct((depth, m, n), BF16),
        compiler_params=_params("parallel"),
        name="mem_kv_proj",
    )(mem2, gain.reshape(1, k), w_stack)


def _ffn_up_kernel(xn_ref, wg_ref, wu_ref, o_ref):
    wg = wg_ref[...].astype(BF16)
    wu = wu_ref[...].astype(BF16)
    for r in range(0, xn_ref.shape[0], ROW_SUBTILE):
        xn = xn_ref[r:r + ROW_SUBTILE, :]
        gate = jnp.dot(xn, wg, preferred_element_type=F32)
        up = jnp.dot(xn, wu, preferred_element_type=F32)
        o_ref[r:r + ROW_SUBTILE, :] = (gate * _sigmoid(gate) * up).astype(o_ref.dtype)


def _ffn_up(xn, w_stack, layer, tm, tn):
    m, k = xn.shape
    d_ff = w_stack.shape[2] // 2
    n_blocks = d_ff // tn
    return pl.pallas_call(
        _ffn_up_kernel,
        grid=(m // tm, n_blocks),
        in_specs=[
            pl.BlockSpec((tm, k), lambda i, j: (i, 0)),
            pl.BlockSpec((None, k, tn), lambda i, j: (layer, 0, j)),
            pl.BlockSpec((None, k, tn), lambda i, j: (layer, 0, j + n_blocks)),
        ],
        out_specs=pl.BlockSpec((tm, tn), lambda i, j: (i, j)),
        out_shape=jax.ShapeDtypeStruct((m, d_ff), BF16),
        compiler_params=_params("parallel", "arbitrary"),
        name="ffn_up",
    )(xn, w_stack, w_stack)


def _down_proj_kernel(a_ref, w_ref, r_ref, g_ref, *out_refs):
    h = r_ref[...] + jnp.dot(a_ref[...], w_ref[...], preferred_element_type=F32)
    hn_ref = out_refs[-1]
    hn_ref[...] = _rmsnorm_rows(h, g_ref[...]).astype(hn_ref.dtype)
    if len(out_refs) == 2:
        out_refs[0][...] = h


def _down_proj(a, w_stack, layer, res, gain, final, tm):
    m, k = a.shape
    n = w_stack.shape[2]
    rows = pl.BlockSpec((tm, n), lambda i: (i, 0))
    if final:
        out_specs, out_shape = rows, jax.ShapeDtypeStruct((m, n), F32)
    else:
        out_specs = [rows, rows]
        out_shape = [jax.ShapeDtypeStruct((m, n), F32), jax.ShapeDtypeStruct((m, n), BF16)]
    return pl.pallas_call(
        _down_proj_kernel,
        grid=(m // tm,),
        in_specs=[
            pl.BlockSpec((tm, k), lambda i: (i, 0)),
            pl.BlockSpec((None, k, n), lambda i: (layer, 0, 0), pipeline_mode=RESIDENT),
            rows,
            pl.BlockSpec((1, n), lambda i: (0, 0)),
        ],
        out_specs=out_specs,
        out_shape=out_shape,
        compiler_params=_params("parallel"),
        name="down_proj",
    )(a, w_stack, res, gain.reshape(1, n))


def _out_proj_kernel(y_ref, xq_ref, kv_ref, w_ref, r_ref, g_ref, h_ref, hn_ref):
    wy = y_ref.shape[1]
    d = xq_ref.shape[1] // XA_HEADS
    for r in range(0, y_ref.shape[0], OUT_PROJ_SUBTILE):
        rows = slice(r, r + OUT_PROJ_SUBTILE)
        acc = jnp.dot(y_ref[rows, :], w_ref[:wy, :], preferred_element_type=F32)
        heads = []
        for hd in range(XA_HEADS):
            q = xq_ref[rows, hd * d:(hd + 1) * d]
            k = kv_ref[:, hd * d:(hd + 1) * d]
            v = kv_ref[:, (XA_HEADS + hd) * d:(XA_HEADS + hd + 1) * d]
            s = lax.dot_general(q, k, NT_DIMS, preferred_element_type=F32) * (d ** -0.5)
            p = jnp.exp(s - jnp.max(s, axis=-1, keepdims=True))
            denom = jnp.sum(p, axis=-1, keepdims=True)
            o = jnp.dot(p.astype(BF16), v, preferred_element_type=F32)
            heads.append((o / denom).astype(BF16))
        acc += jnp.dot(jnp.concatenate(heads, axis=1), w_ref[wy:, :], preferred_element_type=F32)
        h = r_ref[rows, :] + acc
        h_ref[rows, :] = h
        hn_ref[rows, :] = _rmsnorm_rows(h, g_ref[...]).astype(hn_ref.dtype)


def _out_proj(y, proj, xq_block, kv_stack, w_stack, layer, res, gain, batch, tm):
    m, wy = y.shape
    _, k, n = w_stack.shape
    wx = k - wy
    n_mem, wkv = kv_stack.shape[1:]
    steps_per_batch = m // batch // tm
    return pl.pallas_call(
        _out_proj_kernel,
        grid=(m // tm,),
        in_specs=[
            pl.BlockSpec((tm, wy), lambda i: (i, 0)),
            pl.BlockSpec((tm, wx), lambda i: (i, xq_block)),
            pl.BlockSpec((None, n_mem, wkv), lambda i: (layer * batch + i // steps_per_batch, 0, 0)),
            pl.BlockSpec((None, k, n), lambda i: (layer, 0, 0), pipeline_mode=RESIDENT),
            pl.BlockSpec((tm, n), lambda i: (i, 0)),
            pl.BlockSpec((1, n), lambda i: (0, 0)),
        ],
        out_specs=[pl.BlockSpec((tm, n), lambda i: (i, 0)), pl.BlockSpec((tm, n), lambda i: (i, 0))],
        out_shape=[jax.ShapeDtypeStruct((m, n), F32), jax.ShapeDtypeStruct((m, n), BF16)],
        compiler_params=_params("parallel"),
        name="xattn_out_proj",
    )(y, proj, kv_stack, w_stack, res, gain.reshape(1, n))


def _rmsnorm_kernel(x_ref, g_ref, o_ref):
    o_ref[...] = _rmsnorm_rows(x_ref[...], g_ref[...]).astype(o_ref.dtype)


def _rmsnorm(x, gain, tm):
    m, k = x.shape
    return pl.pallas_call(
        _rmsnorm_kernel,
        grid=(m // tm,),
        in_specs=[pl.BlockSpec((tm, k), lambda i: (i, 0)), pl.BlockSpec((1, k), lambda i: (0, 0))],
        out_specs=pl.BlockSpec((tm, k), lambda i: (i, 0)),
        out_shape=jax.ShapeDtypeStruct((m, k), BF16),
        compiler_params=_params("parallel"),
        name="input_rmsnorm",
    )(x, gain.reshape(1, k))


def _chunk_masks():
    t = lax.broadcasted_iota(jnp.int32, (CHUNK, CHUNK), 0)
    s = lax.broadcasted_iota(jnp.int32, (CHUNK, CHUNK), 1)
    tril = (s <= t)
    return tril.astype(BF16), tril


def _chunk_cumsum(tril_b, hi, lo):
    dk = hi.shape[1]
    both = jnp.dot(tril_b, jnp.concatenate([hi, lo], axis=1), preferred_element_type=F32)
    return both[:, :dk] + both[:, dk:]


def _chunk_scores(q, k, vb, cum, sub):
    dk = q.shape[1]
    n_sub = CHUNK // sub
    blocks = [slice(sub * r, sub * (r + 1)) for r in range(n_sub)]
    mid = [cum[sub * j + sub // 2 - 1:sub * j + sub // 2, :] for j in range(n_sub)]
    last = cum[CHUNK - 1:CHUNK, :]
    zero = jnp.zeros((sub, dk), F32)

    q_slots, k_slots, k_last = [], [], []
    for j in range(n_sub):
        q_slots.append(jnp.concatenate(
            [zero if r < j else q[blocks[r], :] * jnp.exp(cum[blocks[r], :] - mid[j])
             for r in range(n_sub)], axis=0))
        k_mid = k[blocks[j], :] * jnp.exp(mid[j] - cum[blocks[j], :])
        k_slots.append(jnp.concatenate([k_mid if r == j else zero for r in range(n_sub)], axis=0))
        k_last.append(k_mid * jnp.exp(last - mid[j]))
    s_raw = lax.dot_general(jnp.concatenate(q_slots, axis=1).astype(BF16),
                            jnp.concatenate(k_slots, axis=1).astype(BF16),
                            NT_DIMS, preferred_element_type=F32)
    upd = lax.dot_general(vb, jnp.concatenate(k_last, axis=0).astype(BF16), TN_DIMS,
                          preferred_element_type=F32)
    q_dec = (q_slots[0] * jnp.exp(mid[0])).astype(BF16)
    return s_raw, upd, q_dec, jnp.exp(last), vb


def _chunk_readout(s_raw, tril, vb, q_dec, st):
    intra = jnp.dot(jnp.where(tril, s_raw, 0.0).astype(BF16), vb, preferred_element_type=F32)
    inter = lax.dot_general(q_dec, st.astype(BF16), NT_DIMS, preferred_element_type=F32)
    return intra + inter


def _gated_head_norm(o, gain, gate):
    ms = jnp.mean(o * o, axis=-1, keepdims=True)
    return o * lax.rsqrt(ms + EPS) * gain * gate


def _run_staggered(n_chunks, stages):
    n_groups = n_chunks // STAGGER_GROUP
    for slot in range(n_groups + len(stages) - 1):
        for depth in reversed(range(len(stages))):
            group = slot - depth
            if 0 <= group < n_groups:
                for c in range(group * STAGGER_GROUP, (group + 1) * STAGGER_GROUP):
                    stages[depth](c)


def _mixer_kernel(sub, q_ref, k_ref, v_ref, gate_ref, hi_ref, lo_ref, on_ref, y_ref, st_ref):
    @pl.when(pl.program_id(2) == 0)
    def _():
        st_ref[...] = jnp.zeros_like(st_ref)

    tril_b, tril = _chunk_masks()
    gain = on_ref[...]
    n_chunks = q_ref.shape[0] // CHUNK
    rows = [slice(c * CHUNK, (c + 1) * CHUNK) for c in range(n_chunks)]
    ctx = [{} for _ in range(n_chunks)]
    state = [st_ref[...]]

    def decay(c):
        ctx[c]["cum"] = _chunk_cumsum(tril_b, hi_ref[rows[c], :], lo_ref[rows[c], :])

    def scores(c):
        ctx[c]["parts"] = _chunk_scores(q_ref[rows[c], :].astype(F32), k_ref[rows[c], :].astype(F32),
                                        v_ref[rows[c], :], ctx[c].pop("cum"), sub)

    def readout(c):
        s_raw, upd, q_dec, e_last, vb = ctx[c].pop("parts")
        ctx[c]["o"] = _chunk_readout(s_raw, tril, vb, q_dec, state[0])
        state[0] = state[0] * e_last + upd

    def emit(c):
        gate = gate_ref[rows[c], :].astype(F32)
        y_ref[rows[c], :] = _gated_head_norm(ctx[c].pop("o"), gain, gate).astype(y_ref.dtype)

    _run_staggered(n_chunks, [decay, scores, readout, emit])
    st_ref[...] = state[0]


def _mixer(proj, onorm, cols, heads, dk, dv, sub, batch, t_blk, name):
    m = proj.shape[0]
    n_t = m // batch // t_blk

    def group(key, width):
        block0, rem = divmod(cols[key], width)
        assert rem == 0
        return pl.BlockSpec((t_blk, width), lambda b, h, t: (b * n_t + t, block0 + h))

    return pl.pallas_call(
        functools.partial(_mixer_kernel, sub),
        grid=(batch, heads, n_t),
        in_specs=[group("q", dk), group("k", dk), group("v", dv), group("gate", dv),
                  group("hi", dk), group("lo", dk),
                  pl.BlockSpec((1, dv), lambda b, h, t: (0, 0))],
        out_specs=pl.BlockSpec((t_blk, dv), lambda b, h, t: (b * n_t + t, h)),
        out_shape=jax.ShapeDtypeStruct((m, heads * dv), BF16),
        scratch_shapes=[pltpu.VMEM((dv, dk), F32)],
        compiler_params=_params("parallel", "parallel", "arbitrary"),
        name=name,
    )(proj, proj, proj, proj, proj, proj, onorm.reshape(1, dv))


def _gla_pack_weights(w_in, w_gk, b_gk, mix_width):
    heads, rank, dkp = GLA_HEADS, GLA_GATE_RANK, GLA_DK_PAD
    kw = w_gk.shape[1]
    dk = kw // heads
    q, k, v, g, gl, xq = jnp.split(
        w_in.astype(BF16),
        [kw, 2 * kw, 2 * kw + mix_width, 2 * kw + 2 * mix_width, 2 * kw + 2 * mix_width + rank], axis=1)

    def pad_heads(w):
        w = w.reshape(w.shape[0], heads, dk)
        return jnp.pad(w, ((0, 0), (0, 0), (0, dkp - dk))).reshape(w.shape[0], heads * dkp)

    gl = jnp.pad(gl, ((0, 0), (0, LANE - rank)))
    w_packed = jnp.concatenate([v, g, pad_heads(q), pad_heads(k), xq, gl], axis=1)
    w_gk_pad = jnp.pad(pad_heads(w_gk), ((0, LANE - rank), (0, 0))).astype(BF16)
    b_gk_pad = pad_heads(b_gk.reshape(1, kw)).astype(F32)
    return w_packed, w_gk_pad, b_gk_pad, dk


def kernel(x, mem, norm_mix, norm_ffn, norm_mem, norm_final, hgrn_w_in, hgrn_lb_logits, hgrn_onorm,
           gla_w_in, gla_w_gk, gla_b_gk, gla_onorm, w_mem_kv, w_out, w_gate_up, w_down):
    batch, seq, d_model = x.shape
    n_mem = mem.shape[1]
    depth = norm_mix.shape[0]
    mix_width = hgrn_lb_logits.shape[1]
    xa_width = d_model - mix_width
    m = batch * seq

    h = x.reshape(m, d_model)
    kv = _mem_kv(mem.reshape(batch * n_mem, d_model), norm_mem, w_mem_kv.astype(BF16))
    kv = kv.reshape(depth * batch, n_mem, 2 * xa_width)

    hgrn_w = hgrn_w_in.astype(BF16)
    w_out_b = w_out.astype(BF16)
    w_down_b = w_down.astype(BF16)

    hn = _rmsnorm(h, norm_mix[0], tm=512)
    for layer in range(depth):
        j = layer // 2
        if layer % 2 == 0:
            proj = _hgrn_in_proj(hn, hgrn_w, j, hgrn_lb_logits, tm=256)
            w = mix_width
            cols = dict(q=0, hi=w, lo=2 * w, k=3 * w, v=4 * w, gate=5 * w)
            y = _mixer(proj, hgrn_onorm[j], cols, mix_width // HGRN_HEAD_DIM, HGRN_HEAD_DIM,
                       HGRN_HEAD_DIM, HGRN_SUB, batch, t_blk=2048, name="hgrn2_mixer")
            xq_col = 6 * w
        else:
            w_packed, w_gk_pad, b_gk_pad, dk = _gla_pack_weights(
                gla_w_in[j], gla_w_gk[j], gla_b_gk[j], mix_width)
            proj = _gla_in_proj(hn, w_packed, w_gk_pad, b_gk_pad, mix_width, xa_width, dk, tm=512)
            wk = GLA_HEADS * GLA_DK_PAD
            xq_col = 2 * mix_width + 2 * wk
            cols = dict(v=0, gate=mix_width, q=2 * mix_width, k=2 * mix_width + wk,
                        hi=xq_col + xa_width, lo=xq_col + xa_width + wk)
            y = _mixer(proj, gla_onorm[j], cols, GLA_HEADS, GLA_DK_PAD, mix_width // GLA_HEADS,
                       GLA_SUB, batch, t_blk=2048, name="gla_mixer")
        assert xq_col % xa_width == 0
        h, hn = _out_proj(y, proj, xq_col // xa_width, kv, w_out_b, layer, h, norm_ffn[layer], batch,
                          tm=512)
        act = _ffn_up(hn, w_gate_up, layer, tm=2048, tn=512)
        if layer + 1 < depth:
            h, hn = _down_proj(act, w_down_b, layer, h, norm_mix[layer + 1], final=False, tm=256)
        else:
            out = _down_proj(act, w_down_b, layer, h, norm_final, final=True, tm=256)
    return out.reshape(batch, seq, d_model)
```

```python
import functools

import jax
import jax.numpy as jnp
from jax import lax
from jax.experimental import pallas as pl
from jax.experimental.pallas import tpu as pltpu

F32 = jnp.float32
BF16 = jnp.bfloat16

EPS = 1e-6
CHUNK = 64
STAGGER_GROUP = 8
ROW_SUBTILE = 256
OUT_PROJ_SUBTILE = 512
IN_PROJ_SUBTILE = 256
HGRN_SUB = 32
GLA_SUB = 64
XA_HEADS = 4
HGRN_HEAD_DIM = 128
GLA_HEADS = 4
GLA_GATE_RANK = 16
GLA_GATE_NORMALIZER = 16.0
LANE = 128
GLA_DK_PAD = 256
V7X_VMEM_LIMIT = 56 * 1024 * 1024

NT_DIMS = (((1,), (1,)), ((), ()))
TN_DIMS = (((0,), (0,)), ((), ()))

RESIDENT = pl.Buffered(1)


def _params(*semantics):
    return pltpu.CompilerParams(dimension_semantics=semantics, vmem_limit_bytes=V7X_VMEM_LIMIT)


def _sigmoid(x):
    return 1.0 / (1.0 + jnp.exp(-x))


def _rmsnorm_rows(x, gain):
    ms = jnp.mean(x * x, axis=-1, keepdims=True)
    return x * lax.rsqrt(ms + EPS) * gain


def _split_hi_lo(g):
    hi = g.astype(BF16)
    return hi, (g - hi.astype(F32)).astype(BF16)


def _hgrn_in_proj_kernel(layer_j, xn_ref, w_ref, lbl_ref, o_ref):
    w = lbl_ref.shape[1]

    logits = lbl_ref[...]
    ex = jnp.exp(logits - jnp.max(logits, axis=0, keepdims=True))
    sm = ex / jnp.sum(ex, axis=0, keepdims=True)
    lb = jnp.zeros((1, w), F32)
    for r in range(1, layer_j + 1):
        lb = lb + sm[r:r + 1, :]

    for r in range(0, xn_ref.shape[0], IN_PROJ_SUBTILE):
        rows = slice(r, r + IN_PROJ_SUBTILE)
        xn = xn_ref[rows, :]

        def proj(lo, hi):
            return jnp.dot(xn, w_ref[:, lo:hi], preferred_element_type=F32)

        o_ref[rows, 4 * w:5 * w] = proj(2 * w, 3 * w).astype(BF16)
        f = lb + (1.0 - lb) * _sigmoid(proj(w, 2 * w))
        hi, lo = _split_hi_lo(jnp.log(f))
        o_ref[rows, w:2 * w] = hi
        o_ref[rows, 2 * w:3 * w] = lo
        o_ref[rows, 3 * w:4 * w] = (1.0 - f).astype(BF16)
        q = proj(0, w)
        o_ref[rows, 0:w] = (q * _sigmoid(q)).astype(BF16)
        g = proj(3 * w, 4 * w)
        o_ref[rows, 5 * w:6 * w] = (g * _sigmoid(g)).astype(BF16)
        o_ref[rows, 6 * w:] = proj(4 * w, w_ref.shape[1]).astype(BF16)


def _hgrn_in_proj(xn, w_stack, layer_j, lb_logits, tm):
    m, k = xn.shape
    n_in = w_stack.shape[2]
    n_a, w = lb_logits.shape
    n_out = n_in + 2 * w
    return pl.pallas_call(
        functools.partial(_hgrn_in_proj_kernel, layer_j),
        grid=(m // tm,),
        in_specs=[
            pl.BlockSpec((tm, k), lambda i: (i, 0)),
            pl.BlockSpec((None, k, n_in), lambda i: (layer_j, 0, 0), pipeline_mode=RESIDENT),
            pl.BlockSpec((n_a, w), lambda i: (0, 0)),
        ],
        out_specs=pl.BlockSpec((tm, n_out), lambda i: (i, 0)),
        out_shape=jax.ShapeDtypeStruct((m, n_out), BF16),
        compiler_params=_params("parallel"),
        name="hgrn_in_proj",
    )(xn, w_stack, lb_logits)


def _gla_in_proj_kernel(dk_true, wv, wx, xn_ref, w_ref, wgk_ref, bgk_ref, o_ref):
    wk = wgk_ref.shape[1]
    c_q = 2 * wv
    c_x = c_q + 2 * wk

    for r in range(0, xn_ref.shape[0], IN_PROJ_SUBTILE):
        rows = slice(r, r + IN_PROJ_SUBTILE)
        xn = xn_ref[rows, :]

        def proj(lo, hi):
            return jnp.dot(xn, w_ref[:, lo:hi], preferred_element_type=F32)

        def log_decay(head):
            cols = slice(head * GLA_DK_PAD, (head + 1) * GLA_DK_PAD)
            x = jnp.dot(low, wgk_ref[:, cols], preferred_element_type=F32) + bgk_ref[:, cols]
            log_a = (jnp.minimum(x, 0.0) - jnp.log1p(jnp.exp(-jnp.abs(x)))) / GLA_GATE_NORMALIZER
            hi, lo = _split_hi_lo(log_a)
            o_ref[rows, c_x + wx + head * GLA_DK_PAD:c_x + wx + (head + 1) * GLA_DK_PAD] = hi
            o_ref[rows, c_x + wx + wk + head * GLA_DK_PAD:c_x + wx + wk + (head + 1) * GLA_DK_PAD] = lo

        low = proj(c_x + wx, c_x + wx + LANE).astype(BF16)
        o_ref[rows, 0:wv] = proj(0, wv).astype(BF16)
        log_decay(0)
        g = proj(wv, c_q)
        o_ref[rows, wv:c_q] = (g * _sigmoid(g)).astype(BF16)
        log_decay(1)
        o_ref[rows, c_q:c_q + wk] = (proj(c_q, c_q + wk) * (dk_true ** -0.5)).astype(BF16)
        log_decay(2)
        o_ref[rows, c_q + wk:c_x] = proj(c_q + wk, c_x).astype(BF16)
        log_decay(3)
        o_ref[rows, c_x:c_x + wx] = proj(c_x, c_x + wx).astype(BF16)


def _gla_in_proj(xn, w_packed, w_gk_pad, b_gk_pad, mix_width, xa_width, dk_true, tm):
    m, k = xn.shape
    n_in = w_packed.shape[1]
    wk = w_gk_pad.shape[1]
    assert n_in == 2 * mix_width + 2 * wk + xa_width + LANE
    n_out = n_in - LANE + 2 * wk
    return pl.pallas_call(
        functools.partial(_gla_in_proj_kernel, dk_true, mix_width, xa_width),
        grid=(m // tm,),
        in_specs=[
            pl.BlockSpec((tm, k), lambda i: (i, 0)),
            pl.BlockSpec((k, n_in), lambda i: (0, 0), pipeline_mode=RESIDENT),
            pl.BlockSpec((LANE, wk), lambda i: (0, 0)),
            pl.BlockSpec((1, wk), lambda i: (0, 0)),
        ],
        out_specs=pl.BlockSpec((tm, n_out), lambda i: (i, 0)),
        out_shape=jax.ShapeDtypeStruct((m, n_out), BF16),
        compiler_params=_params("parallel"),
        name="gla_in_proj",
    )(xn, w_packed, w_gk_pad, b_gk_pad)


def _mem_kv_kernel(x_ref, g_ref, w_ref, o_ref):
    xn = _rmsnorm_rows(x_ref[...], g_ref[...]).astype(BF16)
    o_ref[...] = jnp.dot(xn, w_ref[...], preferred_element_type=F32).astype(o_ref.dtype)


def _mem_kv(mem2, gain, w_stack):
    m, k = mem2.shape
    depth, _, n = w_stack.shape
    return pl.pallas_call(
        _mem_kv_kernel,
        grid=(depth,),
        in_specs=[
            pl.BlockSpec((m, k), lambda l: (0, 0)),
            pl.BlockSpec((1, k), lambda l: (0, 0)),
            pl.BlockSpec((None, k, n), lambda l: (l, 0, 0)),
        ],
        out_specs=pl.BlockSpec((None, m, n), lambda l: (l, 0, 0)),
        out_shape=jax.ShapeDtypeStruct((depth, m, n), BF16),
        compiler_params=_params("parallel"),
        name="mem_kv_proj",
    )(mem2, gain.reshape(1, k), w_stack)


def _ffn_up_kernel(xn_ref, wg_ref, wu_ref, o_ref):
    wg = wg_ref[...].astype(BF16)
    wu = wu_ref[...].astype(BF16)
    for r in range(0, xn_ref.shape[0], ROW_SUBTILE):
        xn = xn_ref[r:r + ROW_SUBTILE, :]
        gate = jnp.dot(xn, wg, preferred_element_type=F32)
        up = jnp.dot(xn, wu, preferred_element_type=F32)
        o_ref[r:r + ROW_SUBTILE, :] = (gate * _sigmoid(gate) * up).astype(o_ref.dtype)


def _ffn_up(xn, w_stack, layer, tm, tn):
    m, k = xn.shape
    d_ff = w_stack.shape[2] // 2
    n_blocks = d_ff // tn
    return pl.pallas_call(
        _ffn_up_kernel,
        grid=(m // tm, n_blocks),
        in_specs=[
            pl.BlockSpec((tm, k), lambda i, j: (i, 0)),
            pl.BlockSpec((None, k, tn), lambda i, j: (layer, 0, j)),
            pl.BlockSpec((None, k, tn), lambda i, j: (layer, 0, j + n_blocks)),
        ],
        out_specs=pl.BlockSpec((tm, tn), lambda i, j: (i, j)),
        out_shape=jax.ShapeDtypeStruct((m, d_ff), BF16),
        compiler_params=_params("parallel", "arbitrary"),
        name="ffn_up",
    )(xn, w_stack, w_stack)


def _down_proj_kernel(a_ref, w_ref, r_ref, g_ref, *out_refs):
    h = r_ref[...] + jnp.dot(a_ref[...], w_ref[...], preferred_element_type=F32)
    hn_ref = out_refs[-1]
    hn_ref[...] = _rmsnorm_rows(h, g_ref[...]).astype(hn_ref.dtype)
    if len(out_refs) == 2:
        out_refs[0][...] = h


def _down_proj(a, w_stack, layer, res, gain, final, tm):
    m, k = a.shape
    n = w_stack.shape[2]
    rows = pl.BlockSpec((tm, n), lambda i: (i, 0))
    if final:
        out_specs, out_shape = rows, jax.ShapeDtypeStruct((m, n), F32)
    else:
        out_specs = [rows, rows]
        out_shape = [jax.ShapeDtypeStruct((m, n), F32), jax.ShapeDtypeStruct((m, n), BF16)]
    return pl.pallas_call(
        _down_proj_kernel,
        grid=(m // tm,),
        in_specs=[
            pl.BlockSpec((tm, k), lambda i: (i, 0)),
            pl.BlockSpec((None, k, n), lambda i: (layer, 0, 0), pipeline_mode=RESIDENT),
            rows,
            pl.BlockSpec((1, n), lambda i: (0, 0)),
        ],
        out_specs=out_specs,
        out_shape=out_shape,
        compiler_params=_params("parallel"),
        name="down_proj",
    )(a, w_stack, res, gain.reshape(1, n))


def _out_proj_kernel(y_ref, xq_ref, kv_ref, w_ref, r_ref, g_ref, h_ref, hn_ref):
    wy = y_ref.shape[1]
    d = xq_ref.shape[1] // XA_HEADS
    for r in range(0, y_ref.shape[0], OUT_PROJ_SUBTILE):
        rows = slice(r, r + OUT_PROJ_SUBTILE)
        acc = jnp.dot(y_ref[rows, :], w_ref[:wy, :], preferred_element_type=F32)
        heads = []
        for hd in range(XA_HEADS):
            q = xq_ref[rows, hd * d:(hd + 1) * d]
            k = kv_ref[:, hd * d:(hd + 1) * d]
            v = kv_ref[:, (XA_HEADS + hd) * d:(XA_HEADS + hd + 1) * d]
            s = lax.dot_general(q, k, NT_DIMS, preferred_element_type=F32) * (d ** -0.5)
            p = jnp.exp(s - jnp.max(s, axis=-1, keepdims=True))
            denom = jnp.sum(p, axis=-1, keepdims=True)
            o = jnp.dot(p.astype(BF16), v, preferred_element_type=F32)
            heads.append((o / denom).astype(BF16))
        acc += jnp.dot(jnp.concatenate(heads, axis=1), w_ref[wy:, :], preferred_element_type=F32)
        h = r_ref[rows, :] + acc
        h_ref[rows, :] = h
        hn_ref[rows, :] = _rmsnorm_rows(h, g_ref[...]).astype(hn_ref.dtype)


def _out_proj(y, proj, xq_block, kv_stack, w_stack, layer, res, gain, batch, tm):
    m, wy = y.shape
    _, k, n = w_stack.shape
    wx = k - wy
    n_mem, wkv = kv_stack.shape[1:]
    steps_per_batch = m // batch // tm
    return pl.pallas_call(
        _out_proj_kernel,
        grid=(m // tm,),
        in_specs=[
            pl.BlockSpec((tm, wy), lambda i: (i, 0)),
            pl.BlockSpec((tm, wx), lambda i: (i, xq_block)),
            pl.BlockSpec((None, n_mem, wkv), lambda i: (layer * batch + i // steps_per_batch, 0, 0)),
            pl.BlockSpec((None, k, n), lambda i: (layer, 0, 0), pipeline_mode=RESIDENT),
            pl.BlockSpec((tm, n), lambda i: (i, 0)),
            pl.BlockSpec((1, n), lambda i: (0, 0)),
        ],
        out_specs=[pl.BlockSpec((tm, n), lambda i: (i, 0)), pl.BlockSpec((tm, n), lambda i: (i, 0))],
        out_shape=[jax.ShapeDtypeStruct((m, n), F32), jax.ShapeDtypeStruct((m, n), BF16)],
        compiler_params=_params("parallel"),
        name="xattn_out_proj",
    )(y, proj, kv_stack, w_stack, res, gain.reshape(1, n))


def _rmsnorm_kernel(x_ref, g_ref, o_ref):
    o_ref[...] = _rmsnorm_rows(x_ref[...], g_ref[...]).astype(o_ref.dtype)


def _rmsnorm(x, gain, tm):
    m, k = x.shape
    return pl.pallas_call(
        _rmsnorm_kernel,
        grid=(m // tm,),
        in_specs=[pl.BlockSpec((tm, k), lambda i: (i, 0)), pl.BlockSpec((1, k), lambda i: (0, 0))],
        out_specs=pl.BlockSpec((tm, k), lambda i: (i, 0)),
        out_shape=jax.ShapeDtypeStruct((m, k), BF16),
        compiler_params=_params("parallel"),
        name="input_rmsnorm",
    )(x, gain.reshape(1, k))


def _chunk_masks():
    t = lax.broadcasted_iota(jnp.int32, (CHUNK, CHUNK), 0)
    s = lax.broadcasted_iota(jnp.int32, (CHUNK, CHUNK), 1)
    tril = (s <= t)
    return tril.astype(BF16), tril


def _chunk_cumsum(tril_b, hi, lo):
    dk = hi.shape[1]
    both = jnp.dot(tril_b, jnp.concatenate([hi, lo], axis=1), preferred_element_type=F32)
    return both[:, :dk] + both[:, dk:]


def _chunk_scores(q, k, vb, cum, sub):
    dk = q.shape[1]
    n_sub = CHUNK // sub
    blocks = [slice(sub * r, sub * (r + 1)) for r in range(n_sub)]
    mid = [cum[sub * j + sub // 2 - 1:sub * j + sub // 2, :] for j in range(n_sub)]
    last = cum[CHUNK - 1:CHUNK, :]
    zero = jnp.zeros((sub, dk), F32)

    q_slots, k_slots, k_last = [], [], []
    for j in range(n_sub):
        q_slots.append(jnp.concatenate(
            [zero if r < j else q[blocks[r], :] * jnp.exp(cum[blocks[r], :] - mid[j])
             for r in range(n_sub)], axis=0))
        k_mid = k[blocks[j], :] * jnp.exp(mid[j] - cum[blocks[j], :])
        k_slots.append(jnp.concatenate([k_mid if r == j else zero for r in range(n_sub)], axis=0))
        k_last.append(k_mid * jnp.exp(last - mid[j]))
    s_raw = lax.dot_general(jnp.concatenate(q_slots, axis=1).astype(BF16),
                            jnp.concatenate(k_slots, axis=1).astype(BF16),
                            NT_DIMS, preferred_element_type=F32)
    upd = lax.dot_general(vb, jnp.concatenate(k_last, axis=0).astype(BF16), TN_DIMS,
                          preferred_element_type=F32)
    q_dec = (q_slots[0] * jnp.exp(mid[0])).astype(BF16)
    return s_raw, upd, q_dec, jnp.exp(last), vb


def _chunk_readout(s_raw, tril, vb, q_dec, st):
    intra = jnp.dot(jnp.where(tril, s_raw, 0.0).astype(BF16), vb, preferred_element_type=F32)
    inter = lax.dot_general(q_dec, st.astype(BF16), NT_DIMS, preferred_element_type=F32)
    return intra + inter


def _gated_head_norm(o, gain, gate):
    ms = jnp.mean(o * o, axis=-1, keepdims=True)
    return o * lax.rsqrt(ms + EPS) * gain * gate


def _run_staggered(n_chunks, stages):
    n_groups = n_chunks // STAGGER_GROUP
    for slot in range(n_groups + len(stages) - 1):
        for depth in reversed(range(len(stages))):
            group = slot - depth
            if 0 <= group < n_groups:
                for c in range(group * STAGGER_GROUP, (group + 1) * STAGGER_GROUP):
                    stages[depth](c)


def _mixer_kernel(sub, q_ref, k_ref, v_ref, gate_ref, hi_ref, lo_ref, on_ref, y_ref, st_ref):
    @pl.when(pl.program_id(2) == 0)
    def _():
        st_ref[...] = jnp.zeros_like(st_ref)

    tril_b, tril = _chunk_masks()
    gain = on_ref[...]
    n_chunks = q_ref.shape[0] // CHUNK
    rows = [slice(c * CHUNK, (c + 1) * CHUNK) for c in range(n_chunks)]
    ctx = [{} for _ in range(n_chunks)]
    state = [st_ref[...]]

    def decay(c):
        ctx[c]["cum"] = _chunk_cumsum(tril_b, hi_ref[rows[c], :], lo_ref[rows[c], :])

    def scores(c):
        ctx[c]["parts"] = _chunk_scores(q_ref[rows[c], :].astype(F32), k_ref[rows[c], :].astype(F32),
                                        v_ref[rows[c], :], ctx[c].pop("cum"), sub)

    def readout(c):
        s_raw, upd, q_dec, e_last, vb = ctx[c].pop("parts")
        ctx[c]["o"] = _chunk_readout(s_raw, tril, vb, q_dec, state[0])
        state[0] = state[0] * e_last + upd

    def emit(c):
        gate = gate_ref[rows[c], :].astype(F32)
        y_ref[rows[c], :] = _gated_head_norm(ctx[c].pop("o"), gain, gate).astype(y_ref.dtype)

    _run_staggered(n_chunks, [decay, scores, readout, emit])
    st_ref[...] = state[0]


def _mixer(proj, onorm, cols, heads, dk, dv, sub, batch, t_blk, name):
    m = proj.shape[0]
    n_t = m // batch // t_blk

    def group(key, width):
        block0, rem = divmod(cols[key], width)
        assert rem == 0
        return pl.BlockSpec((t_blk, width), lambda b, h, t: (b * n_t + t, block0 + h))

    return pl.pallas_call(
        functools.partial(_mixer_kernel, sub),
        grid=(batch, heads, n_t),
        in_specs=[group("q", dk), group("k", dk), group("v", dv), group("gate", dv),
                  group("hi", dk), group("lo", dk),
                  pl.BlockSpec((1, dv), lambda b, h, t: (0, 0))],
        out_specs=pl.BlockSpec((t_blk, dv), lambda b, h, t: (b * n_t + t, h)),
        out_shape=jax.ShapeDtypeStruct((m, heads * dv), BF16),
        scratch_shapes=[pltpu.VMEM((dv, dk), F32)],
        compiler_params=_params("parallel", "parallel", "arbitrary"),
        name=name,
    )(proj, proj, proj, proj, proj, proj, onorm.reshape(1, dv))


def _gla_pack_weights(w_in, w_gk, b_gk, mix_width):
    heads, rank, dkp = GLA_HEADS, GLA_GATE_RANK, GLA_DK_PAD
    kw = w_gk.shape[1]
    dk = kw // heads
    q, k, v, g, gl, xq = jnp.split(
        w_in.astype(BF16),
        [kw, 2 * kw, 2 * kw + mix_width, 2 * kw + 2 * mix_width, 2 * kw + 2 * mix_width + rank], axis=1)

    def pad_heads(w):
        w = w.reshape(w.shape[0], heads, dk)
        return jnp.pad(w, ((0, 0), (0, 0), (0, dkp - dk))).reshape(w.shape[0], heads * dkp)

    gl = jnp.pad(gl, ((0, 0), (0, LANE - rank)))
    w_packed = jnp.concatenate([v, g, pad_heads(q), pad_heads(k), xq, gl], axis=1)
    w_gk_pad = jnp.pad(pad_heads(w_gk), ((0, LANE - rank), (0, 0))).astype(BF16)
    b_gk_pad = pad_heads(b_gk.reshape(1, kw)).astype(F32)
    return w_packed, w_gk_pad, b_gk_pad, dk


def kernel(x, mem, norm_mix, norm_ffn, norm_mem, norm_final, hgrn_w_in, hgrn_lb_logits, hgrn_onorm,
           gla_w_in, gla_w_gk, gla_b_gk, gla_onorm, w_mem_kv, w_out, w_gate_up, w_down):
    batch, seq, d_model = x.shape
    n_mem = mem.shape[1]
    depth = norm_mix.shape[0]
    mix_width = hgrn_lb_logits.shape[1]
    xa_width = d_model - mix_width
    m = batch * seq

    h = x.reshape(m, d_model)
    kv = _mem_kv(mem.reshape(batch * n_mem, d_model), norm_mem, w_mem_kv.astype(BF16))
    kv = kv.reshape(depth * batch, n_mem, 2 * xa_width)

    hgrn_w = hgrn_w_in.astype(BF16)
    w_out_b = w_out.astype(BF16)
    w_down_b = w_down.astype(BF16)

    hn = _rmsnorm(h, norm_mix[0], tm=512)
    for layer in range(depth):
        j = layer // 2
        if layer % 2 == 0:
            proj = _hgrn_in_proj(hn, hgrn_w, j, hgrn_lb_logits, tm=512)
            w = mix_width
            cols = dict(q=0, hi=w, lo=2 * w, k=3 * w, v=4 * w, gate=5 * w)
            y = _mixer(proj, hgrn_onorm[j], cols, mix_width // HGRN_HEAD_DIM, HGRN_HEAD_DIM,
                       HGRN_HEAD_DIM, HGRN_SUB, batch, t_blk=2048, name="hgrn2_mixer")
            xq_col = 6 * w
        else:
            w_packed, w_gk_pad, b_gk_pad, dk = _gla_pack_weights(
                gla_w_in[j], gla_w_gk[j], gla_b_gk[j], mix_width)
            proj = _gla_in_proj(hn, w_packed, w_gk_pad, b_gk_pad, mix_width, xa_width, dk, tm=512)
            wk = GLA_HEADS * GLA_DK_PAD
            xq_col = 2 * mix_width + 2 * wk
            cols = dict(v=0, gate=mix_width, q=2 * mix_width, k=2 * mix_width + wk,
                        hi=xq_col + xa_width, lo=xq_col + xa_width + wk)
            y = _mixer(proj, gla_onorm[j], cols, GLA_HEADS, GLA_DK_PAD, mix_width // GLA_HEADS,
                       GLA_SUB, batch, t_blk=2048, name="gla_mixer")
        assert xq_col % xa_width == 0
        h, hn = _out_proj(y, proj, xq_col // xa_width, kv, w_out_b, layer, h, norm_ffn[layer], batch,
                          tm=512)
        act = _ffn_up(hn, w_gate_up, layer, tm=2048, tn=512)
        if layer + 1 < depth:
            h, hn = _down_proj(act, w_down_b, layer, h, norm_mix[layer + 1], final=False, tm=256)
        else:
            out = _down_proj(act, w_down_b, layer, h, norm_final, final=True, tm=256)
    return out.reshape(batch, seq, d_model)
```

```python
import functools

import jax
import jax.numpy as jnp
from jax import lax
from jax.experimental import pallas as pl
from jax.experimental.pallas import tpu as pltpu

F32 = jnp.float32
BF16 = jnp.bfloat16

EPS = 1e-6
CHUNK = 64
STAGGER_GROUP = 8
ROW_SUBTILE = 256
OUT_PROJ_SUBTILE = 512
IN_PROJ_SUBTILE = 256
HGRN_SUB = 32
GLA_SUB = 64
XA_HEADS = 4
HGRN_HEAD_DIM = 128
GLA_HEADS = 4
GLA_GATE_RANK = 16
GLA_GATE_NORMALIZER = 16.0
LANE = 128
GLA_DK_PAD = 256
V7X_VMEM_LIMIT = 56 * 1024 * 1024

NT_DIMS = (((1,), (1,)), ((), ()))
TN_DIMS = (((0,), (0,)), ((), ()))

RESIDENT = pl.Buffered(1)


def _params(*semantics):
    return pltpu.CompilerParams(dimension_semantics=semantics, vmem_limit_bytes=V7X_VMEM_LIMIT)


def _sigmoid(x):
    return 1.0 / (1.0 + jnp.exp(-x))


def _rmsnorm_rows(x, gain):
    ms = jnp.mean(x * x, axis=-1, keepdims=True)
    return x * lax.rsqrt(ms + EPS) * gain


def _split_hi_lo(g):
    hi = g.astype(BF16)
    return hi, (g - hi.astype(F32)).astype(BF16)


def _hgrn_in_proj_kernel(layer_j, xn_ref, w_ref, lbl_ref, o_ref):
    w = lbl_ref.shape[1]

    logits = lbl_ref[...]
    ex = jnp.exp(logits - jnp.max(logits, axis=0, keepdims=True))
    sm = ex / jnp.sum(ex, axis=0, keepdims=True)
    lb = jnp.zeros((1, w), F32)
    for r in range(1, layer_j + 1):
        lb = lb + sm[r:r + 1, :]

    for r in range(0, xn_ref.shape[0], IN_PROJ_SUBTILE):
        rows = slice(r, r + IN_PROJ_SUBTILE)
        xn = xn_ref[rows, :]

        def proj(lo, hi):
            return jnp.dot(xn, w_ref[:, lo:hi], preferred_element_type=F32)

        o_ref[rows, 4 * w:5 * w] = proj(2 * w, 3 * w).astype(BF16)
        f = lb + (1.0 - lb) * _sigmoid(proj(w, 2 * w))
        hi, lo = _split_hi_lo(jnp.log(f))
        o_ref[rows, w:2 * w] = hi
        o_ref[rows, 2 * w:3 * w] = lo
        o_ref[rows, 3 * w:4 * w] = (1.0 - f).astype(BF16)
        q = proj(0, w)
        o_ref[rows, 0:w] = (q * _sigmoid(q)).astype(BF16)
        g = proj(3 * w, 4 * w)
        o_ref[rows, 5 * w:6 * w] = (g * _sigmoid(g)).astype(BF16)
        o_ref[rows, 6 * w:] = proj(4 * w, w_ref.shape[1]).astype(BF16)


def _hgrn_in_proj(xn, w_stack, layer_j, lb_logits, tm):
    m, k = xn.shape
    n_in = w_stack.shape[2]
    n_a, w = lb_logits.shape
    n_out = n_in + 2 * w
    return pl.pallas_call(
        functools.partial(_hgrn_in_proj_kernel, layer_j),
        grid=(m // tm,),
        in_specs=[
            pl.BlockSpec((tm, k), lambda i: (i, 0)),
            pl.BlockSpec((None, k, n_in), lambda i: (layer_j, 0, 0), pipeline_mode=RESIDENT),
            pl.BlockSpec((n_a, w), lambda i: (0, 0)),
        ],
        out_specs=pl.BlockSpec((tm, n_out), lambda i: (i, 0)),
        out_shape=jax.ShapeDtypeStruct((m, n_out), BF16),
        compiler_params=_params("parallel"),
        name="hgrn_in_proj",
    )(xn, w_stack, lb_logits)


def _gla_in_proj_kernel(dk_true, wv, wx, xn_ref, w_ref, wgk_ref, bgk_ref, o_ref):
    dkp = GLA_DK_PAD
    wk = wgk_ref.shape[1]
    wqk = GLA_HEADS * dk_true
    in_v, in_g, in_tail = 2 * wqk, 2 * wqk + wv, 2 * wqk + 2 * wv
    c_q = 2 * wv
    c_x = c_q + 2 * wk
    zero_pad = jnp.zeros((IN_PROJ_SUBTILE, dkp - dk_true), BF16)

    for r in range(0, xn_ref.shape[0], IN_PROJ_SUBTILE):
        rows = slice(r, r + IN_PROJ_SUBTILE)
        xn = xn_ref[rows, :]

        def proj(lo, hi):
            return jnp.dot(xn, w_ref[:, lo:hi], preferred_element_type=F32)

        def log_decay(head):
            cols = slice(head * dkp, (head + 1) * dkp)
            x = jnp.dot(low, wgk_ref[:, cols], preferred_element_type=F32) + bgk_ref[:, cols]
            log_a = (jnp.minimum(x, 0.0) - jnp.log1p(jnp.exp(-jnp.abs(x)))) / GLA_GATE_NORMALIZER
            hi, lo = _split_hi_lo(log_a)
            o_ref[rows, c_x + wx + head * dkp:c_x + wx + (head + 1) * dkp] = hi
            o_ref[rows, c_x + wx + wk + head * dkp:c_x + wx + wk + (head + 1) * dkp] = lo

        def store_heads(col0, values, scale):
            for hd in range(GLA_HEADS):
                piece = values[:, hd * dk_true:(hd + 1) * dk_true]
                o_ref[rows, col0 + hd * dkp:col0 + hd * dkp + dk_true] = (piece * scale).astype(BF16)
                o_ref[rows, col0 + hd * dkp + dk_true:col0 + (hd + 1) * dkp] = zero_pad

        tail = proj(in_tail, w_ref.shape[1])
        low = tail[:, :LANE].astype(BF16)
        o_ref[rows, c_x:c_x + wx] = tail[:, GLA_GATE_RANK:GLA_GATE_RANK + wx].astype(BF16)
        o_ref[rows, 0:wv] = proj(in_v, in_g).astype(BF16)
        log_decay(0)
        g = proj(in_g, in_tail)
        o_ref[rows, wv:c_q] = (g * _sigmoid(g)).astype(BF16)
        log_decay(1)
        qk = proj(0, in_v)
        store_heads(c_q, qk[:, :wqk], dk_true ** -0.5)
        log_decay(2)
        store_heads(c_q + wk, qk[:, wqk:], 1.0)
        log_decay(3)


def _gla_in_proj(xn, w_packed, w_gk_pad, b_gk_pad, mix_width, xa_width, dk_true, tm):
    m, k = xn.shape
    n_in = w_packed.shape[1]
    wk = w_gk_pad.shape[1]
    n_out = 2 * mix_width + 2 * wk + xa_width + 2 * wk
    return pl.pallas_call(
        functools.partial(_gla_in_proj_kernel, dk_true, mix_width, xa_width),
        grid=(m // tm,),
        in_specs=[
            pl.BlockSpec((tm, k), lambda i: (i, 0)),
            pl.BlockSpec((k, n_in), lambda i: (0, 0), pipeline_mode=RESIDENT),
            pl.BlockSpec((LANE, wk), lambda i: (0, 0)),
            pl.BlockSpec((1, wk), lambda i: (0, 0)),
        ],
        out_specs=pl.BlockSpec((tm, n_out), lambda i: (i, 0)),
        out_shape=jax.ShapeDtypeStruct((m, n_out), BF16),
        compiler_params=_params("parallel"),
        name="gla_in_proj",
    )(xn, w_packed, w_gk_pad, b_gk_pad)


def _mem_kv_kernel(x_ref, g_ref, w_ref, o_ref):
    xn = _rmsnorm_rows(x_ref[...], g_ref[...]).astype(BF16)
    o_ref[...] = jnp.dot(xn, w_ref[...], preferred_element_type=F32).astype(o_ref.dtype)


def _mem_kv(mem2, gain, w_stack):
    m, k = mem2.shape
    depth, _, n = w_stack.shape
    return pl.pallas_call(
        _mem_kv_kernel,
        grid=(depth,),
        in_specs=[
            pl.BlockSpec((m, k), lambda l: (0, 0)),
            pl.BlockSpec((1, k), lambda l: (0, 0)),
            pl.BlockSpec((None, k, n), lambda l: (l, 0, 0)),
        ],
        out_specs=pl.BlockSpec((None, m, n), lambda l: (l, 0, 0)),
        out_shape=jax.ShapeDtypeStruct((depth, m, n), BF16),
        compiler_params=_params("parallel"),
        name="mem_kv_proj",
    )(mem2, gain.reshape(1, k), w_stack)


def _ffn_up_kernel(xn_ref, wg_ref, wu_ref, o_ref):
    wg = wg_ref[...].astype(BF16)
    wu = wu_ref[...].astype(BF16)
    for r in range(0, xn_ref.shape[0], ROW_SUBTILE):
        xn = xn_ref[r:r + ROW_SUBTILE, :]
        gate = jnp.dot(xn, wg, preferred_element_type=F32)
        up = jnp.dot(xn, wu, preferred_element_type=F32)
        o_ref[r:r + ROW_SUBTILE, :] = (gate * _sigmoid(gate) * up).astype(o_ref.dtype)


def _ffn_up(xn, w_stack, layer, tm, tn):
    m, k = xn.shape
    d_ff = w_stack.shape[2] // 2
    n_blocks = d_ff // tn
    return pl.pallas_call(
        _ffn_up_kernel,
        grid=(m // tm, n_blocks),
        in_specs=[
            pl.BlockSpec((tm, k), lambda i, j: (i, 0)),
            pl.BlockSpec((None, k, tn), lambda i, j: (layer, 0, j)),
            pl.BlockSpec((None, k, tn), lambda i, j: (layer, 0, j + n_blocks)),
        ],
        out_specs=pl.BlockSpec((tm, tn), lambda i, j: (i, j)),
        out_shape=jax.ShapeDtypeStruct((m, d_ff), BF16),
        compiler_params=_params("parallel", "arbitrary"),
        name="ffn_up",
    )(xn, w_stack, w_stack)


def _down_proj_kernel(a_ref, w_ref, r_ref, g_ref, *out_refs):
    h = r_ref[...] + jnp.dot(a_ref[...], w_ref[...], preferred_element_type=F32)
    hn_ref = out_refs[-1]
    hn_ref[...] = _rmsnorm_rows(h, g_ref[...]).astype(hn_ref.dtype)
    if len(out_refs) == 2:
        out_refs[0][...] = h


def _down_proj(a, w_stack, layer, res, gain, final, tm):
    m, k = a.shape
    n = w_stack.shape[2]
    rows = pl.BlockSpec((tm, n), lambda i: (i, 0))
    if final:
        out_specs, out_shape = rows, jax.ShapeDtypeStruct((m, n), F32)
    else:
        out_specs = [rows, rows]
        out_shape = [jax.ShapeDtypeStruct((m, n), F32), jax.ShapeDtypeStruct((m, n), BF16)]
    return pl.pallas_call(
        _down_proj_kernel,
        grid=(m // tm,),
        in_specs=[
            pl.BlockSpec((tm, k), lambda i: (i, 0)),
            pl.BlockSpec((None, k, n), lambda i: (layer, 0, 0), pipeline_mode=RESIDENT),
            rows,
            pl.BlockSpec((1, n), lambda i: (0, 0)),
        ],
        out_specs=out_specs,
        out_shape=out_shape,
        compiler_params=_params("parallel"),
        name="down_proj",
    )(a, w_stack, res, gain.reshape(1, n))


def _out_proj_kernel(y_ref, xq_ref, kv_ref, w_ref, r_ref, g_ref, h_ref, hn_ref):
    wy = y_ref.shape[1]
    d = xq_ref.shape[1] // XA_HEADS
    for r in range(0, y_ref.shape[0], OUT_PROJ_SUBTILE):
        rows = slice(r, r + OUT_PROJ_SUBTILE)
        acc = jnp.dot(y_ref[rows, :], w_ref[:wy, :], preferred_element_type=F32)
        heads = []
        for hd in range(XA_HEADS):
            q = xq_ref[rows, hd * d:(hd + 1) * d]
            k = kv_ref[:, hd * d:(hd + 1) * d]
            v = kv_ref[:, (XA_HEADS + hd) * d:(XA_HEADS + hd + 1) * d]
            s = lax.dot_general(q, k, NT_DIMS, preferred_element_type=F32) * (d ** -0.5)
            p = jnp.exp(s - jnp.max(s, axis=-1, keepdims=True))
            denom = jnp.sum(p, axis=-1, keepdims=True)
            o = jnp.dot(p.astype(BF16), v, preferred_element_type=F32)
            heads.append((o / denom).astype(BF16))
        acc += jnp.dot(jnp.concatenate(heads, axis=1), w_ref[wy:, :], preferred_element_type=F32)
        h = r_ref[rows, :] + acc
        h_ref[rows, :] = h
        hn_ref[rows, :] = _rmsnorm_rows(h, g_ref[...]).astype(hn_ref.dtype)


def _out_proj(y, proj, xq_block, kv_stack, w_stack, layer, res, gain, batch, tm):
    m, wy = y.shape
    _, k, n = w_stack.shape
    wx = k - wy
    n_mem, wkv = kv_stack.shape[1:]
    steps_per_batch = m // batch // tm
    return pl.pallas_call(
        _out_proj_kernel,
        grid=(m // tm,),
        in_specs=[
            pl.BlockSpec((tm, wy), lambda i: (i, 0)),
            pl.BlockSpec((tm, wx), lambda i: (i, xq_block)),
            pl.BlockSpec((None, n_mem, wkv), lambda i: (layer * batch + i // steps_per_batch, 0, 0)),
            pl.BlockSpec((None, k, n), lambda i: (layer, 0, 0), pipeline_mode=RESIDENT),
            pl.BlockSpec((tm, n), lambda i: (i, 0)),
            pl.BlockSpec((1, n), lambda i: (0, 0)),
        ],
        out_specs=[pl.BlockSpec((tm, n), lambda i: (i, 0)), pl.BlockSpec((tm, n), lambda i: (i, 0))],
        out_shape=[jax.ShapeDtypeStruct((m, n), F32), jax.ShapeDtypeStruct((m, n), BF16)],
        compiler_params=_params("parallel"),
        name="xattn_out_proj",
    )(y, proj, kv_stack, w_stack, res, gain.reshape(1, n))


def _rmsnorm_kernel(x_ref, g_ref, o_ref):
    o_ref[...] = _rmsnorm_rows(x_ref[...], g_ref[...]).astype(o_ref.dtype)


def _rmsnorm(x, gain, tm):
    m, k = x.shape
    return pl.pallas_call(
        _rmsnorm_kernel,
        grid=(m // tm,),
        in_specs=[pl.BlockSpec((tm, k), lambda i: (i, 0)), pl.BlockSpec((1, k), lambda i: (0, 0))],
        out_specs=pl.BlockSpec((tm, k), lambda i: (i, 0)),
        out_shape=jax.ShapeDtypeStruct((m, k), BF16),
        compiler_params=_params("parallel"),
        name="input_rmsnorm",
    )(x, gain.reshape(1, k))


def _chunk_masks():
    t = lax.broadcasted_iota(jnp.int32, (CHUNK, CHUNK), 0)
    s = lax.broadcasted_iota(jnp.int32, (CHUNK, CHUNK), 1)
    tril = (s <= t)
    return tril.astype(BF16), tril


def _chunk_cumsum(tril_b, hi, lo):
    dk = hi.shape[1]
    both = jnp.dot(tril_b, jnp.concatenate([hi, lo], axis=1), preferred_element_type=F32)
    return both[:, :dk] + both[:, dk:]


def _chunk_scores(q, k, vb, cum, sub):
    dk = q.shape[1]
    n_sub = CHUNK // sub
    blocks = [slice(sub * r, sub * (r + 1)) for r in range(n_sub)]
    mid = [cum[sub * j + sub // 2 - 1:sub * j + sub // 2, :] for j in range(n_sub)]
    last = cum[CHUNK - 1:CHUNK, :]
    zero = jnp.zeros((sub, dk), F32)

    q_slots, k_slots, k_last = [], [], []
    for j in range(n_sub):
        q_slots.append(jnp.concatenate(
            [zero if r < j else q[blocks[r], :] * jnp.exp(cum[blocks[r], :] - mid[j])
             for r in range(n_sub)], axis=0))
        k_mid = k[blocks[j], :] * jnp.exp(mid[j] - cum[blocks[j], :])
        k_slots.append(jnp.concatenate([k_mid if r == j else zero for r in range(n_sub)], axis=0))
        k_last.append(k_mid * jnp.exp(last - mid[j]))
    s_raw = lax.dot_general(jnp.concatenate(q_slots, axis=1).astype(BF16),
                            jnp.concatenate(k_slots, axis=1).astype(BF16),
                            NT_DIMS, preferred_element_type=F32)
    upd = lax.dot_general(vb, jnp.concatenate(k_last, axis=0).astype(BF16), TN_DIMS,
                          preferred_element_type=F32)
    q_dec = (q_slots[0] * jnp.exp(mid[0])).astype(BF16)
    return s_raw, upd, q_dec, jnp.exp(last), vb


def _chunk_readout(s_raw, tril, vb, q_dec, st):
    intra = jnp.dot(jnp.where(tril, s_raw, 0.0).astype(BF16), vb, preferred_element_type=F32)
    inter = lax.dot_general(q_dec, st.astype(BF16), NT_DIMS, preferred_element_type=F32)
    return intra + inter


def _gated_head_norm(o, gain, gate):
    ms = jnp.mean(o * o, axis=-1, keepdims=True)
    return o * lax.rsqrt(ms + EPS) * gain * gate


def _run_staggered(n_chunks, stages):
    n_groups = n_chunks // STAGGER_GROUP
    for slot in range(n_groups + len(stages) - 1):
        for depth in reversed(range(len(stages))):
            group = slot - depth
            if 0 <= group < n_groups:
                for c in range(group * STAGGER_GROUP, (group + 1) * STAGGER_GROUP):
                    stages[depth](c)


def _mixer_kernel(sub, q_ref, k_ref, v_ref, gate_ref, hi_ref, lo_ref, on_ref, y_ref, st_ref):
    @pl.when(pl.program_id(2) == 0)
    def _():
        st_ref[...] = jnp.zeros_like(st_ref)

    tril_b, tril = _chunk_masks()
    gain = on_ref[...]
    n_chunks = q_ref.shape[0] // CHUNK
    rows = [slice(c * CHUNK, (c + 1) * CHUNK) for c in range(n_chunks)]
    ctx = [{} for _ in range(n_chunks)]
    state = [st_ref[...]]

    def decay(c):
        ctx[c]["cum"] = _chunk_cumsum(tril_b, hi_ref[rows[c], :], lo_ref[rows[c], :])

    def scores(c):
        ctx[c]["parts"] = _chunk_scores(q_ref[rows[c], :].astype(F32), k_ref[rows[c], :].astype(F32),
                                        v_ref[rows[c], :], ctx[c].pop("cum"), sub)

    def readout(c):
        s_raw, upd, q_dec, e_last, vb = ctx[c].pop("parts")
        ctx[c]["o"] = _chunk_readout(s_raw, tril, vb, q_dec, state[0])
        state[0] = state[0] * e_last + upd

    def emit(c):
        gate = gate_ref[rows[c], :].astype(F32)
        y_ref[rows[c], :] = _gated_head_norm(ctx[c].pop("o"), gain, gate).astype(y_ref.dtype)

    _run_staggered(n_chunks, [decay, scores, readout, emit])
    st_ref[...] = state[0]


def _mixer(proj, onorm, cols, heads, dk, dv, sub, batch, t_blk, name):
    m = proj.shape[0]
    n_t = m // batch // t_blk

    def group(key, width):
        block0, rem = divmod(cols[key], width)
        assert rem == 0
        return pl.BlockSpec((t_blk, width), lambda b, h, t: (b * n_t + t, block0 + h))

    return pl.pallas_call(
        functools.partial(_mixer_kernel, sub),
        grid=(batch, heads, n_t),
        in_specs=[group("q", dk), group("k", dk), group("v", dv), group("gate", dv),
                  group("hi", dk), group("lo", dk),
                  pl.BlockSpec((1, dv), lambda b, h, t: (0, 0))],
        out_specs=pl.BlockSpec((t_blk, dv), lambda b, h, t: (b * n_t + t, h)),
        out_shape=jax.ShapeDtypeStruct((m, heads * dv), BF16),
        scratch_shapes=[pltpu.VMEM((dv, dk), F32)],
        compiler_params=_params("parallel", "parallel", "arbitrary"),
        name=name,
    )(proj, proj, proj, proj, proj, proj, onorm.reshape(1, dv))


def _gla_pack_weights(w_in, w_gk, b_gk):
    heads, rank, dkp = GLA_HEADS, GLA_GATE_RANK, GLA_DK_PAD
    kw = w_gk.shape[1]
    dk = kw // heads

    def pad_heads(w):
        w = w.reshape(w.shape[0], heads, dk)
        return jnp.pad(w, ((0, 0), (0, 0), (0, dkp - dk))).reshape(w.shape[0], heads * dkp)

    w_packed = jnp.pad(w_in.astype(BF16), ((0, 0), (0, -w_in.shape[1] % LANE)))
    w_gk_pad = jnp.pad(pad_heads(w_gk), ((0, LANE - rank), (0, 0))).astype(BF16)
    b_gk_pad = pad_heads(b_gk.reshape(1, kw)).astype(F32)
    return w_packed, w_gk_pad, b_gk_pad, dk


def kernel(x, mem, norm_mix, norm_ffn, norm_mem, norm_final, hgrn_w_in, hgrn_lb_logits, hgrn_onorm,
           gla_w_in, gla_w_gk, gla_b_gk, gla_onorm, w_mem_kv, w_out, w_gate_up, w_down):
    batch, seq, d_model = x.shape
    n_mem = mem.shape[1]
    depth = norm_mix.shape[0]
    mix_width = hgrn_lb_logits.shape[1]
    xa_width = d_model - mix_width
    m = batch * seq

    h = x.reshape(m, d_model)
    kv = _mem_kv(mem.reshape(batch * n_mem, d_model), norm_mem, w_mem_kv.astype(BF16))
    kv = kv.reshape(depth * batch, n_mem, 2 * xa_width)

    hgrn_w = hgrn_w_in.astype(BF16)
    w_out_b = w_out.astype(BF16)
    w_down_b = w_down.astype(BF16)

    hn = _rmsnorm(h, norm_mix[0], tm=512)
    for layer in range(depth):
        j = layer // 2
        if layer % 2 == 0:
            proj = _hgrn_in_proj(hn, hgrn_w, j, hgrn_lb_logits, tm=512)
            w = mix_width
            cols = dict(q=0, hi=w, lo=2 * w, k=3 * w, v=4 * w, gate=5 * w)
            y = _mixer(proj, hgrn_onorm[j], cols, mix_width // HGRN_HEAD_DIM, HGRN_HEAD_DIM,
                       HGRN_HEAD_DIM, HGRN_SUB, batch, t_blk=2048, name="hgrn2_mixer")
            xq_col = 6 * w
        else:
            w_packed, w_gk_pad, b_gk_pad, dk = _gla_pack_weights(
                gla_w_in[j], gla_w_gk[j], gla_b_gk[j])
            proj = _gla_in_proj(hn, w_packed, w_gk_pad, b_gk_pad, mix_width, xa_width, dk, tm=512)
            wk = GLA_HEADS * GLA_DK_PAD
            xq_col = 2 * mix_width + 2 * wk
            cols = dict(v=0, gate=mix_width, q=2 * mix_width, k=2 * mix_width + wk,
                        hi=xq_col + xa_width, lo=xq_col + xa_width + wk)
            y = _mixer(proj, gla_onorm[j], cols, GLA_HEADS, GLA_DK_PAD, mix_width // GLA_HEADS,
                       GLA_SUB, batch, t_blk=2048, name="gla_mixer")
        assert xq_col % xa_width == 0
        h, hn = _out_proj(y, proj, xq_col // xa_width, kv, w_out_b, layer, h, norm_ffn[layer], batch,
                          tm=512)
        act = _ffn_up(hn, w_gate_up, layer, tm=2048, tn=512)
        if layer + 1 < depth:
            h, hn = _down_proj(act, w_down_b, layer, h, norm_mix[layer + 1], final=False, tm=256)
        else:
            out = _down_proj(act, w_down_b, layer, h, norm_final, final=True, tm=256)
    return out.reshape(batch, seq, d_model)
```

```python
import functools

import jax
import jax.numpy as jnp
from jax import lax
from jax.experimental import pallas as pl
from jax.experimental.pallas import tpu as pltpu

F32 = jnp.float32
BF16 = jnp.bfloat16

EPS = 1e-6
CHUNK = 64
STAGGER_GROUP = 8
ROW_SUBTILE = 256
OUT_PROJ_SUBTILE = 512
IN_PROJ_SUBTILE = 256
HGRN_SUB = 32
GLA_SUB = 64
XA_HEADS = 4
HGRN_HEAD_DIM = 128
GLA_HEADS = 4
GLA_GATE_RANK = 16
GLA_GATE_NORMALIZER = 16.0
LANE = 128
GLA_DK_PAD = 256
V7X_VMEM_LIMIT = 56 * 1024 * 1024

NT_DIMS = (((1,), (1,)), ((), ()))
TN_DIMS = (((0,), (0,)), ((), ()))

RESIDENT = pl.Buffered(1)


def _params(*semantics):
    return pltpu.CompilerParams(dimension_semantics=semantics, vmem_limit_bytes=V7X_VMEM_LIMIT)


def _sigmoid(x):
    return 1.0 / (1.0 + jnp.exp(-x))


def _rmsnorm_rows(x, gain):
    ms = jnp.mean(x * x, axis=-1, keepdims=True)
    return x * lax.rsqrt(ms + EPS) * gain


def _split_hi_lo(g):
    hi = g.astype(BF16)
    return hi, (g - hi.astype(F32)).astype(BF16)


def _hgrn_in_proj_kernel(layer_j, xn_ref, w_ref, lbl_ref, o_ref):
    w = lbl_ref.shape[1]

    logits = lbl_ref[...]
    ex = jnp.exp(logits - jnp.max(logits, axis=0, keepdims=True))
    sm = ex / jnp.sum(ex, axis=0, keepdims=True)
    lb = jnp.zeros((1, w), F32)
    for r in range(1, layer_j + 1):
        lb = lb + sm[r:r + 1, :]

    for r in range(0, xn_ref.shape[0], IN_PROJ_SUBTILE):
        rows = slice(r, r + IN_PROJ_SUBTILE)
        xn = xn_ref[rows, :]

        def proj(lo, hi):
            return jnp.dot(xn, w_ref[:, lo:hi], preferred_element_type=F32)

        o_ref[rows, 4 * w:5 * w] = proj(2 * w, 3 * w).astype(BF16)
        f = lb + (1.0 - lb) * _sigmoid(proj(w, 2 * w))
        hi, lo = _split_hi_lo(jnp.log(f))
        o_ref[rows, w:2 * w] = hi
        o_ref[rows, 2 * w:3 * w] = lo
        o_ref[rows, 3 * w:4 * w] = (1.0 - f).astype(BF16)
        q = proj(0, w)
        o_ref[rows, 0:w] = (q * _sigmoid(q)).astype(BF16)
        g = proj(3 * w, 4 * w)
        o_ref[rows, 5 * w:6 * w] = (g * _sigmoid(g)).astype(BF16)
        o_ref[rows, 6 * w:] = proj(4 * w, w_ref.shape[1]).astype(BF16)


def _hgrn_in_proj(xn, w_stack, layer_j, lb_logits, tm):
    m, k = xn.shape
    n_in = w_stack.shape[2]
    n_a, w = lb_logits.shape
    n_out = n_in + 2 * w
    return pl.pallas_call(
        functools.partial(_hgrn_in_proj_kernel, layer_j),
        grid=(m // tm,),
        in_specs=[
            pl.BlockSpec((tm, k), lambda i: (i, 0)),
            pl.BlockSpec((None, k, n_in), lambda i: (layer_j, 0, 0), pipeline_mode=RESIDENT),
            pl.BlockSpec((n_a, w), lambda i: (0, 0)),
        ],
        out_specs=pl.BlockSpec((tm, n_out), lambda i: (i, 0)),
        out_shape=jax.ShapeDtypeStruct((m, n_out), BF16),
        compiler_params=_params("parallel"),
        name="hgrn_in_proj",
    )(xn, w_stack, lb_logits)


def _gla_in_proj_kernel(dk_true, wv, wx, xn_ref, w_ref, wgk_ref, bgk_ref, o_ref):
    dkp = GLA_DK_PAD
    wk = wgk_ref.shape[1]
    wqk = GLA_HEADS * dk_true
    in_v, in_g, in_tail = 2 * wqk, 2 * wqk + wv, 2 * wqk + 2 * wv
    c_q = 2 * wv
    c_x = c_q + 2 * wk
    zero_pad = jnp.zeros((IN_PROJ_SUBTILE, dkp - dk_true), BF16)

    for r in range(0, xn_ref.shape[0], IN_PROJ_SUBTILE):
        rows = slice(r, r + IN_PROJ_SUBTILE)
        xn = xn_ref[rows, :]

        def proj(lo, hi):
            return jnp.dot(xn, w_ref[:, lo:hi], preferred_element_type=F32)

        def log_decay(head):
            cols = slice(head * dkp, (head + 1) * dkp)
            x = jnp.dot(low, wgk_ref[:, cols], preferred_element_type=F32) + bgk_ref[:, cols]
            log_a = (jnp.minimum(x, 0.0) - jnp.log1p(jnp.exp(-jnp.abs(x)))) / GLA_GATE_NORMALIZER
            hi, lo = _split_hi_lo(log_a)
            o_ref[rows, c_x + wx + head * dkp:c_x + wx + (head + 1) * dkp] = hi
            o_ref[rows, c_x + wx + wk + head * dkp:c_x + wx + wk + (head + 1) * dkp] = lo

        def store_heads(col0, values, scale):
            for hd in range(GLA_HEADS):
                piece = values[:, hd * dk_true:(hd + 1) * dk_true]
                o_ref[rows, col0 + hd * dkp:col0 + hd * dkp + dk_true] = (piece * scale).astype(BF16)
                o_ref[rows, col0 + hd * dkp + dk_true:col0 + (hd + 1) * dkp] = zero_pad

        tail = proj(in_tail, w_ref.shape[1])
        low = tail[:, :LANE].astype(BF16)
        o_ref[rows, c_x:c_x + wx] = tail[:, GLA_GATE_RANK:GLA_GATE_RANK + wx].astype(BF16)
        o_ref[rows, 0:wv] = proj(in_v, in_g).astype(BF16)
        log_decay(0)
        g = proj(in_g, in_tail)
        o_ref[rows, wv:c_q] = (g * _sigmoid(g)).astype(BF16)
        log_decay(1)
        qk = proj(0, in_v)
        store_heads(c_q, qk[:, :wqk], dk_true ** -0.5)
        log_decay(2)
        store_heads(c_q + wk, qk[:, wqk:], 1.0)
        log_decay(3)


def _gla_in_proj(xn, w_packed, w_gk_pad, b_gk_pad, mix_width, xa_width, dk_true, tm):
    m, k = xn.shape
    n_in = w_packed.shape[1]
    wk = w_gk_pad.shape[1]
    n_out = 2 * mix_width + 2 * wk + xa_width + 2 * wk
    return pl.pallas_call(
        functools.partial(_gla_in_proj_kernel, dk_true, mix_width, xa_width),
        grid=(m // tm,),
        in_specs=[
            pl.BlockSpec((tm, k), lambda i: (i, 0)),
            pl.BlockSpec((k, n_in), lambda i: (0, 0), pipeline_mode=RESIDENT),
            pl.BlockSpec((LANE, wk), lambda i: (0, 0)),
            pl.BlockSpec((1, wk), lambda i: (0, 0)),
        ],
        out_specs=pl.BlockSpec((tm, n_out), lambda i: (i, 0)),
        out_shape=jax.ShapeDtypeStruct((m, n_out), BF16),
        compiler_params=_params("parallel"),
        name="gla_in_proj",
    )(xn, w_packed, w_gk_pad, b_gk_pad)


def _mem_kv_kernel(x_ref, g_ref, w_ref, o_ref):
    xn = _rmsnorm_rows(x_ref[...], g_ref[...]).astype(BF16)
    o_ref[...] = jnp.dot(xn, w_ref[...], preferred_element_type=F32).astype(o_ref.dtype)


def _mem_kv(mem2, gain, w_stack):
    m, k = mem2.shape
    depth, _, n = w_stack.shape
    return pl.pallas_call(
        _mem_kv_kernel,
        grid=(depth,),
        in_specs=[
            pl.BlockSpec((m, k), lambda l: (0, 0)),
            pl.BlockSpec((1, k), lambda l: (0, 0)),
            pl.BlockSpec((None, k, n), lambda l: (l, 0, 0)),
        ],
        out_specs=pl.BlockSpec((None, m, n), lambda l: (l, 0, 0)),
        out_shape=jax.ShapeDtypeStruct((depth, m, n), BF16),
        compiler_params=_params("parallel"),
        name="mem_kv_proj",
    )(mem2, gain.reshape(1, k), w_stack)


def _ffn_up_kernel(xn_ref, wg_ref, wu_ref, o_ref):
    wg = wg_ref[...].astype(BF16)
    wu = wu_ref[...].astype(BF16)
    for r in range(0, xn_ref.shape[0], ROW_SUBTILE):
        xn = xn_ref[r:r + ROW_SUBTILE, :]
        gate = jnp.dot(xn, wg, preferred_element_type=F32)
        up = jnp.dot(xn, wu, preferred_element_type=F32)
        o_ref[r:r + ROW_SUBTILE, :] = (gate * _sigmoid(gate) * up).astype(o_ref.dtype)


def _ffn_up(xn, w_stack, layer, tm, tn):
    m, k = xn.shape
    d_ff = w_stack.shape[2] // 2
    n_blocks = d_ff // tn
    return pl.pallas_call(
        _ffn_up_kernel,
        grid=(m // tm, n_blocks),
        in_specs=[
            pl.BlockSpec((tm, k), lambda i, j: (i, 0)),
            pl.BlockSpec((None, k, tn), lambda i, j: (layer, 0, j)),
            pl.BlockSpec((None, k, tn), lambda i, j: (layer, 0, j + n_blocks)),
        ],
        out_specs=pl.BlockSpec((tm, tn), lambda i, j: (i, j)),
        out_shape=jax.ShapeDtypeStruct((m, d_ff), BF16),
        compiler_params=_params("parallel", "arbitrary"),
        name="ffn_up",
    )(xn, w_stack, w_stack)


def _down_proj_kernel(a_ref, w_ref, r_ref, g_ref, *out_refs):
    h = r_ref[...] + jnp.dot(a_ref[...], w_ref[...], preferred_element_type=F32)
    hn_ref = out_refs[-1]
    hn_ref[...] = _rmsnorm_rows(h, g_ref[...]).astype(hn_ref.dtype)
    if len(out_refs) == 2:
        out_refs[0][...] = h


def _down_proj(a, w_stack, layer, res, gain, final, tm):
    m, k = a.shape
    n = w_stack.shape[2]
    rows = pl.BlockSpec((tm, n), lambda i: (i, 0))
    if final:
        out_specs, out_shape = rows, jax.ShapeDtypeStruct((m, n), F32)
    else:
        out_specs = [rows, rows]
        out_shape = [jax.ShapeDtypeStruct((m, n), F32), jax.ShapeDtypeStruct((m, n), BF16)]
    return pl.pallas_call(
        _down_proj_kernel,
        grid=(m // tm,),
        in_specs=[
            pl.BlockSpec((tm, k), lambda i: (i, 0)),
            pl.BlockSpec((None, k, n), lambda i: (layer, 0, 0), pipeline_mode=RESIDENT),
            rows,
            pl.BlockSpec((1, n), lambda i: (0, 0)),
        ],
        out_specs=out_specs,
        out_shape=out_shape,
        compiler_params=_params("parallel"),
        name="down_proj",
    )(a, w_stack, res, gain.reshape(1, n))


def _out_proj_kernel(y_ref, xq_ref, kv_ref, w_ref, r_ref, g_ref, h_ref, hn_ref):
    wy = y_ref.shape[1]
    d = xq_ref.shape[1] // XA_HEADS
    for r in range(0, y_ref.shape[0], OUT_PROJ_SUBTILE):
        rows = slice(r, r + OUT_PROJ_SUBTILE)
        acc = jnp.dot(y_ref[rows, :], w_ref[:wy, :], preferred_element_type=F32)
        heads = []
        for hd in range(XA_HEADS):
            q = xq_ref[rows, hd * d:(hd + 1) * d]
            k = kv_ref[:, hd * d:(hd + 1) * d]
            v = kv_ref[:, (XA_HEADS + hd) * d:(XA_HEADS + hd + 1) * d]
            s = lax.dot_general(q, k, NT_DIMS, preferred_element_type=F32) * (d ** -0.5)
            p = jnp.exp(s - jnp.max(s, axis=-1, keepdims=True))
            denom = jnp.sum(p, axis=-1, keepdims=True)
            o = jnp.dot(p.astype(BF16), v, preferred_element_type=F32)
            heads.append((o / denom).astype(BF16))
        acc += jnp.dot(jnp.concatenate(heads, axis=1), w_ref[wy:, :], preferred_element_type=F32)
        h = r_ref[rows, :] + acc
        h_ref[rows, :] = h
        hn_ref[rows, :] = _rmsnorm_rows(h, g_ref[...]).astype(hn_ref.dtype)


def _out_proj(y, proj, xq_block, kv_stack, w_stack, layer, res, gain, batch, tm):
    m, wy = y.shape
    _, k, n = w_stack.shape
    wx = k - wy
    n_mem, wkv = kv_stack.shape[1:]
    steps_per_batch = m // batch // tm
    return pl.pallas_call(
        _out_proj_kernel,
        grid=(m // tm,),
        in_specs=[
            pl.BlockSpec((tm, wy), lambda i: (i, 0)),
            pl.BlockSpec((tm, wx), lambda i: (i, xq_block)),
            pl.BlockSpec((None, n_mem, wkv), lambda i: (layer * batch + i // steps_per_batch, 0, 0)),
            pl.BlockSpec((None, k, n), lambda i: (layer, 0, 0), pipeline_mode=RESIDENT),
            pl.BlockSpec((tm, n), lambda i: (i, 0)),
            pl.BlockSpec((1, n), lambda i: (0, 0)),
        ],
        out_specs=[pl.BlockSpec((tm, n), lambda i: (i, 0)), pl.BlockSpec((tm, n), lambda i: (i, 0))],
        out_shape=[jax.ShapeDtypeStruct((m, n), F32), jax.ShapeDtypeStruct((m, n), BF16)],
        compiler_params=_params("parallel"),
        name="xattn_out_proj",
    )(y, proj, kv_stack, w_stack, res, gain.reshape(1, n))


def _rmsnorm_kernel(x_ref, g_ref, o_ref):
    o_ref[...] = _rmsnorm_rows(x_ref[...], g_ref[...]).astype(o_ref.dtype)


def _rmsnorm(x, gain, tm):
    m, k = x.shape
    return pl.pallas_call(
        _rmsnorm_kernel,
        grid=(m // tm,),
        in_specs=[pl.BlockSpec((tm, k), lambda i: (i, 0)), pl.BlockSpec((1, k), lambda i: (0, 0))],
        out_specs=pl.BlockSpec((tm, k), lambda i: (i, 0)),
        out_shape=jax.ShapeDtypeStruct((m, k), BF16),
        compiler_params=_params("parallel"),
        name="input_rmsnorm",
    )(x, gain.reshape(1, k))


def _chunk_masks():
    t = lax.broadcasted_iota(jnp.int32, (CHUNK, CHUNK), 0)
    s = lax.broadcasted_iota(jnp.int32, (CHUNK, CHUNK), 1)
    tril = (s <= t)
    return tril.astype(BF16), tril


def _chunk_cumsum(tril_b, hi, lo):
    dk = hi.shape[1]
    both = jnp.dot(tril_b, jnp.concatenate([hi, lo], axis=1), preferred_element_type=F32)
    return both[:, :dk] + both[:, dk:]


def _chunk_scores(q, k, vb, cum, sub):
    dk = q.shape[1]
    n_sub = CHUNK // sub
    blocks = [slice(sub * r, sub * (r + 1)) for r in range(n_sub)]
    mid = [cum[sub * j + sub // 2 - 1:sub * j + sub // 2, :] for j in range(n_sub)]
    last = cum[CHUNK - 1:CHUNK, :]
    zero = jnp.zeros((sub, dk), F32)

    q_slots, k_slots, k_last = [], [], []
    for j in range(n_sub):
        q_slots.append(jnp.concatenate(
            [zero if r < j else q[blocks[r], :] * jnp.exp(cum[blocks[r], :] - mid[j])
             for r in range(n_sub)], axis=0))
        k_mid = k[blocks[j], :] * jnp.exp(mid[j] - cum[blocks[j], :])
        k_slots.append(jnp.concatenate([k_mid if r == j else zero for r in range(n_sub)], axis=0))
        k_last.append(k_mid * jnp.exp(last - mid[j]))
    s_raw = lax.dot_general(jnp.concatenate(q_slots, axis=1).astype(BF16),
                            jnp.concatenate(k_slots, axis=1).astype(BF16),
                            NT_DIMS, preferred_element_type=F32)
    upd = lax.dot_general(vb, jnp.concatenate(k_last, axis=0).astype(BF16), TN_DIMS,
                          preferred_element_type=F32)
    q_dec = (q_slots[0] * jnp.exp(mid[0])).astype(BF16)
    return s_raw, upd, q_dec, jnp.exp(last), vb


def _chunk_readout(s_raw, tril, vb, q_dec, st):
    intra = jnp.dot(jnp.where(tril, s_raw, 0.0).astype(BF16), vb, preferred_element_type=F32)
    inter = lax.dot_general(q_dec, st.astype(BF16), NT_DIMS, preferred_element_type=F32)
    return intra + inter


def _gated_head_norm(o, gain, gate):
    ms = jnp.mean(o * o, axis=-1, keepdims=True)
    return o * lax.rsqrt(ms + EPS) * gain * gate


def _run_staggered(n_chunks, stages):
    n_groups = n_chunks // STAGGER_GROUP
    for slot in range(n_groups + len(stages) - 1):
        for depth in reversed(range(len(stages))):
            group = slot - depth
            if 0 <= group < n_groups:
                for c in range(group * STAGGER_GROUP, (group + 1) * STAGGER_GROUP):
                    stages[depth](c)


def _mixer_kernel(sub, q_ref, k_ref, v_ref, gate_ref, hi_ref, lo_ref, on_ref, y_ref, st_ref):
    @pl.when(pl.program_id(2) == 0)
    def _():
        st_ref[...] = jnp.zeros_like(st_ref)

    tril_b, tril = _chunk_masks()
    gain = on_ref[...]
    n_chunks = q_ref.shape[0] // CHUNK
    rows = [slice(c * CHUNK, (c + 1) * CHUNK) for c in range(n_chunks)]
    ctx = [{} for _ in range(n_chunks)]
    state = [st_ref[...]]

    def decay(c):
        ctx[c]["cum"] = _chunk_cumsum(tril_b, hi_ref[rows[c], :], lo_ref[rows[c], :])

    def scores(c):
        ctx[c]["parts"] = _chunk_scores(q_ref[rows[c], :].astype(F32), k_ref[rows[c], :].astype(F32),
                                        v_ref[rows[c], :], ctx[c].pop("cum"), sub)

    def readout(c):
        s_raw, upd, q_dec, e_last, vb = ctx[c].pop("parts")
        ctx[c]["o"] = _chunk_readout(s_raw, tril, vb, q_dec, state[0])
        state[0] = state[0] * e_last + upd

    def emit(c):
        gate = gate_ref[rows[c], :].astype(F32)
        y_ref[rows[c], :] = _gated_head_norm(ctx[c].pop("o"), gain, gate).astype(y_ref.dtype)

    _run_staggered(n_chunks, [decay, scores, readout, emit])
    st_ref[...] = state[0]


def _mixer(proj, onorm, cols, heads, dk, dv, sub, batch, t_blk, name):
    m = proj.shape[0]
    n_t = m // batch // t_blk

    def group(key, width):
        block0, rem = divmod(cols[key], width)
        assert rem == 0
        return pl.BlockSpec((t_blk, width), lambda b, h, t: (b * n_t + t, block0 + h))

    return pl.pallas_call(
        functools.partial(_mixer_kernel, sub),
        grid=(batch, heads, n_t),
        in_specs=[group("q", dk), group("k", dk), group("v", dv), group("gate", dv),
                  group("hi", dk), group("lo", dk),
                  pl.BlockSpec((1, dv), lambda b, h, t: (0, 0))],
        out_specs=pl.BlockSpec((t_blk, dv), lambda b, h, t: (b * n_t + t, h)),
        out_shape=jax.ShapeDtypeStruct((m, heads * dv), BF16),
        scratch_shapes=[pltpu.VMEM((dv, dk), F32)],
        compiler_params=_params("parallel", "parallel", "arbitrary"),
        name=name,
    )(proj, proj, proj, proj, proj, proj, onorm.reshape(1, dv))


def _gla_pack_weights(w_in, w_gk, b_gk):
    heads, rank, dkp = GLA_HEADS, GLA_GATE_RANK, GLA_DK_PAD
    kw = w_gk.shape[1]
    dk = kw // heads

    def pad_heads(w):
        w = w.reshape(w.shape[0], heads, dk)
        return jnp.pad(w, ((0, 0), (0, 0), (0, dkp - dk))).reshape(w.shape[0], heads * dkp)

    w_packed = w_in.astype(BF16)
    w_gk_pad = jnp.pad(pad_heads(w_gk), ((0, LANE - rank), (0, 0))).astype(BF16)
    b_gk_pad = pad_heads(b_gk.reshape(1, kw)).astype(F32)
    return w_packed, w_gk_pad, b_gk_pad, dk


def kernel(x, mem, norm_mix, norm_ffn, norm_mem, norm_final, hgrn_w_in, hgrn_lb_logits, hgrn_onorm,
           gla_w_in, gla_w_gk, gla_b_gk, gla_onorm, w_mem_kv, w_out, w_gate_up, w_down):
    batch, seq, d_model = x.shape
    n_mem = mem.shape[1]
    depth = norm_mix.shape[0]
    mix_width = hgrn_lb_logits.shape[1]
    xa_width = d_model - mix_width
    m = batch * seq

    h = x.reshape(m, d_model)
    kv = _mem_kv(mem.reshape(batch * n_mem, d_model), norm_mem, w_mem_kv.astype(BF16))
    kv = kv.reshape(depth * batch, n_mem, 2 * xa_width)

    hgrn_w = hgrn_w_in.astype(BF16)
    w_out_b = w_out.astype(BF16)
    w_down_b = w_down.astype(BF16)

    hn = _rmsnorm(h, norm_mix[0], tm=512)
    for layer in range(depth):
        j = layer // 2
        if layer % 2 == 0:
            proj = _hgrn_in_proj(hn, hgrn_w, j, hgrn_lb_logits, tm=512)
            w = mix_width
            cols = dict(q=0, hi=w, lo=2 * w, k=3 * w, v=4 * w, gate=5 * w)
            y = _mixer(proj, hgrn_onorm[j], cols, mix_width // HGRN_HEAD_DIM, HGRN_HEAD_DIM,
                       HGRN_HEAD_DIM, HGRN_SUB, batch, t_blk=2048, name="hgrn2_mixer")
            xq_col = 6 * w
        else:
            w_packed, w_gk_pad, b_gk_pad, dk = _gla_pack_weights(
                gla_w_in[j], gla_w_gk[j], gla_b_gk[j])
            proj = _gla_in_proj(hn, w_packed, w_gk_pad, b_gk_pad, mix_width, xa_width, dk, tm=512)
            wk = GLA_HEADS * GLA_DK_PAD
            xq_col = 2 * mix_width + 2 * wk
            cols = dict(v=0, gate=mix_width, q=2 * mix_width, k=2 * mix_width + wk,
                        hi=xq_col + xa_width, lo=xq_col + xa_width + wk)
            y = _mixer(proj, gla_onorm[j], cols, GLA_HEADS, GLA_DK_PAD, mix_width // GLA_HEADS,
                       GLA_SUB, batch, t_blk=2048, name="gla_mixer")
        assert xq_col % xa_width == 0
        h, hn = _out_proj(y, proj, xq_col // xa_width, kv, w_out_b, layer, h, norm_ffn[layer], batch,
                          tm=512)
        act = _ffn_up(hn, w_gate_up, layer, tm=2048, tn=512)
        if layer + 1 < depth:
            h, hn = _down_proj(act, w_down_b, layer, h, norm_mix[layer + 1], final=False, tm=256)
        else:
            out = _down_proj(act, w_down_b, layer, h, norm_final, final=True, tm=256)
    return out.reshape(batch, seq, d_model)
```

```python
import functools

import jax
import jax.numpy as jnp
from jax import lax
from jax.experimental import pallas as pl
from jax.experimental.pallas import tpu as pltpu

F32 = jnp.float32
BF16 = jnp.bfloat16

EPS = 1e-6
CHUNK = 64
STAGGER_GROUP = 8
ROW_SUBTILE = 256
IN_PROJ_SUBTILE = 256
HGRN_SUB = 32
GLA_SUB = 64
XA_HEADS = 4
HGRN_HEAD_DIM = 128
GLA_HEADS = 4
GLA_GATE_RANK = 16
GLA_GATE_NORMALIZER = 16.0
LANE = 128
GLA_DK_PAD = 256
V7X_VMEM_LIMIT = 56 * 1024 * 1024

NT_DIMS = (((1,), (1,)), ((), ()))
TN_DIMS = (((0,), (0,)), ((), ()))

RESIDENT = pl.Buffered(1)


def _params(*semantics):
    return pltpu.CompilerParams(dimension_semantics=semantics, vmem_limit_bytes=V7X_VMEM_LIMIT)


def _sigmoid(x):
    return 1.0 / (1.0 + jnp.exp(-x))


def _rmsnorm_rows(x, gain):
    ms = jnp.mean(x * x, axis=-1, keepdims=True)
    return x * lax.rsqrt(ms + EPS) * gain


def _split_hi_lo(g):
    hi = g.astype(BF16)
    return hi, (g - hi.astype(F32)).astype(BF16)


def _hgrn_in_proj_kernel(layer_j, xn_ref, w_ref, lbl_ref, o_ref):
    w = lbl_ref.shape[1]

    logits = lbl_ref[...]
    ex = jnp.exp(logits - jnp.max(logits, axis=0, keepdims=True))
    sm = ex / jnp.sum(ex, axis=0, keepdims=True)
    lb = jnp.zeros((1, w), F32)
    for r in range(1, layer_j + 1):
        lb = lb + sm[r:r + 1, :]

    for r in range(0, xn_ref.shape[0], IN_PROJ_SUBTILE):
        rows = slice(r, r + IN_PROJ_SUBTILE)
        xn = xn_ref[rows, :]

        def proj(lo, hi):
            return jnp.dot(xn, w_ref[:, lo:hi], preferred_element_type=F32)

        o_ref[rows, 4 * w:5 * w] = proj(2 * w, 3 * w).astype(BF16)
        f = lb + (1.0 - lb) * _sigmoid(proj(w, 2 * w))
        hi, lo = _split_hi_lo(jnp.log(f))
        o_ref[rows, w:2 * w] = hi
        o_ref[rows, 2 * w:3 * w] = lo
        o_ref[rows, 3 * w:4 * w] = (1.0 - f).astype(BF16)
        q = proj(0, w)
        o_ref[rows, 0:w] = (q * _sigmoid(q)).astype(BF16)
        g = proj(3 * w, 4 * w)
        o_ref[rows, 5 * w:6 * w] = (g * _sigmoid(g)).astype(BF16)
        o_ref[rows, 6 * w:] = proj(4 * w, w_ref.shape[1]).astype(BF16)


def _hgrn_in_proj(xn, w_stack, layer_j, lb_logits, tm):
    m, k = xn.shape
    n_in = w_stack.shape[2]
    n_a, w = lb_logits.shape
    n_out = n_in + 2 * w
    return pl.pallas_call(
        functools.partial(_hgrn_in_proj_kernel, layer_j),
        grid=(m // tm,),
        in_specs=[
            pl.BlockSpec((tm, k), lambda i: (i, 0)),
            pl.BlockSpec((None, k, n_in), lambda i: (layer_j, 0, 0), pipeline_mode=RESIDENT),
            pl.BlockSpec((n_a, w), lambda i: (0, 0)),
        ],
        out_specs=pl.BlockSpec((tm, n_out), lambda i: (i, 0)),
        out_shape=jax.ShapeDtypeStruct((m, n_out), BF16),
        compiler_params=_params("parallel"),
        name="hgrn_in_proj",
    )(xn, w_stack, lb_logits)


def _gla_in_proj_kernel(dk_true, wv, wx, xn_ref, w_ref, wgk_ref, bgk_ref, o_ref):
    dkp = GLA_DK_PAD
    wk = wgk_ref.shape[1]
    wqk = GLA_HEADS * dk_true
    in_v, in_g, in_tail = 2 * wqk, 2 * wqk + wv, 2 * wqk + 2 * wv
    c_q = 2 * wv
    c_x = c_q + 2 * wk
    zero_pad = jnp.zeros((IN_PROJ_SUBTILE, dkp - dk_true), BF16)

    for r in range(0, xn_ref.shape[0], IN_PROJ_SUBTILE):
        rows = slice(r, r + IN_PROJ_SUBTILE)
        xn = xn_ref[rows, :]

        def proj(lo, hi):
            return jnp.dot(xn, w_ref[:, lo:hi], preferred_element_type=F32)

        def log_decay(head):
            cols = slice(head * dkp, (head + 1) * dkp)
            x = jnp.dot(low, wgk_ref[:, cols], preferred_element_type=F32) + bgk_ref[:, cols]
            log_a = (jnp.minimum(x, 0.0) - jnp.log1p(jnp.exp(-jnp.abs(x)))) / GLA_GATE_NORMALIZER
            hi, lo = _split_hi_lo(log_a)
            o_ref[rows, c_x + wx + head * dkp:c_x + wx + (head + 1) * dkp] = hi
            o_ref[rows, c_x + wx + wk + head * dkp:c_x + wx + wk + (head + 1) * dkp] = lo

        def store_heads(col0, values, scale):
            for hd in range(GLA_HEADS):
                piece = values[:, hd * dk_true:(hd + 1) * dk_true]
                o_ref[rows, col0 + hd * dkp:col0 + hd * dkp + dk_true] = (piece * scale).astype(BF16)
                o_ref[rows, col0 + hd * dkp + dk_true:col0 + (hd + 1) * dkp] = zero_pad

        tail = proj(in_tail, w_ref.shape[1])
        low = tail[:, :LANE].astype(BF16)
        o_ref[rows, c_x:c_x + wx] = tail[:, GLA_GATE_RANK:GLA_GATE_RANK + wx].astype(BF16)
        o_ref[rows, 0:wv] = proj(in_v, in_g).astype(BF16)
        log_decay(0)
        g = proj(in_g, in_tail)
        o_ref[rows, wv:c_q] = (g * _sigmoid(g)).astype(BF16)
        log_decay(1)
        qk = proj(0, in_v)
        store_heads(c_q, qk[:, :wqk], dk_true ** -0.5)
        log_decay(2)
        store_heads(c_q + wk, qk[:, wqk:], 1.0)
        log_decay(3)


def _gla_in_proj(xn, w_stack, layer_j, w_gk_pad, b_gk_pad, mix_width, xa_width, dk_true, tm):
    m, k = xn.shape
    n_in = w_stack.shape[2]
    wk = w_gk_pad.shape[1]
    n_out = 2 * mix_width + 2 * wk + xa_width + 2 * wk
    return pl.pallas_call(
        functools.partial(_gla_in_proj_kernel, dk_true, mix_width, xa_width),
        grid=(m // tm,),
        in_specs=[
            pl.BlockSpec((tm, k), lambda i: (i, 0)),
            pl.BlockSpec((None, k, n_in), lambda i: (layer_j, 0, 0), pipeline_mode=RESIDENT),
            pl.BlockSpec((LANE, wk), lambda i: (0, 0)),
            pl.BlockSpec((1, wk), lambda i: (0, 0)),
        ],
        out_specs=pl.BlockSpec((tm, n_out), lambda i: (i, 0)),
        out_shape=jax.ShapeDtypeStruct((m, n_out), BF16),
        compiler_params=_params("parallel"),
        name="gla_in_proj",
    )(xn, w_stack, w_gk_pad, b_gk_pad)


def _mem_kv_kernel(x_ref, g_ref, w_ref, o_ref):
    xn = _rmsnorm_rows(x_ref[...], g_ref[...]).astype(BF16)
    o_ref[...] = jnp.dot(xn, w_ref[...].astype(BF16), preferred_element_type=F32).astype(o_ref.dtype)


def _mem_kv(mem2, gain, w_stack):
    m, k = mem2.shape
    depth, _, n = w_stack.shape
    return pl.pallas_call(
        _mem_kv_kernel,
        grid=(depth,),
        in_specs=[
            pl.BlockSpec((m, k), lambda l: (0, 0)),
            pl.BlockSpec((1, k), lambda l: (0, 0)),
            pl.BlockSpec((None, k, n), lambda l: (l, 0, 0)),
        ],
        out_specs=pl.BlockSpec((None, m, n), lambda l: (l, 0, 0)),
        out_shape=jax.ShapeDtypeStruct((depth, m, n), BF16),
        compiler_params=_params("parallel"),
        name="mem_kv_proj",
    )(mem2, gain.reshape(1, k), w_stack)


def _ffn_up_kernel(xn_ref, wg_ref, wu_ref, o_ref):
    wg = wg_ref[...].astype(BF16)
    wu = wu_ref[...].astype(BF16)
    for r in range(0, xn_ref.shape[0], ROW_SUBTILE):
        xn = xn_ref[r:r + ROW_SUBTILE, :]
        gate = jnp.dot(xn, wg, preferred_element_type=F32)
        up = jnp.dot(xn, wu, preferred_element_type=F32)
        o_ref[r:r + ROW_SUBTILE, :] = (gate * _sigmoid(gate) * up).astype(o_ref.dtype)


def _ffn_up(xn, w_stack, layer, tm, tn):
    m, k = xn.shape
    d_ff = w_stack.shape[2] // 2
    n_blocks = d_ff // tn
    return pl.pallas_call(
        _ffn_up_kernel,
        grid=(m // tm, n_blocks),
        in_specs=[
            pl.BlockSpec((tm, k), lambda i, j: (i, 0)),
            pl.BlockSpec((None, k, tn), lambda i, j: (layer, 0, j)),
            pl.BlockSpec((None, k, tn), lambda i, j: (layer, 0, j + n_blocks)),
        ],
        out_specs=pl.BlockSpec((tm, tn), lambda i, j: (i, j)),
        out_shape=jax.ShapeDtypeStruct((m, d_ff), BF16),
        compiler_params=_params("parallel", "arbitrary"),
        name="ffn_up",
    )(xn, w_stack, w_stack)


def _down_proj_kernel(a_ref, w_ref, r_ref, g_ref, *out_refs):
    h = r_ref[...] + jnp.dot(a_ref[...], w_ref[...], preferred_element_type=F32)
    hn_ref = out_refs[-1]
    hn_ref[...] = _rmsnorm_rows(h, g_ref[...]).astype(hn_ref.dtype)
    if len(out_refs) == 2:
        out_refs[0][...] = h


def _down_proj(a, w_stack, layer, res, gain, final, tm):
    m, k = a.shape
    n = w_stack.shape[2]
    rows = pl.BlockSpec((tm, n), lambda i: (i, 0))
    if final:
        out_specs, out_shape = rows, jax.ShapeDtypeStruct((m, n), F32)
    else:
        out_specs = [rows, rows]
        out_shape = [jax.ShapeDtypeStruct((m, n), F32), jax.ShapeDtypeStruct((m, n), BF16)]
    return pl.pallas_call(
        _down_proj_kernel,
        grid=(m // tm,),
        in_specs=[
            pl.BlockSpec((tm, k), lambda i: (i, 0)),
            pl.BlockSpec((None, k, n), lambda i: (layer, 0, 0), pipeline_mode=RESIDENT),
            rows,
            pl.BlockSpec((1, n), lambda i: (0, 0)),
        ],
        out_specs=out_specs,
        out_shape=out_shape,
        compiler_params=_params("parallel"),
        name="down_proj",
    )(a, w_stack, res, gain.reshape(1, n))


def _out_proj_kernel(y_ref, xq_ref, kv_ref, w_ref, r_ref, g_ref, h_ref, hn_ref, wb_ref):
    @pl.when(pl.program_id(0) == 0)
    def _():
        wb_ref[...] = w_ref[...].astype(BF16)

    wy = y_ref.shape[1]
    d = xq_ref.shape[1] // XA_HEADS
    acc = jnp.dot(y_ref[...], wb_ref[:wy, :], preferred_element_type=F32)
    heads = []
    for hd in range(XA_HEADS):
        q = xq_ref[:, hd * d:(hd + 1) * d]
        k = kv_ref[:, hd * d:(hd + 1) * d]
        v = kv_ref[:, (XA_HEADS + hd) * d:(XA_HEADS + hd + 1) * d]
        s = lax.dot_general(q, k, NT_DIMS, preferred_element_type=F32) * (d ** -0.5)
        p = jnp.exp(s - jnp.max(s, axis=-1, keepdims=True))
        denom = jnp.sum(p, axis=-1, keepdims=True)
        o = jnp.dot(p.astype(BF16), v, preferred_element_type=F32)
        heads.append((o / denom).astype(BF16))
    acc += jnp.dot(jnp.concatenate(heads, axis=1), wb_ref[wy:, :], preferred_element_type=F32)
    h = r_ref[...] + acc
    h_ref[...] = h
    hn_ref[...] = _rmsnorm_rows(h, g_ref[...]).astype(hn_ref.dtype)


def _out_proj(y, proj, xq_block, kv_stack, w_stack, layer, res, gain, batch, tm):
    m, wy = y.shape
    _, k, n = w_stack.shape
    wx = k - wy
    n_mem, wkv = kv_stack.shape[1:]
    steps_per_batch = m // batch // tm
    return pl.pallas_call(
        _out_proj_kernel,
        grid=(m // tm,),
        in_specs=[
            pl.BlockSpec((tm, wy), lambda i: (i, 0)),
            pl.BlockSpec((tm, wx), lambda i: (i, xq_block)),
            pl.BlockSpec((None, n_mem, wkv), lambda i: (layer * batch + i // steps_per_batch, 0, 0)),
            pl.BlockSpec((None, k, n), lambda i: (layer, 0, 0), pipeline_mode=RESIDENT),
            pl.BlockSpec((tm, n), lambda i: (i, 0)),
            pl.BlockSpec((1, n), lambda i: (0, 0)),
        ],
        out_specs=[pl.BlockSpec((tm, n), lambda i: (i, 0)), pl.BlockSpec((tm, n), lambda i: (i, 0))],
        out_shape=[jax.ShapeDtypeStruct((m, n), F32), jax.ShapeDtypeStruct((m, n), BF16)],
        scratch_shapes=[pltpu.VMEM((k, n), BF16)],
        compiler_params=_params("arbitrary"),
        name="xattn_out_proj",
    )(y, proj, kv_stack, w_stack, res, gain.reshape(1, n))


def _rmsnorm_kernel(x_ref, g_ref, o_ref):
    o_ref[...] = _rmsnorm_rows(x_ref[...], g_ref[...]).astype(o_ref.dtype)


def _rmsnorm(x, gain, tm):
    m, k = x.shape
    return pl.pallas_call(
        _rmsnorm_kernel,
        grid=(m // tm,),
        in_specs=[pl.BlockSpec((tm, k), lambda i: (i, 0)), pl.BlockSpec((1, k), lambda i: (0, 0))],
        out_specs=pl.BlockSpec((tm, k), lambda i: (i, 0)),
        out_shape=jax.ShapeDtypeStruct((m, k), BF16),
        compiler_params=_params("parallel"),
        name="input_rmsnorm",
    )(x, gain.reshape(1, k))


def _chunk_masks():
    t = lax.broadcasted_iota(jnp.int32, (CHUNK, CHUNK), 0)
    s = lax.broadcasted_iota(jnp.int32, (CHUNK, CHUNK), 1)
    tril = (s <= t)
    return tril.astype(BF16), tril


def _chunk_cumsum(tril_b, hi, lo):
    dk = hi.shape[1]
    both = jnp.dot(tril_b, jnp.concatenate([hi, lo], axis=1), preferred_element_type=F32)
    return both[:, :dk] + both[:, dk:]


def _chunk_scores(q, k, vb, cum, sub):
    dk = q.shape[1]
    n_sub = CHUNK // sub
    blocks = [slice(sub * r, sub * (r + 1)) for r in range(n_sub)]
    mid = [cum[sub * j + sub // 2 - 1:sub * j + sub // 2, :] for j in range(n_sub)]
    last = cum[CHUNK - 1:CHUNK, :]
    zero = jnp.zeros((sub, dk), F32)

    q_slots, k_slots, k_last = [], [], []
    for j in range(n_sub):
        q_slots.append(jnp.concatenate(
            [zero if r < j else q[blocks[r], :] * jnp.exp(cum[blocks[r], :] - mid[j])
             for r in range(n_sub)], axis=0))
        k_mid = k[blocks[j], :] * jnp.exp(mid[j] - cum[blocks[j], :])
        k_slots.append(jnp.concatenate([k_mid if r == j else zero for r in range(n_sub)], axis=0))
        k_last.append(k_mid * jnp.exp(last - mid[j]))
    s_raw = lax.dot_general(jnp.concatenate(q_slots, axis=1).astype(BF16),
                            jnp.concatenate(k_slots, axis=1).astype(BF16),
                            NT_DIMS, preferred_element_type=F32)
    upd = lax.dot_general(vb, jnp.concatenate(k_last, axis=0).astype(BF16), TN_DIMS,
                          preferred_element_type=F32)
    q_dec = (q_slots[0] * jnp.exp(mid[0])).astype(BF16)
    return s_raw, upd, q_dec, jnp.exp(last), vb


def _chunk_readout(s_raw, tril, vb, q_dec, st):
    intra = jnp.dot(jnp.where(tril, s_raw, 0.0).astype(BF16), vb, preferred_element_type=F32)
    inter = lax.dot_general(q_dec, st.astype(BF16), NT_DIMS, preferred_element_type=F32)
    return intra + inter


def _gated_head_norm(o, gain, gate):
    ms = jnp.mean(o * o, axis=-1, keepdims=True)
    return o * lax.rsqrt(ms + EPS) * gain * gate


def _run_staggered(n_chunks, stages):
    n_groups = n_chunks // STAGGER_GROUP
    for slot in range(n_groups + len(stages) - 1):
        for depth in reversed(range(len(stages))):
            group = slot - depth
            if 0 <= group < n_groups:
                for c in range(group * STAGGER_GROUP, (group + 1) * STAGGER_GROUP):
                    stages[depth](c)


def _mixer_kernel(sub, q_ref, k_ref, v_ref, gate_ref, hi_ref, lo_ref, on_ref, y_ref, st_ref):
    @pl.when(pl.program_id(2) == 0)
    def _():
        st_ref[...] = jnp.zeros_like(st_ref)

    tril_b, tril = _chunk_masks()
    gain = on_ref[...]
    n_chunks = q_ref.shape[0] // CHUNK
    rows = [slice(c * CHUNK, (c + 1) * CHUNK) for c in range(n_chunks)]
    ctx = [{} for _ in range(n_chunks)]
    state = [st_ref[...]]

    def decay(c):
        ctx[c]["cum"] = _chunk_cumsum(tril_b, hi_ref[rows[c], :], lo_ref[rows[c], :])

    def scores(c):
        ctx[c]["parts"] = _chunk_scores(q_ref[rows[c], :].astype(F32), k_ref[rows[c], :].astype(F32),
                                        v_ref[rows[c], :], ctx[c].pop("cum"), sub)

    def readout(c):
        s_raw, upd, q_dec, e_last, vb = ctx[c].pop("parts")
        ctx[c]["o"] = _chunk_readout(s_raw, tril, vb, q_dec, state[0])
        state[0] = state[0] * e_last + upd

    def emit(c):
        gate = gate_ref[rows[c], :].astype(F32)
        y_ref[rows[c], :] = _gated_head_norm(ctx[c].pop("o"), gain, gate).astype(y_ref.dtype)

    _run_staggered(n_chunks, [decay, scores, readout, emit])
    st_ref[...] = state[0]


def _mixer(proj, onorm, cols, heads, dk, dv, sub, batch, t_blk, name):
    m = proj.shape[0]
    n_t = m // batch // t_blk

    def group(key, width):
        block0, rem = divmod(cols[key], width)
        assert rem == 0
        return pl.BlockSpec((t_blk, width), lambda b, h, t: (b * n_t + t, block0 + h))

    return pl.pallas_call(
        functools.partial(_mixer_kernel, sub),
        grid=(batch, heads, n_t),
        in_specs=[group("q", dk), group("k", dk), group("v", dv), group("gate", dv),
                  group("hi", dk), group("lo", dk),
                  pl.BlockSpec((1, dv), lambda b, h, t: (0, 0))],
        out_specs=pl.BlockSpec((t_blk, dv), lambda b, h, t: (b * n_t + t, h)),
        out_shape=jax.ShapeDtypeStruct((m, heads * dv), BF16),
        scratch_shapes=[pltpu.VMEM((dv, dk), F32)],
        compiler_params=_params("parallel", "parallel", "arbitrary"),
        name=name,
    )(proj, proj, proj, proj, proj, proj, onorm.reshape(1, dv))


def _gla_pad_decay_weights(w_gk, b_gk):
    heads, rank, dkp = GLA_HEADS, GLA_GATE_RANK, GLA_DK_PAD
    kw = w_gk.shape[1]
    dk = kw // heads

    def pad_heads(w):
        w = w.reshape(w.shape[0], heads, dk)
        return jnp.pad(w, ((0, 0), (0, 0), (0, dkp - dk))).reshape(w.shape[0], heads * dkp)

    w_gk_pad = jnp.pad(pad_heads(w_gk), ((0, LANE - rank), (0, 0))).astype(BF16)
    b_gk_pad = pad_heads(b_gk.reshape(1, kw)).astype(F32)
    return w_gk_pad, b_gk_pad, dk


def kernel(x, mem, norm_mix, norm_ffn, norm_mem, norm_final, hgrn_w_in, hgrn_lb_logits, hgrn_onorm,
           gla_w_in, gla_w_gk, gla_b_gk, gla_onorm, w_mem_kv, w_out, w_gate_up, w_down):
    batch, seq, d_model = x.shape
    n_mem = mem.shape[1]
    depth = norm_mix.shape[0]
    mix_width = hgrn_lb_logits.shape[1]
    xa_width = d_model - mix_width
    m = batch * seq

    h = x.reshape(m, d_model)
    kv = _mem_kv(mem.reshape(batch * n_mem, d_model), norm_mem, w_mem_kv)
    kv = kv.reshape(depth * batch, n_mem, 2 * xa_width)

    hgrn_w = hgrn_w_in.astype(BF16)
    gla_w = gla_w_in.astype(BF16)
    w_down_b = w_down.astype(BF16)

    hn = _rmsnorm(h, norm_mix[0], tm=512)
    for layer in range(depth):
        j = layer // 2
        if layer % 2 == 0:
            proj = _hgrn_in_proj(hn, hgrn_w, j, hgrn_lb_logits, tm=512)
            w = mix_width
            cols = dict(q=0, hi=w, lo=2 * w, k=3 * w, v=4 * w, gate=5 * w)
            y = _mixer(proj, hgrn_onorm[j], cols, mix_width // HGRN_HEAD_DIM, HGRN_HEAD_DIM,
                       HGRN_HEAD_DIM, HGRN_SUB, batch, t_blk=2048, name="hgrn2_mixer")
            xq_col = 6 * w
        else:
            w_gk_pad, b_gk_pad, dk = _gla_pad_decay_weights(gla_w_gk[j], gla_b_gk[j])
            proj = _gla_in_proj(hn, gla_w, j, w_gk_pad, b_gk_pad, mix_width, xa_width, dk, tm=512)
            wk = GLA_HEADS * GLA_DK_PAD
            xq_col = 2 * mix_width + 2 * wk
            cols = dict(v=0, gate=mix_width, q=2 * mix_width, k=2 * mix_width + wk,
                        hi=xq_col + xa_width, lo=xq_col + xa_width + wk)
            y = _mixer(proj, gla_onorm[j], cols, GLA_HEADS, GLA_DK_PAD, mix_width // GLA_HEADS,
                       GLA_SUB, batch, t_blk=2048, name="gla_mixer")
        assert xq_col % xa_width == 0
        h, hn = _out_proj(y, proj, xq_col // xa_width, kv, w_out, layer, h, norm_ffn[layer], batch,
                          tm=512)
        act = _ffn_up(hn, w_gate_up, layer, tm=2048, tn=512)
        if layer + 1 < depth:
            h, hn = _down_proj(act, w_down_b, layer, h, norm_mix[layer + 1], final=False, tm=256)
        else:
            out = _down_proj(act, w_down_b, layer, h, norm_final, final=True, tm=256)
    return out.reshape(batch, seq, d_model)
```

```python
import functools

import jax
import jax.numpy as jnp
from jax import lax
from jax.experimental import pallas as pl
from jax.experimental.pallas import tpu as pltpu

F32 = jnp.float32
BF16 = jnp.bfloat16

EPS = 1e-6
CHUNK = 64
STAGGER_GROUP = 8
ROW_SUBTILE = 256
IN_PROJ_SUBTILE = 256
HGRN_SUB = 32
GLA_SUB = 64
XA_HEADS = 4
HGRN_HEAD_DIM = 128
GLA_HEADS = 4
GLA_GATE_RANK = 16
GLA_GATE_NORMALIZER = 16.0
LANE = 128
GLA_DK_PAD = 256
V7X_VMEM_LIMIT = 56 * 1024 * 1024

NT_DIMS = (((1,), (1,)), ((), ()))
TN_DIMS = (((0,), (0,)), ((), ()))

RESIDENT = pl.Buffered(1)


def _params(*semantics):
    return pltpu.CompilerParams(dimension_semantics=semantics, vmem_limit_bytes=V7X_VMEM_LIMIT)


def _sigmoid(x):
    return 1.0 / (1.0 + jnp.exp(-x))


def _rmsnorm_rows(x, gain):
    ms = jnp.mean(x * x, axis=-1, keepdims=True)
    return x * lax.rsqrt(ms + EPS) * gain


def _split_hi_lo(g):
    hi = g.astype(BF16)
    return hi, (g - hi.astype(F32)).astype(BF16)


def _hgrn_in_proj_kernel(layer_j, xn_ref, w_ref, lbl_ref, o_ref):
    w = lbl_ref.shape[1]

    logits = lbl_ref[...]
    ex = jnp.exp(logits - jnp.max(logits, axis=0, keepdims=True))
    sm = ex / jnp.sum(ex, axis=0, keepdims=True)
    lb = jnp.zeros((1, w), F32)
    for r in range(1, layer_j + 1):
        lb = lb + sm[r:r + 1, :]

    for r in range(0, xn_ref.shape[0], IN_PROJ_SUBTILE):
        rows = slice(r, r + IN_PROJ_SUBTILE)
        xn = xn_ref[rows, :]

        def proj(lo, hi):
            return jnp.dot(xn, w_ref[:, lo:hi], preferred_element_type=F32)

        o_ref[rows, 4 * w:5 * w] = proj(2 * w, 3 * w).astype(BF16)
        f = lb + (1.0 - lb) * _sigmoid(proj(w, 2 * w))
        hi, lo = _split_hi_lo(jnp.log(f))
        o_ref[rows, w:2 * w] = hi
        o_ref[rows, 2 * w:3 * w] = lo
        o_ref[rows, 3 * w:4 * w] = (1.0 - f).astype(BF16)
        q = proj(0, w)
        o_ref[rows, 0:w] = (q * _sigmoid(q)).astype(BF16)
        g = proj(3 * w, 4 * w)
        o_ref[rows, 5 * w:6 * w] = (g * _sigmoid(g)).astype(BF16)
        o_ref[rows, 6 * w:] = proj(4 * w, w_ref.shape[1]).astype(BF16)


def _hgrn_in_proj(xn, w_stack, layer_j, lb_logits, tm):
    m, k = xn.shape
    n_in = w_stack.shape[2]
    n_a, w = lb_logits.shape
    n_out = n_in + 2 * w
    return pl.pallas_call(
        functools.partial(_hgrn_in_proj_kernel, layer_j),
        grid=(m // tm,),
        in_specs=[
            pl.BlockSpec((tm, k), lambda i: (i, 0)),
            pl.BlockSpec((None, k, n_in), lambda i: (layer_j, 0, 0), pipeline_mode=RESIDENT),
            pl.BlockSpec((n_a, w), lambda i: (0, 0)),
        ],
        out_specs=pl.BlockSpec((tm, n_out), lambda i: (i, 0)),
        out_shape=jax.ShapeDtypeStruct((m, n_out), BF16),
        compiler_params=_params("parallel"),
        name="hgrn_in_proj",
    )(xn, w_stack, lb_logits)


def _gla_in_proj_kernel(dk_true, wv, wx, xn_ref, w_ref, wgk_ref, bgk_ref, o_ref):
    dkp = GLA_DK_PAD
    wk = wgk_ref.shape[1]
    wqk = GLA_HEADS * dk_true
    in_v, in_g, in_tail = 2 * wqk, 2 * wqk + wv, 2 * wqk + 2 * wv
    c_q = 2 * wv
    c_x = c_q + 2 * wk
    zero_pad = jnp.zeros((IN_PROJ_SUBTILE, dkp - dk_true), BF16)

    for r in range(0, xn_ref.shape[0], IN_PROJ_SUBTILE):
        rows = slice(r, r + IN_PROJ_SUBTILE)
        xn = xn_ref[rows, :]

        def proj(lo, hi):
            return jnp.dot(xn, w_ref[:, lo:hi], preferred_element_type=F32)

        def log_decay(head):
            cols = slice(head * dkp, (head + 1) * dkp)
            x = jnp.dot(low, wgk_ref[:, cols], preferred_element_type=F32) + bgk_ref[:, cols]
            log_a = (jnp.minimum(x, 0.0) - jnp.log1p(jnp.exp(-jnp.abs(x)))) / GLA_GATE_NORMALIZER
            hi, lo = _split_hi_lo(log_a)
            o_ref[rows, c_x + wx + head * dkp:c_x + wx + (head + 1) * dkp] = hi
            o_ref[rows, c_x + wx + wk + head * dkp:c_x + wx + wk + (head + 1) * dkp] = lo

        def store_heads(col0, values, scale):
            for hd in range(GLA_HEADS):
                piece = values[:, hd * dk_true:(hd + 1) * dk_true]
                o_ref[rows, col0 + hd * dkp:col0 + hd * dkp + dk_true] = (piece * scale).astype(BF16)
                o_ref[rows, col0 + hd * dkp + dk_true:col0 + (hd + 1) * dkp] = zero_pad

        tail = proj(in_tail, w_ref.shape[1])
        low = tail[:, :LANE].astype(BF16)
        o_ref[rows, c_x:c_x + wx] = tail[:, GLA_GATE_RANK:GLA_GATE_RANK + wx].astype(BF16)
        o_ref[rows, 0:wv] = proj(in_v, in_g).astype(BF16)
        log_decay(0)
        g = proj(in_g, in_tail)
        o_ref[rows, wv:c_q] = (g * _sigmoid(g)).astype(BF16)
        log_decay(1)
        qk = proj(0, in_v)
        store_heads(c_q, qk[:, :wqk], dk_true ** -0.5)
        log_decay(2)
        store_heads(c_q + wk, qk[:, wqk:], 1.0)
        log_decay(3)


def _gla_in_proj(xn, w_stack, layer_j, w_gk_pad, b_gk_pad, mix_width, xa_width, dk_true, tm):
    m, k = xn.shape
    n_in = w_stack.shape[2]
    wk = w_gk_pad.shape[1]
    n_out = 2 * mix_width + 2 * wk + xa_width + 2 * wk
    return pl.pallas_call(
        functools.partial(_gla_in_proj_kernel, dk_true, mix_width, xa_width),
        grid=(m // tm,),
        in_specs=[
            pl.BlockSpec((tm, k), lambda i: (i, 0)),
            pl.BlockSpec((None, k, n_in), lambda i: (layer_j, 0, 0), pipeline_mode=RESIDENT),
            pl.BlockSpec((LANE, wk), lambda i: (0, 0)),
            pl.BlockSpec((1, wk), lambda i: (0, 0)),
        ],
        out_specs=pl.BlockSpec((tm, n_out), lambda i: (i, 0)),
        out_shape=jax.ShapeDtypeStruct((m, n_out), BF16),
        compiler_params=_params("parallel"),
        name="gla_in_proj",
    )(xn, w_stack, w_gk_pad, b_gk_pad)


def _mem_kv_kernel(x_ref, g_ref, w_ref, o_ref):
    xn = _rmsnorm_rows(x_ref[...], g_ref[...]).astype(BF16)
    o_ref[...] = jnp.dot(xn, w_ref[...].astype(BF16), preferred_element_type=F32).astype(o_ref.dtype)


def _mem_kv(mem2, gain, w_stack):
    m, k = mem2.shape
    depth, _, n = w_stack.shape
    return pl.pallas_call(
        _mem_kv_kernel,
        grid=(depth,),
        in_specs=[
            pl.BlockSpec((m, k), lambda l: (0, 0)),
            pl.BlockSpec((1, k), lambda l: (0, 0)),
            pl.BlockSpec((None, k, n), lambda l: (l, 0, 0)),
        ],
        out_specs=pl.BlockSpec((None, m, n), lambda l: (l, 0, 0)),
        out_shape=jax.ShapeDtypeStruct((depth, m, n), BF16),
        compiler_params=_params("parallel"),
        name="mem_kv_proj",
    )(mem2, gain.reshape(1, k), w_stack)


def _ffn_up_kernel(xn_ref, wg_ref, wu_ref, o_ref):
    wg = wg_ref[...].astype(BF16)
    wu = wu_ref[...].astype(BF16)
    for r in range(0, xn_ref.shape[0], ROW_SUBTILE):
        xn = xn_ref[r:r + ROW_SUBTILE, :]
        gate = jnp.dot(xn, wg, preferred_element_type=F32)
        up = jnp.dot(xn, wu, preferred_element_type=F32)
        o_ref[r:r + ROW_SUBTILE, :] = (gate * _sigmoid(gate) * up).astype(o_ref.dtype)


def _ffn_up(xn, w_stack, layer, tm, tn):
    m, k = xn.shape
    d_ff = w_stack.shape[2] // 2
    n_blocks = d_ff // tn
    return pl.pallas_call(
        _ffn_up_kernel,
        grid=(m // tm, n_blocks),
        in_specs=[
            pl.BlockSpec((tm, k), lambda i, j: (i, 0)),
            pl.BlockSpec((None, k, tn), lambda i, j: (layer, 0, j)),
            pl.BlockSpec((None, k, tn), lambda i, j: (layer, 0, j + n_blocks)),
        ],
        out_specs=pl.BlockSpec((tm, tn), lambda i, j: (i, j)),
        out_shape=jax.ShapeDtypeStruct((m, d_ff), BF16),
        compiler_params=_params("parallel", "arbitrary"),
        name="ffn_up",
    )(xn, w_stack, w_stack)


def _down_proj_kernel(a_ref, w_ref, r_ref, g_ref, *out_refs):
    h = r_ref[...] + jnp.dot(a_ref[...], w_ref[...], preferred_element_type=F32)
    hn_ref = out_refs[-1]
    hn_ref[...] = _rmsnorm_rows(h, g_ref[...]).astype(hn_ref.dtype)
    if len(out_refs) == 2:
        out_refs[0][...] = h


def _down_proj(a, w_stack, layer, res, gain, final, tm):
    m, k = a.shape
    n = w_stack.shape[2]
    rows = pl.BlockSpec((tm, n), lambda i: (i, 0))
    if final:
        out_specs, out_shape = rows, jax.ShapeDtypeStruct((m, n), F32)
    else:
        out_specs = [rows, rows]
        out_shape = [jax.ShapeDtypeStruct((m, n), F32), jax.ShapeDtypeStruct((m, n), BF16)]
    return pl.pallas_call(
        _down_proj_kernel,
        grid=(m // tm,),
        in_specs=[
            pl.BlockSpec((tm, k), lambda i: (i, 0)),
            pl.BlockSpec((None, k, n), lambda i: (layer, 0, 0), pipeline_mode=RESIDENT),
            rows,
            pl.BlockSpec((1, n), lambda i: (0, 0)),
        ],
        out_specs=out_specs,
        out_shape=out_shape,
        compiler_params=_params("parallel"),
        name="down_proj",
    )(a, w_stack, res, gain.reshape(1, n))


def _out_proj_kernel(y_ref, xq_ref, kv_ref, w_ref, r_ref, g_ref, h_ref, hn_ref, wb_ref):
    @pl.when(pl.program_id(0) == 0)
    def _():
        wb_ref[...] = w_ref[...].astype(BF16)

    wy = y_ref.shape[1]
    d = xq_ref.shape[1] // XA_HEADS
    acc = jnp.dot(y_ref[...], wb_ref[:wy, :], preferred_element_type=F32)
    heads = []
    for hd in range(XA_HEADS):
        q = xq_ref[:, hd * d:(hd + 1) * d]
        k = kv_ref[:, hd * d:(hd + 1) * d]
        v = kv_ref[:, (XA_HEADS + hd) * d:(XA_HEADS + hd + 1) * d]
        s = lax.dot_general(q, k, NT_DIMS, preferred_element_type=F32) * (d ** -0.5)
        p = jnp.exp(s - jnp.max(s, axis=-1, keepdims=True))
        denom = jnp.sum(p, axis=-1, keepdims=True)
        o = jnp.dot(p.astype(BF16), v, preferred_element_type=F32)
        heads.append((o / denom).astype(BF16))
    acc += jnp.dot(jnp.concatenate(heads, axis=1), wb_ref[wy:, :], preferred_element_type=F32)
    h = r_ref[...] + acc
    h_ref[...] = h
    hn_ref[...] = _rmsnorm_rows(h, g_ref[...]).astype(hn_ref.dtype)


def _out_proj(y, proj, xq_block, kv_stack, w_stack, layer, res, gain, batch, tm):
    m, wy = y.shape
    _, k, n = w_stack.shape
    wx = k - wy
    n_mem, wkv = kv_stack.shape[1:]
    steps_per_batch = m // batch // tm
    return pl.pallas_call(
        _out_proj_kernel,
        grid=(m // tm,),
        in_specs=[
            pl.BlockSpec((tm, wy), lambda i: (i, 0)),
            pl.BlockSpec((tm, wx), lambda i: (i, xq_block)),
            pl.BlockSpec((None, n_mem, wkv), lambda i: (layer * batch + i // steps_per_batch, 0, 0)),
            pl.BlockSpec((None, k, n), lambda i: (layer, 0, 0), pipeline_mode=RESIDENT),
            pl.BlockSpec((tm, n), lambda i: (i, 0)),
            pl.BlockSpec((1, n), lambda i: (0, 0)),
        ],
        out_specs=[pl.BlockSpec((tm, n), lambda i: (i, 0)), pl.BlockSpec((tm, n), lambda i: (i, 0))],
        out_shape=[jax.ShapeDtypeStruct((m, n), F32), jax.ShapeDtypeStruct((m, n), BF16)],
        scratch_shapes=[pltpu.VMEM((k, n), BF16)],
        compiler_params=_params("arbitrary"),
        name="xattn_out_proj",
    )(y, proj, kv_stack, w_stack, res, gain.reshape(1, n))


def _rmsnorm_kernel(x_ref, g_ref, o_ref):
    o_ref[...] = _rmsnorm_rows(x_ref[...], g_ref[...]).astype(o_ref.dtype)


def _rmsnorm(x, gain, tm):
    m, k = x.shape
    return pl.pallas_call(
        _rmsnorm_kernel,
        grid=(m // tm,),
        in_specs=[pl.BlockSpec((tm, k), lambda i: (i, 0)), pl.BlockSpec((1, k), lambda i: (0, 0))],
        out_specs=pl.BlockSpec((tm, k), lambda i: (i, 0)),
        out_shape=jax.ShapeDtypeStruct((m, k), BF16),
        compiler_params=_params("parallel"),
        name="input_rmsnorm",
    )(x, gain.reshape(1, k))


def _chunk_masks():
    t = lax.broadcasted_iota(jnp.int32, (CHUNK, CHUNK), 0)
    s = lax.broadcasted_iota(jnp.int32, (CHUNK, CHUNK), 1)
    tril = (s <= t)
    return tril.astype(BF16), tril


def _chunk_cumsum(tril_b, hi, lo):
    dk = hi.shape[1]
    both = jnp.dot(tril_b, jnp.concatenate([hi, lo], axis=1), preferred_element_type=F32)
    return both[:, :dk] + both[:, dk:]


def _chunk_scores(q, k, vb, cum, sub):
    dk = q.shape[1]
    n_sub = CHUNK // sub
    blocks = [slice(sub * r, sub * (r + 1)) for r in range(n_sub)]
    mid = [cum[sub * j + sub // 2 - 1:sub * j + sub // 2, :] for j in range(n_sub)]
    last = cum[CHUNK - 1:CHUNK, :]
    zero = jnp.zeros((sub, dk), F32)

    q_slots, k_slots, k_last = [], [], []
    for j in range(n_sub):
        q_slots.append(jnp.concatenate(
            [zero if r < j else q[blocks[r], :] * jnp.exp(cum[blocks[r], :] - mid[j])
             for r in range(n_sub)], axis=0))
        k_mid = k[blocks[j], :] * jnp.exp(mid[j] - cum[blocks[j], :])
        k_slots.append(jnp.concatenate([k_mid if r == j else zero for r in range(n_sub)], axis=0))
        k_last.append(k_mid * jnp.exp(last - mid[j]))
    s_raw = lax.dot_general(jnp.concatenate(q_slots, axis=1).astype(BF16),
                            jnp.concatenate(k_slots, axis=1).astype(BF16),
                            NT_DIMS, preferred_element_type=F32)
    upd = lax.dot_general(vb, jnp.concatenate(k_last, axis=0).astype(BF16), TN_DIMS,
                          preferred_element_type=F32)
    q_dec = (q_slots[0] * jnp.exp(mid[0])).astype(BF16)
    return s_raw, upd, q_dec, jnp.exp(last), vb


def _chunk_readout(s_raw, tril, vb, q_dec, st):
    intra = jnp.dot(jnp.where(tril, s_raw, 0.0).astype(BF16), vb, preferred_element_type=F32)
    inter = lax.dot_general(q_dec, st.astype(BF16), NT_DIMS, preferred_element_type=F32)
    return intra + inter


def _gated_head_norm(o, gain, gate):
    ms = jnp.mean(o * o, axis=-1, keepdims=True)
    return o * lax.rsqrt(ms + EPS) * gain * gate


def _run_staggered(n_chunks, stages):
    n_groups = n_chunks // STAGGER_GROUP
    for slot in range(n_groups + len(stages) - 1):
        for depth in reversed(range(len(stages))):
            group = slot - depth
            if 0 <= group < n_groups:
                for c in range(group * STAGGER_GROUP, (group + 1) * STAGGER_GROUP):
                    stages[depth](c)


def _mixer_kernel(sub, q_ref, k_ref, v_ref, gate_ref, hi_ref, lo_ref, on_ref, y_ref, st_ref):
    @pl.when(pl.program_id(2) == 0)
    def _():
        st_ref[...] = jnp.zeros_like(st_ref)

    tril_b, tril = _chunk_masks()
    gain = on_ref[...]
    n_chunks = q_ref.shape[0] // CHUNK
    rows = [slice(c * CHUNK, (c + 1) * CHUNK) for c in range(n_chunks)]
    ctx = [{} for _ in range(n_chunks)]
    state = [st_ref[...]]

    def decay(c):
        ctx[c]["cum"] = _chunk_cumsum(tril_b, hi_ref[rows[c], :], lo_ref[rows[c], :])

    def scores(c):
        ctx[c]["parts"] = _chunk_scores(q_ref[rows[c], :].astype(F32), k_ref[rows[c], :].astype(F32),
                                        v_ref[rows[c], :], ctx[c].pop("cum"), sub)

    def readout(c):
        s_raw, upd, q_dec, e_last, vb = ctx[c].pop("parts")
        ctx[c]["o"] = _chunk_readout(s_raw, tril, vb, q_dec, state[0])
        state[0] = state[0] * e_last + upd

    def emit(c):
        gate = gate_ref[rows[c], :].astype(F32)
        y_ref[rows[c], :] = _gated_head_norm(ctx[c].pop("o"), gain, gate).astype(y_ref.dtype)

    _run_staggered(n_chunks, [decay, scores, readout, emit])
    st_ref[...] = state[0]


def _mixer(proj, onorm, cols, heads, dk, dv, sub, batch, t_blk, name):
    m = proj.shape[0]
    n_t = m // batch // t_blk

    def group(key, width):
        block0, rem = divmod(cols[key], width)
        assert rem == 0
        return pl.BlockSpec((t_blk, width), lambda b, h, t: (b * n_t + t, block0 + h))

    return pl.pallas_call(
        functools.partial(_mixer_kernel, sub),
        grid=(batch, heads, n_t),
        in_specs=[group("q", dk), group("k", dk), group("v", dv), group("gate", dv),
                  group("hi", dk), group("lo", dk),
                  pl.BlockSpec((1, dv), lambda b, h, t: (0, 0))],
        out_specs=pl.BlockSpec((t_blk, dv), lambda b, h, t: (b * n_t + t, h)),
        out_shape=jax.ShapeDtypeStruct((m, heads * dv), BF16),
        scratch_shapes=[pltpu.VMEM((dv, dk), F32)],
        compiler_params=_params("parallel", "parallel", "arbitrary"),
        name=name,
    )(proj, proj, proj, proj, proj, proj, onorm.reshape(1, dv))


def _gla_pad_decay_weights(w_gk, b_gk):
    heads, rank, dkp = GLA_HEADS, GLA_GATE_RANK, GLA_DK_PAD
    kw = w_gk.shape[1]
    dk = kw // heads

    def pad_heads(w):
        w = w.reshape(w.shape[0], heads, dk)
        return jnp.pad(w, ((0, 0), (0, 0), (0, dkp - dk))).reshape(w.shape[0], heads * dkp)

    w_gk_pad = jnp.pad(pad_heads(w_gk), ((0, LANE - rank), (0, 0))).astype(BF16)
    b_gk_pad = pad_heads(b_gk.reshape(1, kw)).astype(F32)
    return w_gk_pad, b_gk_pad, dk


def kernel(x, mem, norm_mix, norm_ffn, norm_mem, norm_final, hgrn_w_in, hgrn_lb_logits, hgrn_onorm,
           gla_w_in, gla_w_gk, gla_b_gk, gla_onorm, w_mem_kv, w_out, w_gate_up, w_down):
    batch, seq, d_model = x.shape
    n_mem = mem.shape[1]
    depth = norm_mix.shape[0]
    mix_width = hgrn_lb_logits.shape[1]
    xa_width = d_model - mix_width
    m = batch * seq

    h = x.reshape(m, d_model)
    kv = _mem_kv(mem.reshape(batch * n_mem, d_model), norm_mem, w_mem_kv)
    kv = kv.reshape(depth * batch, n_mem, 2 * xa_width)

    hgrn_w = hgrn_w_in.astype(BF16)
    gla_w = gla_w_in.astype(BF16)
    w_down_b = w_down.astype(BF16)

    hn = _rmsnorm(h, norm_mix[0], tm=512)
    for layer in range(depth):
        j = layer // 2
        if layer % 2 == 0:
            proj = _hgrn_in_proj(hn, hgrn_w, j, hgrn_lb_logits, tm=512)
            w = mix_width
            cols = dict(q=0, hi=w, lo=2 * w, k=3 * w, v=4 * w, gate=5 * w)
            y = _mixer(proj, hgrn_onorm[j], cols, mix_width // HGRN_HEAD_DIM, HGRN_HEAD_DIM,
                       HGRN_HEAD_DIM, HGRN_SUB, batch, t_blk=4096, name="hgrn2_mixer")
            xq_col = 6 * w
        else:
            w_gk_pad, b_gk_pad, dk = _gla_pad_decay_weights(gla_w_gk[j], gla_b_gk[j])
            proj = _gla_in_proj(hn, gla_w, j, w_gk_pad, b_gk_pad, mix_width, xa_width, dk, tm=512)
            wk = GLA_HEADS * GLA_DK_PAD
            xq_col = 2 * mix_width + 2 * wk
            cols = dict(v=0, gate=mix_width, q=2 * mix_width, k=2 * mix_width + wk,
                        hi=xq_col + xa_width, lo=xq_col + xa_width + wk)
            y = _mixer(proj, gla_onorm[j], cols, GLA_HEADS, GLA_DK_PAD, mix_width // GLA_HEADS,
                       GLA_SUB, batch, t_blk=4096, name="gla_mixer")
        assert xq_col % xa_width == 0
        h, hn = _out_proj(y, proj, xq_col // xa_width, kv, w_out, layer, h, norm_ffn[layer], batch,
                          tm=512)
        act = _ffn_up(hn, w_gate_up, layer, tm=2048, tn=512)
        if layer + 1 < depth:
            h, hn = _down_proj(act, w_down_b, layer, h, norm_mix[layer + 1], final=False, tm=256)
        else:
            out = _down_proj(act, w_down_b, layer, h, norm_final, final=True, tm=256)
    return out.reshape(batch, seq, d_model)
```

```python
import functools

import jax
import jax.numpy as jnp
from jax import lax
from jax.experimental import pallas as pl
from jax.experimental.pallas import tpu as pltpu

F32 = jnp.float32
BF16 = jnp.bfloat16

EPS = 1e-6
CHUNK = 64
STAGGER_GROUP = 8
ROW_SUBTILE = 256
IN_PROJ_SUBTILE = 256
DOWN_PROJ_PIECE = 512
HGRN_SUB = 32
GLA_SUB = 64
XA_HEADS = 4
HGRN_HEAD_DIM = 128
GLA_HEADS = 4
GLA_GATE_RANK = 16
GLA_GATE_NORMALIZER = 16.0
LANE = 128
GLA_DK_PAD = 256
V7X_VMEM_LIMIT = 56 * 1024 * 1024

NT_DIMS = (((1,), (1,)), ((), ()))
TN_DIMS = (((0,), (0,)), ((), ()))

RESIDENT = pl.Buffered(1)


def _params(*semantics):
    return pltpu.CompilerParams(dimension_semantics=semantics, vmem_limit_bytes=V7X_VMEM_LIMIT)


def _sigmoid(x):
    return 1.0 / (1.0 + jnp.exp(-x))


def _rmsnorm_rows(x, gain):
    ms = jnp.mean(x * x, axis=-1, keepdims=True)
    return x * lax.rsqrt(ms + EPS) * gain


def _gain_and_scale(x, gain):
    ms = jnp.mean(x * x, axis=-1, keepdims=True)
    return (x * gain).astype(BF16), lax.rsqrt(ms + EPS)


def _split_hi_lo(g):
    hi = g.astype(BF16)
    return hi, (g - hi.astype(F32)).astype(BF16)


def _hgrn_in_proj_kernel(layer_j, xn_ref, rs_ref, w_ref, lbl_ref, o_ref):
    w = lbl_ref.shape[1]

    logits = lbl_ref[...]
    ex = jnp.exp(logits - jnp.max(logits, axis=0, keepdims=True))
    sm = ex / jnp.sum(ex, axis=0, keepdims=True)
    lb = jnp.zeros((1, w), F32)
    for r in range(1, layer_j + 1):
        lb = lb + sm[r:r + 1, :]

    for r in range(0, xn_ref.shape[0], IN_PROJ_SUBTILE):
        rows = slice(r, r + IN_PROJ_SUBTILE)
        xn = xn_ref[rows, :]
        rs = rs_ref[rows, :]

        def proj(lo, hi):
            return jnp.dot(xn, w_ref[:, lo:hi], preferred_element_type=F32) * rs

        o_ref[rows, 4 * w:5 * w] = proj(2 * w, 3 * w).astype(BF16)
        f = lb + (1.0 - lb) * _sigmoid(proj(w, 2 * w))
        hi, lo = _split_hi_lo(jnp.log(f))
        o_ref[rows, w:2 * w] = hi
        o_ref[rows, 2 * w:3 * w] = lo
        o_ref[rows, 3 * w:4 * w] = (1.0 - f).astype(BF16)
        q = proj(0, w)
        o_ref[rows, 0:w] = (q * _sigmoid(q)).astype(BF16)
        g = proj(3 * w, 4 * w)
        o_ref[rows, 5 * w:6 * w] = (g * _sigmoid(g)).astype(BF16)
        o_ref[rows, 6 * w:] = proj(4 * w, w_ref.shape[1]).astype(BF16)


def _hgrn_in_proj(xn, rs, w_stack, layer_j, lb_logits, tm):
    m, k = xn.shape
    n_in = w_stack.shape[2]
    n_a, w = lb_logits.shape
    n_out = n_in + 2 * w
    return pl.pallas_call(
        functools.partial(_hgrn_in_proj_kernel, layer_j),
        grid=(m // tm,),
        in_specs=[
            pl.BlockSpec((tm, k), lambda i: (i, 0)),
            pl.BlockSpec((tm, 1), lambda i: (i, 0)),
            pl.BlockSpec((None, k, n_in), lambda i: (layer_j, 0, 0), pipeline_mode=RESIDENT),
            pl.BlockSpec((n_a, w), lambda i: (0, 0)),
        ],
        out_specs=pl.BlockSpec((tm, n_out), lambda i: (i, 0)),
        out_shape=jax.ShapeDtypeStruct((m, n_out), BF16),
        compiler_params=_params("parallel"),
        name="hgrn_in_proj",
    )(xn, rs, w_stack, lb_logits)


def _gla_in_proj_kernel(dk_true, wv, wx, xn_ref, rs_ref, w_ref, wgk_ref, bgk_ref, o_ref):
    dkp = GLA_DK_PAD
    wk = wgk_ref.shape[1]
    wqk = GLA_HEADS * dk_true
    in_v, in_g, in_tail = 2 * wqk, 2 * wqk + wv, 2 * wqk + 2 * wv
    c_q = 2 * wv
    c_x = c_q + 2 * wk
    zero_pad = jnp.zeros((IN_PROJ_SUBTILE, dkp - dk_true), BF16)

    for r in range(0, xn_ref.shape[0], IN_PROJ_SUBTILE):
        rows = slice(r, r + IN_PROJ_SUBTILE)
        xn = xn_ref[rows, :]
        rs = rs_ref[rows, :]

        def proj(lo, hi):
            return jnp.dot(xn, w_ref[:, lo:hi], preferred_element_type=F32) * rs

        def log_decay(head):
            cols = slice(head * dkp, (head + 1) * dkp)
            x = jnp.dot(low, wgk_ref[:, cols], preferred_element_type=F32) + bgk_ref[:, cols]
            log_a = (jnp.minimum(x, 0.0) - jnp.log1p(jnp.exp(-jnp.abs(x)))) / GLA_GATE_NORMALIZER
            hi, lo = _split_hi_lo(log_a)
            o_ref[rows, c_x + wx + head * dkp:c_x + wx + (head + 1) * dkp] = hi
            o_ref[rows, c_x + wx + wk + head * dkp:c_x + wx + wk + (head + 1) * dkp] = lo

        def store_heads(col0, values, scale):
            for hd in range(GLA_HEADS):
                piece = values[:, hd * dk_true:(hd + 1) * dk_true]
                o_ref[rows, col0 + hd * dkp:col0 + hd * dkp + dk_true] = (piece * scale).astype(BF16)
                o_ref[rows, col0 + hd * dkp + dk_true:col0 + (hd + 1) * dkp] = zero_pad

        tail = proj(in_tail, w_ref.shape[1])
        low = tail[:, :LANE].astype(BF16)
        o_ref[rows, c_x:c_x + wx] = tail[:, GLA_GATE_RANK:GLA_GATE_RANK + wx].astype(BF16)
        o_ref[rows, 0:wv] = proj(in_v, in_g).astype(BF16)
        log_decay(0)
        g = proj(in_g, in_tail)
        o_ref[rows, wv:c_q] = (g * _sigmoid(g)).astype(BF16)
        log_decay(1)
        qk = proj(0, in_v)
        store_heads(c_q, qk[:, :wqk], dk_true ** -0.5)
        log_decay(2)
        store_heads(c_q + wk, qk[:, wqk:], 1.0)
        log_decay(3)


def _gla_in_proj(xn, rs, w_stack, layer_j, w_gk_pad, b_gk_pad, mix_width, xa_width, dk_true, tm):
    m, k = xn.shape
    n_in = w_stack.shape[2]
    wk = w_gk_pad.shape[1]
    n_out = 2 * mix_width + 2 * wk + xa_width + 2 * wk
    return pl.pallas_call(
        functools.partial(_gla_in_proj_kernel, dk_true, mix_width, xa_width),
        grid=(m // tm,),
        in_specs=[
            pl.BlockSpec((tm, k), lambda i: (i, 0)),
            pl.BlockSpec((tm, 1), lambda i: (i, 0)),
            pl.BlockSpec((None, k, n_in), lambda i: (layer_j, 0, 0), pipeline_mode=RESIDENT),
            pl.BlockSpec((LANE, wk), lambda i: (0, 0)),
            pl.BlockSpec((1, wk), lambda i: (0, 0)),
        ],
        out_specs=pl.BlockSpec((tm, n_out), lambda i: (i, 0)),
        out_shape=jax.ShapeDtypeStruct((m, n_out), BF16),
        compiler_params=_params("parallel"),
        name="gla_in_proj",
    )(xn, rs, w_stack, w_gk_pad, b_gk_pad)


def _mem_kv_kernel(x_ref, g_ref, w_ref, o_ref):
    xn = _rmsnorm_rows(x_ref[...], g_ref[...]).astype(BF16)
    o_ref[...] = jnp.dot(xn, w_ref[...].astype(BF16), preferred_element_type=F32).astype(o_ref.dtype)


def _mem_kv(mem2, gain, w_stack):
    m, k = mem2.shape
    depth, _, n = w_stack.shape
    return pl.pallas_call(
        _mem_kv_kernel,
        grid=(depth,),
        in_specs=[
            pl.BlockSpec((m, k), lambda l: (0, 0)),
            pl.BlockSpec((1, k), lambda l: (0, 0)),
            pl.BlockSpec((None, k, n), lambda l: (l, 0, 0)),
        ],
        out_specs=pl.BlockSpec((None, m, n), lambda l: (l, 0, 0)),
        out_shape=jax.ShapeDtypeStruct((depth, m, n), BF16),
        compiler_params=_params("parallel"),
        name="mem_kv_proj",
    )(mem2, gain.reshape(1, k), w_stack)


def _ffn_up_kernel(xn_ref, rs_ref, wg_ref, wu_ref, o_ref):
    wg = wg_ref[...].astype(BF16)
    wu = wu_ref[...].astype(BF16)
    for r in range(0, xn_ref.shape[0], ROW_SUBTILE):
        xn = xn_ref[r:r + ROW_SUBTILE, :]
        rs = rs_ref[r:r + ROW_SUBTILE, :]
        gate = jnp.dot(xn, wg, preferred_element_type=F32) * rs
        up = jnp.dot(xn, wu, preferred_element_type=F32) * rs
        o_ref[r:r + ROW_SUBTILE, :] = (gate * _sigmoid(gate) * up).astype(o_ref.dtype)


def _ffn_up(xn, rs, w_stack, layer, tm, tn):
    m, k = xn.shape
    d_ff = w_stack.shape[2] // 2
    n_blocks = d_ff // tn
    return pl.pallas_call(
        _ffn_up_kernel,
        grid=(m // tm, n_blocks),
        in_specs=[
            pl.BlockSpec((tm, k), lambda i, j: (i, 0)),
            pl.BlockSpec((tm, 1), lambda i, j: (i, 0)),
            pl.BlockSpec((None, k, tn), lambda i, j: (layer, 0, j)),
            pl.BlockSpec((None, k, tn), lambda i, j: (layer, 0, j + n_blocks)),
        ],
        out_specs=pl.BlockSpec((tm, tn), lambda i, j: (i, j)),
        out_shape=jax.ShapeDtypeStruct((m, d_ff), BF16),
        compiler_params=_params("parallel", "arbitrary"),
        name="ffn_up",
    )(xn, rs, w_stack, w_stack)


def _down_proj_kernel(a_ref, w_ref, r_ref, g_ref, *out_refs):
    a = a_ref[...]
    n = r_ref.shape[1]
    final = len(out_refs) == 1
    pieces, ssq = [], 0.0
    for c in range(0, n, DOWN_PROJ_PIECE):
        cols = slice(c, c + DOWN_PROJ_PIECE)
        h = r_ref[:, cols] + jnp.dot(a, w_ref[:, cols], preferred_element_type=F32)
        ssq = ssq + jnp.sum(h * h, axis=-1, keepdims=True)
        if final:
            pieces.append(h)
        else:
            out_refs[0][:, cols] = h
            out_refs[1][:, cols] = (h * g_ref[:, cols]).astype(BF16)
    scale = lax.rsqrt(ssq / n + EPS)
    if final:
        for i, h in enumerate(pieces):
            cols = slice(i * DOWN_PROJ_PIECE, (i + 1) * DOWN_PROJ_PIECE)
            out_refs[0][:, cols] = h * scale * g_ref[:, cols]
    else:
        out_refs[2][...] = scale


def _down_proj(a, w_stack, layer, res, gain, final, tm):
    m, k = a.shape
    n = w_stack.shape[2]
    rows = pl.BlockSpec((tm, n), lambda i: (i, 0))
    if final:
        out_specs, out_shape = rows, jax.ShapeDtypeStruct((m, n), F32)
    else:
        out_specs = [rows, rows, pl.BlockSpec((tm, 1), lambda i: (i, 0))]
        out_shape = [jax.ShapeDtypeStruct((m, n), F32), jax.ShapeDtypeStruct((m, n), BF16),
                     jax.ShapeDtypeStruct((m, 1), F32)]
    return pl.pallas_call(
        _down_proj_kernel,
        grid=(m // tm,),
        in_specs=[
            pl.BlockSpec((tm, k), lambda i: (i, 0)),
            pl.BlockSpec((None, k, n), lambda i: (layer, 0, 0), pipeline_mode=RESIDENT),
            rows,
            pl.BlockSpec((1, n), lambda i: (0, 0)),
        ],
        out_specs=out_specs,
        out_shape=out_shape,
        compiler_params=_params("parallel"),
        name="down_proj",
    )(a, w_stack, res, gain.reshape(1, n))


def _out_proj_kernel(y_ref, xq_ref, kv_ref, w_ref, r_ref, g_ref, h_ref, hn_ref, rs_ref, wb_ref):
    @pl.when(pl.program_id(0) == 0)
    def _():
        wb_ref[...] = w_ref[...].astype(BF16)

    wy = y_ref.shape[1]
    d = xq_ref.shape[1] // XA_HEADS
    acc = jnp.dot(y_ref[...], wb_ref[:wy, :], preferred_element_type=F32)
    heads = []
    for hd in range(XA_HEADS):
        q = xq_ref[:, hd * d:(hd + 1) * d]
        k = kv_ref[:, hd * d:(hd + 1) * d]
        v = kv_ref[:, (XA_HEADS + hd) * d:(XA_HEADS + hd + 1) * d]
        s = lax.dot_general(q, k, NT_DIMS, preferred_element_type=F32) * (d ** -0.5)
        p = jnp.exp(s - jnp.max(s, axis=-1, keepdims=True))
        denom = jnp.sum(p, axis=-1, keepdims=True)
        o = jnp.dot(p.astype(BF16), v, preferred_element_type=F32)
        heads.append((o / denom).astype(BF16))
    acc += jnp.dot(jnp.concatenate(heads, axis=1), wb_ref[wy:, :], preferred_element_type=F32)
    h = r_ref[...] + acc
    h_ref[...] = h
    hn_ref[...], rs_ref[...] = _gain_and_scale(h, g_ref[...])


def _out_proj(y, proj, xq_block, kv_stack, w_stack, layer, res, gain, batch, tm):
    m, wy = y.shape
    _, k, n = w_stack.shape
    wx = k - wy
    n_mem, wkv = kv_stack.shape[1:]
    steps_per_batch = m // batch // tm
    return pl.pallas_call(
        _out_proj_kernel,
        grid=(m // tm,),
        in_specs=[
            pl.BlockSpec((tm, wy), lambda i: (i, 0)),
            pl.BlockSpec((tm, wx), lambda i: (i, xq_block)),
            pl.BlockSpec((None, n_mem, wkv), lambda i: (layer * batch + i // steps_per_batch, 0, 0)),
            pl.BlockSpec((None, k, n), lambda i: (layer, 0, 0), pipeline_mode=RESIDENT),
            pl.BlockSpec((tm, n), lambda i: (i, 0)),
            pl.BlockSpec((1, n), lambda i: (0, 0)),
        ],
        out_specs=[pl.BlockSpec((tm, n), lambda i: (i, 0)), pl.BlockSpec((tm, n), lambda i: (i, 0)),
                   pl.BlockSpec((tm, 1), lambda i: (i, 0))],
        out_shape=[jax.ShapeDtypeStruct((m, n), F32), jax.ShapeDtypeStruct((m, n), BF16),
                   jax.ShapeDtypeStruct((m, 1), F32)],
        scratch_shapes=[pltpu.VMEM((k, n), BF16)],
        compiler_params=_params("arbitrary"),
        name="xattn_out_proj",
    )(y, proj, kv_stack, w_stack, res, gain.reshape(1, n))


def _rmsnorm_kernel(x_ref, g_ref, xn_ref, rs_ref):
    xn_ref[...], rs_ref[...] = _gain_and_scale(x_ref[...], g_ref[...])


def _rmsnorm(x, gain, tm):
    m, k = x.shape
    return pl.pallas_call(
        _rmsnorm_kernel,
        grid=(m // tm,),
        in_specs=[pl.BlockSpec((tm, k), lambda i: (i, 0)), pl.BlockSpec((1, k), lambda i: (0, 0))],
        out_specs=[pl.BlockSpec((tm, k), lambda i: (i, 0)), pl.BlockSpec((tm, 1), lambda i: (i, 0))],
        out_shape=[jax.ShapeDtypeStruct((m, k), BF16), jax.ShapeDtypeStruct((m, 1), F32)],
        compiler_params=_params("parallel"),
        name="input_rmsnorm",
    )(x, gain.reshape(1, k))


def _chunk_masks():
    t = lax.broadcasted_iota(jnp.int32, (CHUNK, CHUNK), 0)
    s = lax.broadcasted_iota(jnp.int32, (CHUNK, CHUNK), 1)
    tril = (s <= t)
    return tril.astype(BF16), tril


def _chunk_cumsum(tril_b, hi, lo):
    dk = hi.shape[1]
    both = jnp.dot(tril_b, jnp.concatenate([hi, lo], axis=1), preferred_element_type=F32)
    return both[:, :dk] + both[:, dk:]


def _chunk_scores(q, k, vb, cum, sub):
    dk = q.shape[1]
    n_sub = CHUNK // sub
    blocks = [slice(sub * r, sub * (r + 1)) for r in range(n_sub)]
    mid = [cum[sub * j + sub // 2 - 1:sub * j + sub // 2, :] for j in range(n_sub)]
    last = cum[CHUNK - 1:CHUNK, :]
    zero = jnp.zeros((sub, dk), F32)

    q_slots, k_slots, k_last = [], [], []
    for j in range(n_sub):
        q_slots.append(jnp.concatenate(
            [zero if r < j else q[blocks[r], :] * jnp.exp(cum[blocks[r], :] - mid[j])
             for r in range(n_sub)], axis=0))
        k_mid = k[blocks[j], :] * jnp.exp(mid[j] - cum[blocks[j], :])
        k_slots.append(jnp.concatenate([k_mid if r == j else zero for r in range(n_sub)], axis=0))
        k_last.append(k_mid * jnp.exp(last - mid[j]))
    s_raw = lax.dot_general(jnp.concatenate(q_slots, axis=1).astype(BF16),
                            jnp.concatenate(k_slots, axis=1).astype(BF16),
                            NT_DIMS, preferred_element_type=F32)
    upd = lax.dot_general(vb, jnp.concatenate(k_last, axis=0).astype(BF16), TN_DIMS,
                          preferred_element_type=F32)
    q_dec = (q_slots[0] * jnp.exp(mid[0])).astype(BF16)
    return s_raw, upd, q_dec, jnp.exp(last), vb


def _chunk_readout(s_raw, tril, vb, q_dec, st):
    intra = jnp.dot(jnp.where(tril, s_raw, 0.0).astype(BF16), vb, preferred_element_type=F32)
    inter = lax.dot_general(q_dec, st.astype(BF16), NT_DIMS, preferred_element_type=F32)
    return intra + inter


def _gated_head_norm(o, gain, gate):
    ms = jnp.mean(o * o, axis=-1, keepdims=True)
    return o * lax.rsqrt(ms + EPS) * gain * gate


def _run_staggered(n_chunks, stages):
    n_groups = n_chunks // STAGGER_GROUP
    for slot in range(n_groups + len(stages) - 1):
        for depth in reversed(range(len(stages))):
            group = slot - depth
            if 0 <= group < n_groups:
                for c in range(group * STAGGER_GROUP, (group + 1) * STAGGER_GROUP):
                    stages[depth](c)


def _mixer_kernel(sub, q_ref, k_ref, v_ref, gate_ref, hi_ref, lo_ref, on_ref, y_ref, st_ref):
    @pl.when(pl.program_id(2) == 0)
    def _():
        st_ref[...] = jnp.zeros_like(st_ref)

    tril_b, tril = _chunk_masks()
    gain = on_ref[...]
    n_chunks = q_ref.shape[0] // CHUNK
    rows = [slice(c * CHUNK, (c + 1) * CHUNK) for c in range(n_chunks)]
    ctx = [{} for _ in range(n_chunks)]
    state = [st_ref[...]]

    def decay(c):
        ctx[c]["cum"] = _chunk_cumsum(tril_b, hi_ref[rows[c], :], lo_ref[rows[c], :])

    def scores(c):
        ctx[c]["parts"] = _chunk_scores(q_ref[rows[c], :].astype(F32), k_ref[rows[c], :].astype(F32),
                                        v_ref[rows[c], :], ctx[c].pop("cum"), sub)

    def readout(c):
        s_raw, upd, q_dec, e_last, vb = ctx[c].pop("parts")
        ctx[c]["o"] = _chunk_readout(s_raw, tril, vb, q_dec, state[0])
        state[0] = state[0] * e_last + upd

    def emit(c):
        gate = gate_ref[rows[c], :].astype(F32)
        y_ref[rows[c], :] = _gated_head_norm(ctx[c].pop("o"), gain, gate).astype(y_ref.dtype)

    _run_staggered(n_chunks, [decay, scores, readout, emit])
    st_ref[...] = state[0]


def _mixer(proj, onorm, cols, heads, dk, dv, sub, batch, t_blk, name):
    m = proj.shape[0]
    n_t = m // batch // t_blk

    def group(key, width):
        block0, rem = divmod(cols[key], width)
        assert rem == 0
        return pl.BlockSpec((t_blk, width), lambda b, h, t: (b * n_t + t, block0 + h))

    return pl.pallas_call(
        functools.partial(_mixer_kernel, sub),
        grid=(batch, heads, n_t),
        in_specs=[group("q", dk), group("k", dk), group("v", dv), group("gate", dv),
                  group("hi", dk), group("lo", dk),
                  pl.BlockSpec((1, dv), lambda b, h, t: (0, 0))],
        out_specs=pl.BlockSpec((t_blk, dv), lambda b, h, t: (b * n_t + t, h)),
        out_shape=jax.ShapeDtypeStruct((m, heads * dv), BF16),
        scratch_shapes=[pltpu.VMEM((dv, dk), F32)],
        compiler_params=_params("parallel", "parallel", "arbitrary"),
        name=name,
    )(proj, proj, proj, proj, proj, proj, onorm.reshape(1, dv))


def _gla_pad_decay_weights(w_gk, b_gk):
    heads, rank, dkp = GLA_HEADS, GLA_GATE_RANK, GLA_DK_PAD
    kw = w_gk.shape[1]
    dk = kw // heads

    def pad_heads(w):
        w = w.reshape(w.shape[0], heads, dk)
        return jnp.pad(w, ((0, 0), (0, 0), (0, dkp - dk))).reshape(w.shape[0], heads * dkp)

    w_gk_pad = jnp.pad(pad_heads(w_gk), ((0, LANE - rank), (0, 0))).astype(BF16)
    b_gk_pad = pad_heads(b_gk.reshape(1, kw)).astype(F32)
    return w_gk_pad, b_gk_pad, dk


def kernel(x, mem, norm_mix, norm_ffn, norm_mem, norm_final, hgrn_w_in, hgrn_lb_logits, hgrn_onorm,
           gla_w_in, gla_w_gk, gla_b_gk, gla_onorm, w_mem_kv, w_out, w_gate_up, w_down):
    batch, seq, d_model = x.shape
    n_mem = mem.shape[1]
    depth = norm_mix.shape[0]
    mix_width = hgrn_lb_logits.shape[1]
    xa_width = d_model - mix_width
    m = batch * seq

    h = x.reshape(m, d_model)
    kv = _mem_kv(mem.reshape(batch * n_mem, d_model), norm_mem, w_mem_kv)
    kv = kv.reshape(depth * batch, n_mem, 2 * xa_width)

    hgrn_w = hgrn_w_in.astype(BF16)
    gla_w = gla_w_in.astype(BF16)
    w_down_b = w_down.astype(BF16)

    hn, rs = _rmsnorm(h, norm_mix[0], tm=512)
    for layer in range(depth):
        j = layer // 2
        if layer % 2 == 0:
            proj = _hgrn_in_proj(hn, rs, hgrn_w, j, hgrn_lb_logits, tm=512)
            w = mix_width
            cols = dict(q=0, hi=w, lo=2 * w, k=3 * w, v=4 * w, gate=5 * w)
            y = _mixer(proj, hgrn_onorm[j], cols, mix_width // HGRN_HEAD_DIM, HGRN_HEAD_DIM,
                       HGRN_HEAD_DIM, HGRN_SUB, batch, t_blk=4096, name="hgrn2_mixer")
            xq_col = 6 * w
        else:
            w_gk_pad, b_gk_pad, dk = _gla_pad_decay_weights(gla_w_gk[j], gla_b_gk[j])
            proj = _gla_in_proj(hn, rs, gla_w, j, w_gk_pad, b_gk_pad, mix_width, xa_width, dk, tm=512)
            wk = GLA_HEADS * GLA_DK_PAD
            xq_col = 2 * mix_width + 2 * wk
            cols = dict(v=0, gate=mix_width, q=2 * mix_width, k=2 * mix_width + wk,
                        hi=xq_col + xa_width, lo=xq_col + xa_width + wk)
            y = _mixer(proj, gla_onorm[j], cols, GLA_HEADS, GLA_DK_PAD, mix_width // GLA_HEADS,
                       GLA_SUB, batch, t_blk=4096, name="gla_mixer")
        assert xq_col % xa_width == 0
        h, hn, rs = _out_proj(y, proj, xq_col // xa_width, kv, w_out, layer, h, norm_ffn[layer],
                              batch, tm=512)
        act = _ffn_up(hn, rs, w_gate_up, layer, tm=2048, tn=512)
        if layer + 1 < depth:
            h, hn, rs = _down_proj(act, w_down_b, layer, h, norm_mix[layer + 1], final=False, tm=256)
        else:
            out = _down_proj(act, w_down_b, layer, h, norm_final, final=True, tm=256)
    return out.reshape(batch, seq, d_model)
```

```python
import functools

import jax
import jax.numpy as jnp
from jax import lax
from jax.experimental import pallas as pl
from jax.experimental.pallas import tpu as pltpu

F32 = jnp.float32
BF16 = jnp.bfloat16

EPS = 1e-6
CHUNK = 64
STAGGER_GROUP = 8
ROW_SUBTILE = 256
IN_PROJ_SUBTILE = 256
HGRN_SUB = 32
GLA_SUB = 64
XA_HEADS = 4
HGRN_HEAD_DIM = 128
GLA_HEADS = 4
GLA_GATE_RANK = 16
GLA_GATE_NORMALIZER = 16.0
LANE = 128
GLA_DK_PAD = 256
V7X_VMEM_LIMIT = 56 * 1024 * 1024

NT_DIMS = (((1,), (1,)), ((), ()))
TN_DIMS = (((0,), (0,)), ((), ()))

RESIDENT = pl.Buffered(1)


def _params(*semantics):
    return pltpu.CompilerParams(dimension_semantics=semantics, vmem_limit_bytes=V7X_VMEM_LIMIT)


def _sigmoid(x):
    return 1.0 / (1.0 + jnp.exp(-x))


def _rmsnorm_rows(x, gain):
    ms = jnp.mean(x * x, axis=-1, keepdims=True)
    return x * lax.rsqrt(ms + EPS) * gain


def _split_hi_lo(g):
    hi = g.astype(BF16)
    return hi, (g - hi.astype(F32)).astype(BF16)


def _hgrn_in_proj_kernel(layer_j, normalise, xn_ref, *refs):
    gn_ref = refs[0] if normalise else None
    w_ref, lbl_ref, o_ref = refs[-3:]
    w = lbl_ref.shape[1]

    logits = lbl_ref[...]
    ex = jnp.exp(logits - jnp.max(logits, axis=0, keepdims=True))
    sm = ex / jnp.sum(ex, axis=0, keepdims=True)
    lb = jnp.zeros((1, w), F32)
    for r in range(1, layer_j + 1):
        lb = lb + sm[r:r + 1, :]

    for r in range(0, xn_ref.shape[0], IN_PROJ_SUBTILE):
        rows = slice(r, r + IN_PROJ_SUBTILE)
        xn = xn_ref[rows, :]
        if normalise:
            xn = _rmsnorm_rows(xn, gn_ref[...]).astype(BF16)

        def proj(lo, hi):
            return jnp.dot(xn, w_ref[:, lo:hi], preferred_element_type=F32)

        o_ref[rows, 4 * w:5 * w] = proj(2 * w, 3 * w).astype(BF16)
        f = lb + (1.0 - lb) * _sigmoid(proj(w, 2 * w))
        hi, lo = _split_hi_lo(jnp.log(f))
        o_ref[rows, w:2 * w] = hi
        o_ref[rows, 2 * w:3 * w] = lo
        o_ref[rows, 3 * w:4 * w] = (1.0 - f).astype(BF16)
        q = proj(0, w)
        o_ref[rows, 0:w] = (q * _sigmoid(q)).astype(BF16)
        g = proj(3 * w, 4 * w)
        o_ref[rows, 5 * w:6 * w] = (g * _sigmoid(g)).astype(BF16)
        o_ref[rows, 6 * w:] = proj(4 * w, w_ref.shape[1]).astype(BF16)


def _hgrn_in_proj(xn, gain, w_stack, layer_j, lb_logits, tm):
    m, k = xn.shape
    n_in = w_stack.shape[2]
    n_a, w = lb_logits.shape
    n_out = n_in + 2 * w
    normalise = gain is not None
    norm_specs = [pl.BlockSpec((1, k), lambda i: (0, 0))] if normalise else []
    norm_args = [gain.reshape(1, k)] if normalise else []
    return pl.pallas_call(
        functools.partial(_hgrn_in_proj_kernel, layer_j, normalise),
        grid=(m // tm,),
        in_specs=[
            pl.BlockSpec((tm, k), lambda i: (i, 0)),
            *norm_specs,
            pl.BlockSpec((None, k, n_in), lambda i: (layer_j, 0, 0), pipeline_mode=RESIDENT),
            pl.BlockSpec((n_a, w), lambda i: (0, 0)),
        ],
        out_specs=pl.BlockSpec((tm, n_out), lambda i: (i, 0)),
        out_shape=jax.ShapeDtypeStruct((m, n_out), BF16),
        compiler_params=_params("parallel"),
        name="hgrn_in_proj",
    )(xn, *norm_args, w_stack, lb_logits)


def _gla_in_proj_kernel(dk_true, wv, wx, xn_ref, w_ref, wgk_ref, bgk_ref, o_ref):
    dkp = GLA_DK_PAD
    wk = wgk_ref.shape[1]
    wqk = GLA_HEADS * dk_true
    in_v, in_g, in_tail = 2 * wqk, 2 * wqk + wv, 2 * wqk + 2 * wv
    c_q = 2 * wv
    c_x = c_q + 2 * wk
    zero_pad = jnp.zeros((IN_PROJ_SUBTILE, dkp - dk_true), BF16)

    for r in range(0, xn_ref.shape[0], IN_PROJ_SUBTILE):
        rows = slice(r, r + IN_PROJ_SUBTILE)
        xn = xn_ref[rows, :]

        def proj(lo, hi):
            return jnp.dot(xn, w_ref[:, lo:hi], preferred_element_type=F32)

        def log_decay(head):
            cols = slice(head * dkp, (head + 1) * dkp)
            x = jnp.dot(low, wgk_ref[:, cols], preferred_element_type=F32) + bgk_ref[:, cols]
            log_a = (jnp.minimum(x, 0.0) - jnp.log1p(jnp.exp(-jnp.abs(x)))) / GLA_GATE_NORMALIZER
            hi, lo = _split_hi_lo(log_a)
            o_ref[rows, c_x + wx + head * dkp:c_x + wx + (head + 1) * dkp] = hi
            o_ref[rows, c_x + wx + wk + head * dkp:c_x + wx + wk + (head + 1) * dkp] = lo

        def store_heads(col0, values, scale):
            for hd in range(GLA_HEADS):
                piece = values[:, hd * dk_true:(hd + 1) * dk_true]
                o_ref[rows, col0 + hd * dkp:col0 + hd * dkp + dk_true] = (piece * scale).astype(BF16)
                o_ref[rows, col0 + hd * dkp + dk_true:col0 + (hd + 1) * dkp] = zero_pad

        tail = proj(in_tail, w_ref.shape[1])
        low = tail[:, :LANE].astype(BF16)
        o_ref[rows, c_x:c_x + wx] = tail[:, GLA_GATE_RANK:GLA_GATE_RANK + wx].astype(BF16)
        o_ref[rows, 0:wv] = proj(in_v, in_g).astype(BF16)
        log_decay(0)
        g = proj(in_g, in_tail)
        o_ref[rows, wv:c_q] = (g * _sigmoid(g)).astype(BF16)
        log_decay(1)
        qk = proj(0, in_v)
        store_heads(c_q, qk[:, :wqk], dk_true ** -0.5)
        log_decay(2)
        store_heads(c_q + wk, qk[:, wqk:], 1.0)
        log_decay(3)


def _gla_in_proj(xn, w_stack, layer_j, w_gk_pad, b_gk_pad, mix_width, xa_width, dk_true, tm):
    m, k = xn.shape
    n_in = w_stack.shape[2]
    wk = w_gk_pad.shape[1]
    n_out = 2 * mix_width + 2 * wk + xa_width + 2 * wk
    return pl.pallas_call(
        functools.partial(_gla_in_proj_kernel, dk_true, mix_width, xa_width),
        grid=(m // tm,),
        in_specs=[
            pl.BlockSpec((tm, k), lambda i: (i, 0)),
            pl.BlockSpec((None, k, n_in), lambda i: (layer_j, 0, 0), pipeline_mode=RESIDENT),
            pl.BlockSpec((LANE, wk), lambda i: (0, 0)),
            pl.BlockSpec((1, wk), lambda i: (0, 0)),
        ],
        out_specs=pl.BlockSpec((tm, n_out), lambda i: (i, 0)),
        out_shape=jax.ShapeDtypeStruct((m, n_out), BF16),
        compiler_params=_params("parallel"),
        name="gla_in_proj",
    )(xn, w_stack, w_gk_pad, b_gk_pad)


def _mem_kv_kernel(x_ref, g_ref, w_ref, o_ref):
    xn = _rmsnorm_rows(x_ref[...], g_ref[...]).astype(BF16)
    o_ref[...] = jnp.dot(xn, w_ref[...].astype(BF16), preferred_element_type=F32).astype(o_ref.dtype)


def _mem_kv(mem2, gain, w_stack):
    m, k = mem2.shape
    depth, _, n = w_stack.shape
    return pl.pallas_call(
        _mem_kv_kernel,
        grid=(depth,),
        in_specs=[
            pl.BlockSpec((m, k), lambda l: (0, 0)),
            pl.BlockSpec((1, k), lambda l: (0, 0)),
            pl.BlockSpec((None, k, n), lambda l: (l, 0, 0)),
        ],
        out_specs=pl.BlockSpec((None, m, n), lambda l: (l, 0, 0)),
        out_shape=jax.ShapeDtypeStruct((depth, m, n), BF16),
        compiler_params=_params("parallel"),
        name="mem_kv_proj",
    )(mem2, gain.reshape(1, k), w_stack)


def _ffn_up_kernel(xn_ref, wg_ref, wu_ref, o_ref):
    wg = wg_ref[...].astype(BF16)
    wu = wu_ref[...].astype(BF16)
    for r in range(0, xn_ref.shape[0], ROW_SUBTILE):
        xn = xn_ref[r:r + ROW_SUBTILE, :]
        gate = jnp.dot(xn, wg, preferred_element_type=F32)
        up = jnp.dot(xn, wu, preferred_element_type=F32)
        o_ref[r:r + ROW_SUBTILE, :] = (gate * _sigmoid(gate) * up).astype(o_ref.dtype)


def _ffn_up(xn, w_stack, layer, tm, tn):
    m, k = xn.shape
    d_ff = w_stack.shape[2] // 2
    n_blocks = d_ff // tn
    return pl.pallas_call(
        _ffn_up_kernel,
        grid=(m // tm, n_blocks),
        in_specs=[
            pl.BlockSpec((tm, k), lambda i, j: (i, 0)),
            pl.BlockSpec((None, k, tn), lambda i, j: (layer, 0, j)),
            pl.BlockSpec((None, k, tn), lambda i, j: (layer, 0, j + n_blocks)),
        ],
        out_specs=pl.BlockSpec((tm, tn), lambda i, j: (i, j)),
        out_shape=jax.ShapeDtypeStruct((m, d_ff), BF16),
        compiler_params=_params("parallel", "arbitrary"),
        name="ffn_up",
    )(xn, w_stack, w_stack)


def _down_proj_kernel(a_ref, w_ref, r_ref, g_ref, *out_refs):
    h = r_ref[...] + jnp.dot(a_ref[...], w_ref[...], preferred_element_type=F32)
    hn_ref = out_refs[-1]
    hn_ref[...] = _rmsnorm_rows(h, g_ref[...]).astype(hn_ref.dtype)
    if len(out_refs) == 2:
        out_refs[0][...] = h


def _down_proj(a, w_stack, layer, res, gain, final, tm):
    m, k = a.shape
    n = w_stack.shape[2]
    rows = pl.BlockSpec((tm, n), lambda i: (i, 0))
    if final:
        out_specs, out_shape = rows, jax.ShapeDtypeStruct((m, n), F32)
    else:
        out_specs = [rows, rows]
        out_shape = [jax.ShapeDtypeStruct((m, n), F32), jax.ShapeDtypeStruct((m, n), BF16)]
    return pl.pallas_call(
        _down_proj_kernel,
        grid=(m // tm,),
        in_specs=[
            pl.BlockSpec((tm, k), lambda i: (i, 0)),
            pl.BlockSpec((None, k, n), lambda i: (layer, 0, 0), pipeline_mode=RESIDENT),
            rows,
            pl.BlockSpec((1, n), lambda i: (0, 0)),
        ],
        out_specs=out_specs,
        out_shape=out_shape,
        compiler_params=_params("parallel"),
        name="down_proj",
    )(a, w_stack, res, gain.reshape(1, n))


def _out_proj_kernel(y_ref, xq_ref, kv_ref, w_ref, r_ref, g_ref, h_ref, hn_ref, wb_ref):
    @pl.when(pl.program_id(0) == 0)
    def _():
        wb_ref[...] = w_ref[...].astype(BF16)

    wy = y_ref.shape[1]
    d = xq_ref.shape[1] // XA_HEADS
    acc = jnp.dot(y_ref[...], wb_ref[:wy, :], preferred_element_type=F32)
    heads = []
    for hd in range(XA_HEADS):
        q = xq_ref[:, hd * d:(hd + 1) * d]
        k = kv_ref[:, hd * d:(hd + 1) * d]
        v = kv_ref[:, (XA_HEADS + hd) * d:(XA_HEADS + hd + 1) * d]
        s = lax.dot_general(q, k, NT_DIMS, preferred_element_type=F32) * (d ** -0.5)
        p = jnp.exp(s - jnp.max(s, axis=-1, keepdims=True))
        denom = jnp.sum(p, axis=-1, keepdims=True)
        o = jnp.dot(p.astype(BF16), v, preferred_element_type=F32)
        heads.append((o / denom).astype(BF16))
    acc += jnp.dot(jnp.concatenate(heads, axis=1), wb_ref[wy:, :], preferred_element_type=F32)
    h = r_ref[...] + acc
    h_ref[...] = h
    hn_ref[...] = _rmsnorm_rows(h, g_ref[...]).astype(hn_ref.dtype)


def _out_proj(y, proj, xq_block, kv_stack, w_stack, layer, res, gain, batch, tm):
    m, wy = y.shape
    _, k, n = w_stack.shape
    wx = k - wy
    n_mem, wkv = kv_stack.shape[1:]
    steps_per_batch = m // batch // tm
    return pl.pallas_call(
        _out_proj_kernel,
        grid=(m // tm,),
        in_specs=[
            pl.BlockSpec((tm, wy), lambda i: (i, 0)),
            pl.BlockSpec((tm, wx), lambda i: (i, xq_block)),
            pl.BlockSpec((None, n_mem, wkv), lambda i: (layer * batch + i // steps_per_batch, 0, 0)),
            pl.BlockSpec((None, k, n), lambda i: (layer, 0, 0), pipeline_mode=RESIDENT),
            pl.BlockSpec((tm, n), lambda i: (i, 0)),
            pl.BlockSpec((1, n), lambda i: (0, 0)),
        ],
        out_specs=[pl.BlockSpec((tm, n), lambda i: (i, 0)), pl.BlockSpec((tm, n), lambda i: (i, 0))],
        out_shape=[jax.ShapeDtypeStruct((m, n), F32), jax.ShapeDtypeStruct((m, n), BF16)],
        scratch_shapes=[pltpu.VMEM((k, n), BF16)],
        compiler_params=_params("arbitrary"),
        name="xattn_out_proj",
    )(y, proj, kv_stack, w_stack, res, gain.reshape(1, n))


def _chunk_masks():
    t = lax.broadcasted_iota(jnp.int32, (CHUNK, CHUNK), 0)
    s = lax.broadcasted_iota(jnp.int32, (CHUNK, CHUNK), 1)
    tril = (s <= t)
    return tril.astype(BF16), tril


def _chunk_cumsum(tril_b, hi, lo):
    dk = hi.shape[1]
    both = jnp.dot(tril_b, jnp.concatenate([hi, lo], axis=1), preferred_element_type=F32)
    return both[:, :dk] + both[:, dk:]


def _chunk_scores(q, k, vb, cum, sub):
    dk = q.shape[1]
    n_sub = CHUNK // sub
    blocks = [slice(sub * r, sub * (r + 1)) for r in range(n_sub)]
    mid = [cum[sub * j + sub // 2 - 1:sub * j + sub // 2, :] for j in range(n_sub)]
    last = cum[CHUNK - 1:CHUNK, :]
    zero = jnp.zeros((sub, dk), F32)

    q_slots, k_slots, k_last = [], [], []
    for j in range(n_sub):
        q_slots.append(jnp.concatenate(
            [zero if r < j else q[blocks[r], :] * jnp.exp(cum[blocks[r], :] - mid[j])
             for r in range(n_sub)], axis=0))
        k_mid = k[blocks[j], :] * jnp.exp(mid[j] - cum[blocks[j], :])
        k_slots.append(jnp.concatenate([k_mid if r == j else zero for r in range(n_sub)], axis=0))
        k_last.append(k_mid * jnp.exp(last - mid[j]))
    s_raw = lax.dot_general(jnp.concatenate(q_slots, axis=1).astype(BF16),
                            jnp.concatenate(k_slots, axis=1).astype(BF16),
                            NT_DIMS, preferred_element_type=F32)
    upd = lax.dot_general(vb, jnp.concatenate(k_last, axis=0).astype(BF16), TN_DIMS,
                          preferred_element_type=F32)
    q_dec = (q_slots[0] * jnp.exp(mid[0])).astype(BF16)
    return s_raw, upd, q_dec, jnp.exp(last), vb


def _chunk_readout(s_raw, tril, vb, q_dec, st):
    intra = jnp.dot(jnp.where(tril, s_raw, 0.0).astype(BF16), vb, preferred_element_type=F32)
    inter = lax.dot_general(q_dec, st.astype(BF16), NT_DIMS, preferred_element_type=F32)
    return intra + inter


def _gated_head_norm(o, gain, gate):
    ms = jnp.mean(o * o, axis=-1, keepdims=True)
    return o * lax.rsqrt(ms + EPS) * gain * gate


def _run_staggered(n_chunks, stages):
    n_groups = n_chunks // STAGGER_GROUP
    for slot in range(n_groups + len(stages) - 1):
        for depth in reversed(range(len(stages))):
            group = slot - depth
            if 0 <= group < n_groups:
                for c in range(group * STAGGER_GROUP, (group + 1) * STAGGER_GROUP):
                    stages[depth](c)


def _mixer_kernel(sub, q_ref, k_ref, v_ref, gate_ref, hi_ref, lo_ref, on_ref, y_ref, st_ref):
    @pl.when(pl.program_id(2) == 0)
    def _():
        st_ref[...] = jnp.zeros_like(st_ref)

    tril_b, tril = _chunk_masks()
    gain = on_ref[...]
    n_chunks = q_ref.shape[0] // CHUNK
    rows = [slice(c * CHUNK, (c + 1) * CHUNK) for c in range(n_chunks)]
    ctx = [{} for _ in range(n_chunks)]
    state = [st_ref[...]]

    def decay(c):
        ctx[c]["cum"] = _chunk_cumsum(tril_b, hi_ref[rows[c], :], lo_ref[rows[c], :])

    def scores(c):
        ctx[c]["parts"] = _chunk_scores(q_ref[rows[c], :].astype(F32), k_ref[rows[c], :].astype(F32),
                                        v_ref[rows[c], :], ctx[c].pop("cum"), sub)

    def readout(c):
        s_raw, upd, q_dec, e_last, vb = ctx[c].pop("parts")
        ctx[c]["o"] = _chunk_readout(s_raw, tril, vb, q_dec, state[0])
        state[0] = state[0] * e_last + upd

    def emit(c):
        gate = gate_ref[rows[c], :].astype(F32)
        y_ref[rows[c], :] = _gated_head_norm(ctx[c].pop("o"), gain, gate).astype(y_ref.dtype)

    _run_staggered(n_chunks, [decay, scores, readout, emit])
    st_ref[...] = state[0]


def _mixer(proj, onorm, cols, heads, dk, dv, sub, batch, t_blk, name):
    m = proj.shape[0]
    n_t = m // batch // t_blk

    def group(key, width):
        block0, rem = divmod(cols[key], width)
        assert rem == 0
        return pl.BlockSpec((t_blk, width), lambda b, h, t: (b * n_t + t, block0 + h))

    return pl.pallas_call(
        functools.partial(_mixer_kernel, sub),
        grid=(batch, heads, n_t),
        in_specs=[group("q", dk), group("k", dk), group("v", dv), group("gate", dv),
                  group("hi", dk), group("lo", dk),
                  pl.BlockSpec((1, dv), lambda b, h, t: (0, 0))],
        out_specs=pl.BlockSpec((t_blk, dv), lambda b, h, t: (b * n_t + t, h)),
        out_shape=jax.ShapeDtypeStruct((m, heads * dv), BF16),
        scratch_shapes=[pltpu.VMEM((dv, dk), F32)],
        compiler_params=_params("parallel", "parallel", "arbitrary"),
        name=name,
    )(proj, proj, proj, proj, proj, proj, onorm.reshape(1, dv))


def _gla_pad_decay_weights(w_gk, b_gk):
    heads, rank, dkp = GLA_HEADS, GLA_GATE_RANK, GLA_DK_PAD
    kw = w_gk.shape[1]
    dk = kw // heads

    def pad_heads(w):
        w = w.reshape(w.shape[0], heads, dk)
        return jnp.pad(w, ((0, 0), (0, 0), (0, dkp - dk))).reshape(w.shape[0], heads * dkp)

    w_gk_pad = jnp.pad(pad_heads(w_gk), ((0, LANE - rank), (0, 0))).astype(BF16)
    b_gk_pad = pad_heads(b_gk.reshape(1, kw)).astype(F32)
    return w_gk_pad, b_gk_pad, dk


def kernel(x, mem, norm_mix, norm_ffn, norm_mem, norm_final, hgrn_w_in, hgrn_lb_logits, hgrn_onorm,
           gla_w_in, gla_w_gk, gla_b_gk, gla_onorm, w_mem_kv, w_out, w_gate_up, w_down):
    batch, seq, d_model = x.shape
    n_mem = mem.shape[1]
    depth = norm_mix.shape[0]
    mix_width = hgrn_lb_logits.shape[1]
    xa_width = d_model - mix_width
    m = batch * seq

    h = x.reshape(m, d_model)
    kv = _mem_kv(mem.reshape(batch * n_mem, d_model), norm_mem, w_mem_kv)
    kv = kv.reshape(depth * batch, n_mem, 2 * xa_width)

    hgrn_w = hgrn_w_in.astype(BF16)
    gla_w = gla_w_in.astype(BF16)
    w_down_b = w_down.astype(BF16)

    for layer in range(depth):
        j = layer // 2
        if layer % 2 == 0:
            if layer == 0:
                proj = _hgrn_in_proj(h, norm_mix[0], hgrn_w, j, hgrn_lb_logits, tm=256)
            else:
                proj = _hgrn_in_proj(hn, None, hgrn_w, j, hgrn_lb_logits, tm=512)
            w = mix_width
            cols = dict(q=0, hi=w, lo=2 * w, k=3 * w, v=4 * w, gate=5 * w)
            y = _mixer(proj, hgrn_onorm[j], cols, mix_width // HGRN_HEAD_DIM, HGRN_HEAD_DIM,
                       HGRN_HEAD_DIM, HGRN_SUB, batch, t_blk=4096, name="hgrn2_mixer")
            xq_col = 6 * w
        else:
            w_gk_pad, b_gk_pad, dk = _gla_pad_decay_weights(gla_w_gk[j], gla_b_gk[j])
            proj = _gla_in_proj(hn, gla_w, j, w_gk_pad, b_gk_pad, mix_width, xa_width, dk, tm=512)
            wk = GLA_HEADS * GLA_DK_PAD
            xq_col = 2 * mix_width + 2 * wk
            cols = dict(v=0, gate=mix_width, q=2 * mix_width, k=2 * mix_width + wk,
                        hi=xq_col + xa_width, lo=xq_col + xa_width + wk)
            y = _mixer(proj, gla_onorm[j], cols, GLA_HEADS, GLA_DK_PAD, mix_width // GLA_HEADS,
                       GLA_SUB, batch, t_blk=4096, name="gla_mixer")
        assert xq_col % xa_width == 0
        h, hn = _out_proj(y, proj, xq_col // xa_width, kv, w_out, layer, h, norm_ffn[layer], batch,
                          tm=512)
        act = _ffn_up(hn, w_gate_up, layer, tm=2048, tn=512)
        if layer + 1 < depth:
            h, hn = _down_proj(act, w_down_b, layer, h, norm_mix[layer + 1], final=False, tm=256)
        else:
            out = _down_proj(act, w_down_b, layer, h, norm_final, final=True, tm=256)
    return out.reshape(batch, seq, d_model)
```

```python
import functools

import jax
import jax.numpy as jnp
from jax import lax
from jax.experimental import pallas as pl
from jax.experimental.pallas import tpu as pltpu

F32 = jnp.float32
BF16 = jnp.bfloat16

EPS = 1e-6
CHUNK = 64
STAGGER_GROUP = 8
ROW_SUBTILE = 256
IN_PROJ_SUBTILE = 256
CAST_STEPS_PER_SLICE = 2
BF16_SUBLANES = 16
HGRN_SUB = 32
GLA_SUB = 64
XA_HEADS = 4
HGRN_HEAD_DIM = 128
GLA_HEADS = 4
GLA_GATE_RANK = 16
GLA_GATE_NORMALIZER = 16.0
LANE = 128
GLA_DK_PAD = 256
V7X_VMEM_LIMIT = 56 * 1024 * 1024

NT_DIMS = (((1,), (1,)), ((), ()))
TN_DIMS = (((0,), (0,)), ((), ()))

RESIDENT = pl.Buffered(1)


def _params(*semantics):
    return pltpu.CompilerParams(dimension_semantics=semantics, vmem_limit_bytes=V7X_VMEM_LIMIT)


def _sigmoid(x):
    return 1.0 / (1.0 + jnp.exp(-x))


def _rmsnorm_rows(x, gain):
    ms = jnp.mean(x * x, axis=-1, keepdims=True)
    return x * lax.rsqrt(ms + EPS) * gain


def _split_hi_lo(g):
    hi = g.astype(BF16)
    return hi, (g - hi.astype(F32)).astype(BF16)


def _hgrn_in_proj_kernel(layer_j, normalise, xn_ref, *refs):
    gn_ref = refs[0] if normalise else None
    w_ref, lbl_ref, o_ref = refs[-3:]
    w = lbl_ref.shape[1]

    logits = lbl_ref[...]
    ex = jnp.exp(logits - jnp.max(logits, axis=0, keepdims=True))
    sm = ex / jnp.sum(ex, axis=0, keepdims=True)
    lb = jnp.zeros((1, w), F32)
    for r in range(1, layer_j + 1):
        lb = lb + sm[r:r + 1, :]

    for r in range(0, xn_ref.shape[0], IN_PROJ_SUBTILE):
        rows = slice(r, r + IN_PROJ_SUBTILE)
        xn = xn_ref[rows, :]
        if normalise:
            xn = _rmsnorm_rows(xn, gn_ref[...]).astype(BF16)

        def proj(lo, hi):
            return jnp.dot(xn, w_ref[:, lo:hi], preferred_element_type=F32)

        o_ref[rows, 4 * w:5 * w] = proj(2 * w, 3 * w).astype(BF16)
        f = lb + (1.0 - lb) * _sigmoid(proj(w, 2 * w))
        hi, lo = _split_hi_lo(jnp.log(f))
        o_ref[rows, w:2 * w] = hi
        o_ref[rows, 2 * w:3 * w] = lo
        o_ref[rows, 3 * w:4 * w] = (1.0 - f).astype(BF16)
        q = proj(0, w)
        o_ref[rows, 0:w] = (q * _sigmoid(q)).astype(BF16)
        g = proj(3 * w, 4 * w)
        o_ref[rows, 5 * w:6 * w] = (g * _sigmoid(g)).astype(BF16)
        o_ref[rows, 6 * w:] = proj(4 * w, w_ref.shape[1]).astype(BF16)


def _hgrn_in_proj(xn, gain, w, layer_j, lb_logits, tm):
    m, k = xn.shape
    n_in = w.shape[1]
    n_a, mix_width = lb_logits.shape
    n_out = n_in + 2 * mix_width
    normalise = gain is not None
    norm_specs = [pl.BlockSpec((1, k), lambda i: (0, 0))] if normalise else []
    norm_args = [gain.reshape(1, k)] if normalise else []
    return pl.pallas_call(
        functools.partial(_hgrn_in_proj_kernel, layer_j, normalise),
        grid=(m // tm,),
        in_specs=[
            pl.BlockSpec((tm, k), lambda i: (i, 0)),
            *norm_specs,
            pl.BlockSpec((k, n_in), lambda i: (0, 0), pipeline_mode=RESIDENT),
            pl.BlockSpec((n_a, mix_width), lambda i: (0, 0)),
        ],
        out_specs=pl.BlockSpec((tm, n_out), lambda i: (i, 0)),
        out_shape=jax.ShapeDtypeStruct((m, n_out), BF16),
        compiler_params=_params("parallel"),
        name="hgrn_in_proj",
    )(xn, *norm_args, w, lb_logits)


def _gla_in_proj_kernel(dk_true, wv, wx, xn_ref, w_ref, wgk_ref, bgk_ref, o_ref):
    dkp = GLA_DK_PAD
    wk = wgk_ref.shape[1]
    wqk = GLA_HEADS * dk_true
    in_v, in_g, in_tail = 2 * wqk, 2 * wqk + wv, 2 * wqk + 2 * wv
    c_q = 2 * wv
    c_x = c_q + 2 * wk
    zero_pad = jnp.zeros((IN_PROJ_SUBTILE, dkp - dk_true), BF16)

    for r in range(0, xn_ref.shape[0], IN_PROJ_SUBTILE):
        rows = slice(r, r + IN_PROJ_SUBTILE)
        xn = xn_ref[rows, :]

        def proj(lo, hi):
            return jnp.dot(xn, w_ref[:, lo:hi], preferred_element_type=F32)

        def log_decay(head):
            cols = slice(head * dkp, (head + 1) * dkp)
            x = jnp.dot(low, wgk_ref[:, cols], preferred_element_type=F32) + bgk_ref[:, cols]
            log_a = (jnp.minimum(x, 0.0) - jnp.log1p(jnp.exp(-jnp.abs(x)))) / GLA_GATE_NORMALIZER
            hi, lo = _split_hi_lo(log_a)
            o_ref[rows, c_x + wx + head * dkp:c_x + wx + (head + 1) * dkp] = hi
            o_ref[rows, c_x + wx + wk + head * dkp:c_x + wx + wk + (head + 1) * dkp] = lo

        def store_heads(col0, values, scale):
            for hd in range(GLA_HEADS):
                piece = values[:, hd * dk_true:(hd + 1) * dk_true]
                o_ref[rows, col0 + hd * dkp:col0 + hd * dkp + dk_true] = (piece * scale).astype(BF16)
                o_ref[rows, col0 + hd * dkp + dk_true:col0 + (hd + 1) * dkp] = zero_pad

        tail = proj(in_tail, w_ref.shape[1])
        low = tail[:, :LANE].astype(BF16)
        o_ref[rows, c_x:c_x + wx] = tail[:, GLA_GATE_RANK:GLA_GATE_RANK + wx].astype(BF16)
        o_ref[rows, 0:wv] = proj(in_v, in_g).astype(BF16)
        log_decay(0)
        g = proj(in_g, in_tail)
        o_ref[rows, wv:c_q] = (g * _sigmoid(g)).astype(BF16)
        log_decay(1)
        qk = proj(0, in_v)
        store_heads(c_q, qk[:, :wqk], dk_true ** -0.5)
        log_decay(2)
        store_heads(c_q + wk, qk[:, wqk:], 1.0)
        log_decay(3)


def _gla_in_proj(xn, w, w_gk_pad, b_gk_pad, mix_width, xa_width, dk_true, tm):
    m, k = xn.shape
    n_in = w.shape[1]
    wk = w_gk_pad.shape[1]
    n_out = 2 * mix_width + 2 * wk + xa_width + 2 * wk
    return pl.pallas_call(
        functools.partial(_gla_in_proj_kernel, dk_true, mix_width, xa_width),
        grid=(m // tm,),
        in_specs=[
            pl.BlockSpec((tm, k), lambda i: (i, 0)),
            pl.BlockSpec((k, n_in), lambda i: (0, 0), pipeline_mode=RESIDENT),
            pl.BlockSpec((LANE, wk), lambda i: (0, 0)),
            pl.BlockSpec((1, wk), lambda i: (0, 0)),
        ],
        out_specs=pl.BlockSpec((tm, n_out), lambda i: (i, 0)),
        out_shape=jax.ShapeDtypeStruct((m, n_out), BF16),
        compiler_params=_params("parallel"),
        name="gla_in_proj",
    )(xn, w, w_gk_pad, b_gk_pad)


def _mem_kv_kernel(x_ref, g_ref, w_ref, o_ref):
    xn = _rmsnorm_rows(x_ref[...], g_ref[...]).astype(BF16)
    o_ref[...] = jnp.dot(xn, w_ref[...].astype(BF16), preferred_element_type=F32).astype(o_ref.dtype)


def _mem_kv(mem2, gain, w_stack):
    m, k = mem2.shape
    depth, _, n = w_stack.shape
    return pl.pallas_call(
        _mem_kv_kernel,
        grid=(depth,),
        in_specs=[
            pl.BlockSpec((m, k), lambda l: (0, 0)),
            pl.BlockSpec((1, k), lambda l: (0, 0)),
            pl.BlockSpec((None, k, n), lambda l: (l, 0, 0)),
        ],
        out_specs=pl.BlockSpec((None, m, n), lambda l: (l, 0, 0)),
        out_shape=jax.ShapeDtypeStruct((depth, m, n), BF16),
        compiler_params=_params("parallel"),
        name="mem_kv_proj",
    )(mem2, gain.reshape(1, k), w_stack)


def _ffn_up_kernel(xn_ref, wg_ref, wu_ref, o_ref):
    wg = wg_ref[...].astype(BF16)
    wu = wu_ref[...].astype(BF16)
    for r in range(0, xn_ref.shape[0], ROW_SUBTILE):
        xn = xn_ref[r:r + ROW_SUBTILE, :]
        gate = jnp.dot(xn, wg, preferred_element_type=F32)
        up = jnp.dot(xn, wu, preferred_element_type=F32)
        o_ref[r:r + ROW_SUBTILE, :] = (gate * _sigmoid(gate) * up).astype(o_ref.dtype)


def _ffn_up(xn, w_stack, layer, tm, tn):
    m, k = xn.shape
    d_ff = w_stack.shape[2] // 2
    n_blocks = d_ff // tn
    return pl.pallas_call(
        _ffn_up_kernel,
        grid=(m // tm, n_blocks),
        in_specs=[
            pl.BlockSpec((tm, k), lambda i, j: (i, 0)),
            pl.BlockSpec((None, k, tn), lambda i, j: (layer, 0, j)),
            pl.BlockSpec((None, k, tn), lambda i, j: (layer, 0, j + n_blocks)),
        ],
        out_specs=pl.BlockSpec((tm, tn), lambda i, j: (i, j)),
        out_shape=jax.ShapeDtypeStruct((m, d_ff), BF16),
        compiler_params=_params("parallel", "arbitrary"),
        name="ffn_up",
    )(xn, w_stack, w_stack)


def _down_proj_kernel(n_casts, a_ref, w_ref, r_ref, g_ref, *refs):
    srcs, outs = refs[:n_casts], refs[n_casts:]
    out_refs, dsts = outs[:len(outs) - n_casts], outs[len(outs) - n_casts:]
    h = r_ref[...] + jnp.dot(a_ref[...], w_ref[...], preferred_element_type=F32)
    for src, dst in zip(srcs, dsts):
        dst[...] = src[...].astype(BF16)
    hn_ref = out_refs[-1]
    hn_ref[...] = _rmsnorm_rows(h, g_ref[...]).astype(hn_ref.dtype)
    if len(out_refs) == 2:
        out_refs[0][...] = h


def _down_proj(a, w, res, gain, final, tm, casts=()):
    m, k = a.shape
    n = w.shape[1]
    steps = m // tm
    rows = pl.BlockSpec((tm, n), lambda i: (i, 0))
    if final:
        out_specs, out_shape = [rows], [jax.ShapeDtypeStruct((m, n), F32)]
    else:
        out_specs = [rows, rows]
        out_shape = [jax.ShapeDtypeStruct((m, n), F32), jax.ShapeDtypeStruct((m, n), BF16)]
    cast_specs, cast_args = [], []
    for stack, idx in casts:
        kc, nc = stack.shape[1:]
        slice_rows = kc * CAST_STEPS_PER_SLICE // steps
        assert slice_rows * steps == kc * CAST_STEPS_PER_SLICE and slice_rows % BF16_SUBLANES == 0
        cast_specs.append(pl.BlockSpec(
            (None, slice_rows, nc), lambda i, idx=idx: (idx, i // CAST_STEPS_PER_SLICE, 0)))
        cast_args.append(stack)
        out_specs.append(pl.BlockSpec((slice_rows, nc), lambda i: (i // CAST_STEPS_PER_SLICE, 0)))
        out_shape.append(jax.ShapeDtypeStruct((kc, nc), BF16))
    outs = pl.pallas_call(
        functools.partial(_down_proj_kernel, len(casts)),
        grid=(steps,),
        in_specs=[
            pl.BlockSpec((tm, k), lambda i: (i, 0)),
            pl.BlockSpec((k, n), lambda i: (0, 0), pipeline_mode=RESIDENT),
            rows,
            pl.BlockSpec((1, n), lambda i: (0, 0)),
            *cast_specs,
        ],
        out_specs=out_specs,
        out_shape=out_shape,
        compiler_params=_params("arbitrary"),
        name="down_proj",
    )(a, w, res, gain.reshape(1, n), *cast_args)
    n_main = len(outs) - len(casts)
    return outs[:n_main], outs[n_main:]


def _out_proj_kernel(y_ref, xq_ref, kv_ref, w_ref, r_ref, g_ref, h_ref, hn_ref, wb_ref):
    @pl.when(pl.program_id(0) == 0)
    def _():
        wb_ref[...] = w_ref[...].astype(BF16)

    wy = y_ref.shape[1]
    d = xq_ref.shape[1] // XA_HEADS
    acc = jnp.dot(y_ref[...], wb_ref[:wy, :], preferred_element_type=F32)
    heads = []
    for hd in range(XA_HEADS):
        q = xq_ref[:, hd * d:(hd + 1) * d]
        k = kv_ref[:, hd * d:(hd + 1) * d]
        v = kv_ref[:, (XA_HEADS + hd) * d:(XA_HEADS + hd + 1) * d]
        s = lax.dot_general(q, k, NT_DIMS, preferred_element_type=F32) * (d ** -0.5)
        p = jnp.exp(s - jnp.max(s, axis=-1, keepdims=True))
        denom = jnp.sum(p, axis=-1, keepdims=True)
        o = jnp.dot(p.astype(BF16), v, preferred_element_type=F32)
        heads.append((o / denom).astype(BF16))
    acc += jnp.dot(jnp.concatenate(heads, axis=1), wb_ref[wy:, :], preferred_element_type=F32)
    h = r_ref[...] + acc
    h_ref[...] = h
    hn_ref[...] = _rmsnorm_rows(h, g_ref[...]).astype(hn_ref.dtype)


def _out_proj(y, proj, xq_block, kv_stack, w_stack, layer, res, gain, batch, tm):
    m, wy = y.shape
    _, k, n = w_stack.shape
    wx = k - wy
    n_mem, wkv = kv_stack.shape[1:]
    steps_per_batch = m // batch // tm
    return pl.pallas_call(
        _out_proj_kernel,
        grid=(m // tm,),
        in_specs=[
            pl.BlockSpec((tm, wy), lambda i: (i, 0)),
            pl.BlockSpec((tm, wx), lambda i: (i, xq_block)),
            pl.BlockSpec((None, n_mem, wkv), lambda i: (layer * batch + i // steps_per_batch, 0, 0)),
            pl.BlockSpec((None, k, n), lambda i: (layer, 0, 0), pipeline_mode=RESIDENT),
            pl.BlockSpec((tm, n), lambda i: (i, 0)),
            pl.BlockSpec((1, n), lambda i: (0, 0)),
        ],
        out_specs=[pl.BlockSpec((tm, n), lambda i: (i, 0)), pl.BlockSpec((tm, n), lambda i: (i, 0))],
        out_shape=[jax.ShapeDtypeStruct((m, n), F32), jax.ShapeDtypeStruct((m, n), BF16)],
        scratch_shapes=[pltpu.VMEM((k, n), BF16)],
        compiler_params=_params("arbitrary"),
        name="xattn_out_proj",
    )(y, proj, kv_stack, w_stack, res, gain.reshape(1, n))


def _chunk_masks():
    t = lax.broadcasted_iota(jnp.int32, (CHUNK, CHUNK), 0)
    s = lax.broadcasted_iota(jnp.int32, (CHUNK, CHUNK), 1)
    tril = (s <= t)
    return tril.astype(BF16), tril


def _chunk_cumsum(tril_b, hi, lo):
    dk = hi.shape[1]
    both = jnp.dot(tril_b, jnp.concatenate([hi, lo], axis=1), preferred_element_type=F32)
    return both[:, :dk] + both[:, dk:]


def _chunk_scores(q, k, vb, cum, sub):
    dk = q.shape[1]
    n_sub = CHUNK // sub
    blocks = [slice(sub * r, sub * (r + 1)) for r in range(n_sub)]
    mid = [cum[sub * j + sub // 2 - 1:sub * j + sub // 2, :] for j in range(n_sub)]
    last = cum[CHUNK - 1:CHUNK, :]
    zero = jnp.zeros((sub, dk), F32)

    q_slots, k_slots, k_last = [], [], []
    for j in range(n_sub):
        q_slots.append(jnp.concatenate(
            [zero if r < j else q[blocks[r], :] * jnp.exp(cum[blocks[r], :] - mid[j])
             for r in range(n_sub)], axis=0))
        k_mid = k[blocks[j], :] * jnp.exp(mid[j] - cum[blocks[j], :])
        k_slots.append(jnp.concatenate([k_mid if r == j else zero for r in range(n_sub)], axis=0))
        k_last.append(k_mid * jnp.exp(last - mid[j]))
    s_raw = lax.dot_general(jnp.concatenate(q_slots, axis=1).astype(BF16),
                            jnp.concatenate(k_slots, axis=1).astype(BF16),
                            NT_DIMS, preferred_element_type=F32)
    upd = lax.dot_general(vb, jnp.concatenate(k_last, axis=0).astype(BF16), TN_DIMS,
                          preferred_element_type=F32)
    q_dec = (q_slots[0] * jnp.exp(mid[0])).astype(BF16)
    return s_raw, upd, q_dec, jnp.exp(last), vb


def _chunk_readout(s_raw, tril, vb, q_dec, st):
    intra = jnp.dot(jnp.where(tril, s_raw, 0.0).astype(BF16), vb, preferred_element_type=F32)
    inter = lax.dot_general(q_dec, st.astype(BF16), NT_DIMS, preferred_element_type=F32)
    return intra + inter


def _gated_head_norm(o, gain, gate):
    ms = jnp.mean(o * o, axis=-1, keepdims=True)
    return o * lax.rsqrt(ms + EPS) * gain * gate


def _run_staggered(n_chunks, stages):
    n_groups = n_chunks // STAGGER_GROUP
    for slot in range(n_groups + len(stages) - 1):
        for depth in reversed(range(len(stages))):
            group = slot - depth
            if 0 <= group < n_groups:
                for c in range(group * STAGGER_GROUP, (group + 1) * STAGGER_GROUP):
                    stages[depth](c)


def _mixer_kernel(sub, q_ref, k_ref, v_ref, gate_ref, hi_ref, lo_ref, on_ref, y_ref, st_ref):
    @pl.when(pl.program_id(2) == 0)
    def _():
        st_ref[...] = jnp.zeros_like(st_ref)

    tril_b, tril = _chunk_masks()
    gain = on_ref[...]
    n_chunks = q_ref.shape[0] // CHUNK
    rows = [slice(c * CHUNK, (c + 1) * CHUNK) for c in range(n_chunks)]
    ctx = [{} for _ in range(n_chunks)]
    state = [st_ref[...]]

    def decay(c):
        ctx[c]["cum"] = _chunk_cumsum(tril_b, hi_ref[rows[c], :], lo_ref[rows[c], :])

    def scores(c):
        ctx[c]["parts"] = _chunk_scores(q_ref[rows[c], :].astype(F32), k_ref[rows[c], :].astype(F32),
                                        v_ref[rows[c], :], ctx[c].pop("cum"), sub)

    def readout(c):
        s_raw, upd, q_dec, e_last, vb = ctx[c].pop("parts")
        ctx[c]["o"] = _chunk_readout(s_raw, tril, vb, q_dec, state[0])
        state[0] = state[0] * e_last + upd

    def emit(c):
        gate = gate_ref[rows[c], :].astype(F32)
        y_ref[rows[c], :] = _gated_head_norm(ctx[c].pop("o"), gain, gate).astype(y_ref.dtype)

    _run_staggered(n_chunks, [decay, scores, readout, emit])
    st_ref[...] = state[0]


def _mixer(proj, onorm, cols, heads, dk, dv, sub, batch, t_blk, name):
    m = proj.shape[0]
    n_t = m // batch // t_blk

    def group(key, width):
        block0, rem = divmod(cols[key], width)
        assert rem == 0
        return pl.BlockSpec((t_blk, width), lambda b, h, t: (b * n_t + t, block0 + h))

    return pl.pallas_call(
        functools.partial(_mixer_kernel, sub),
        grid=(batch, heads, n_t),
        in_specs=[group("q", dk), group("k", dk), group("v", dv), group("gate", dv),
                  group("hi", dk), group("lo", dk),
                  pl.BlockSpec((1, dv), lambda b, h, t: (0, 0))],
        out_specs=pl.BlockSpec((t_blk, dv), lambda b, h, t: (b * n_t + t, h)),
        out_shape=jax.ShapeDtypeStruct((m, heads * dv), BF16),
        scratch_shapes=[pltpu.VMEM((dv, dk), F32)],
        compiler_params=_params("parallel", "parallel", "arbitrary"),
        name=name,
    )(proj, proj, proj, proj, proj, proj, onorm.reshape(1, dv))


def _gla_pad_decay_weights(w_gk, b_gk):
    heads, rank, dkp = GLA_HEADS, GLA_GATE_RANK, GLA_DK_PAD
    kw = w_gk.shape[1]
    dk = kw // heads

    def pad_heads(w):
        w = w.reshape(w.shape[0], heads, dk)
        return jnp.pad(w, ((0, 0), (0, 0), (0, dkp - dk))).reshape(w.shape[0], heads * dkp)

    w_gk_pad = jnp.pad(pad_heads(w_gk), ((0, LANE - rank), (0, 0))).astype(BF16)
    b_gk_pad = pad_heads(b_gk.reshape(1, kw)).astype(F32)
    return w_gk_pad, b_gk_pad, dk


def kernel(x, mem, norm_mix, norm_ffn, norm_mem, norm_final, hgrn_w_in, hgrn_lb_logits, hgrn_onorm,
           gla_w_in, gla_w_gk, gla_b_gk, gla_onorm, w_mem_kv, w_out, w_gate_up, w_down):
    batch, seq, d_model = x.shape
    n_mem = mem.shape[1]
    depth = norm_mix.shape[0]
    mix_width = hgrn_lb_logits.shape[1]
    xa_width = d_model - mix_width
    m = batch * seq

    h = x.reshape(m, d_model)
    kv = _mem_kv(mem.reshape(batch * n_mem, d_model), norm_mem, w_mem_kv)
    kv = kv.reshape(depth * batch, n_mem, 2 * xa_width)

    w_in_b = hgrn_w_in[0].astype(BF16)
    w_down_b = w_down[0].astype(BF16)

    for layer in range(depth):
        j = layer // 2
        if layer % 2 == 0:
            if layer == 0:
                proj = _hgrn_in_proj(h, norm_mix[0], w_in_b, j, hgrn_lb_logits, tm=256)
            else:
                proj = _hgrn_in_proj(hn, None, w_in_b, j, hgrn_lb_logits, tm=512)
            w = mix_width
            cols = dict(q=0, hi=w, lo=2 * w, k=3 * w, v=4 * w, gate=5 * w)
            y = _mixer(proj, hgrn_onorm[j], cols, mix_width // HGRN_HEAD_DIM, HGRN_HEAD_DIM,
                       HGRN_HEAD_DIM, HGRN_SUB, batch, t_blk=4096, name="hgrn2_mixer")
            xq_col = 6 * w
        else:
            w_gk_pad, b_gk_pad, dk = _gla_pad_decay_weights(gla_w_gk[j], gla_b_gk[j])
            proj = _gla_in_proj(hn, w_in_b, w_gk_pad, b_gk_pad, mix_width, xa_width, dk, tm=512)
            wk = GLA_HEADS * GLA_DK_PAD
            xq_col = 2 * mix_width + 2 * wk
            cols = dict(v=0, gate=mix_width, q=2 * mix_width, k=2 * mix_width + wk,
                        hi=xq_col + xa_width, lo=xq_col + xa_width + wk)
            y = _mixer(proj, gla_onorm[j], cols, GLA_HEADS, GLA_DK_PAD, mix_width // GLA_HEADS,
                       GLA_SUB, batch, t_blk=4096, name="gla_mixer")
        assert xq_col % xa_width == 0
        h, hn = _out_proj(y, proj, xq_col // xa_width, kv, w_out, layer, h, norm_ffn[layer], batch,
                          tm=512)
        act = _ffn_up(hn, w_gate_up, layer, tm=2048, tn=512)
        if layer + 1 < depth:
            nxt = layer + 1
            next_w_in = (hgrn_w_in if nxt % 2 == 0 else gla_w_in, nxt // 2)
            (h, hn), (w_down_b, w_in_b) = _down_proj(
                act, w_down_b, h, norm_mix[nxt], final=False, tm=256,
                casts=((w_down, nxt), next_w_in))
        else:
            (out,), _ = _down_proj(act, w_down_b, h, norm_final, final=True, tm=256)
    return out.reshape(batch, seq, d_model)
```

```python
import functools

import jax
import jax.numpy as jnp
from jax import lax
from jax.experimental import pallas as pl
from jax.experimental.pallas import tpu as pltpu

F32 = jnp.float32
BF16 = jnp.bfloat16

EPS = 1e-6
CHUNK = 64
STAGGER_GROUP = 8
ROW_SUBTILE = 256
IN_PROJ_SUBTILE = 256
CAST_STEPS_PER_SLICE = 2
BF16_SUBLANES = 16
HGRN_SUB = 32
GLA_SUB = 64
XA_HEADS = 4
HGRN_HEAD_DIM = 128
GLA_HEADS = 4
GLA_GATE_RANK = 16
GLA_GATE_NORMALIZER = 16.0
LANE = 128
GLA_DK_PAD = 256
V7X_VMEM_LIMIT = 56 * 1024 * 1024

NT_DIMS = (((1,), (1,)), ((), ()))
TN_DIMS = (((0,), (0,)), ((), ()))

RESIDENT = pl.Buffered(1)


def _params(*semantics):
    return pltpu.CompilerParams(dimension_semantics=semantics, vmem_limit_bytes=V7X_VMEM_LIMIT)


def _sigmoid(x):
    return 1.0 / (1.0 + jnp.exp(-x))


def _rmsnorm_rows(x, gain):
    ms = jnp.mean(x * x, axis=-1, keepdims=True)
    return x * lax.rsqrt(ms + EPS) * gain


def _split_hi_lo(g):
    hi = g.astype(BF16)
    return hi, (g - hi.astype(F32)).astype(BF16)


def _hgrn_in_proj_kernel(layer_j, normalise, xn_ref, *refs):
    gn_ref = refs[0] if normalise else None
    w_ref, lbl_ref, o_ref = refs[-3:]
    w = lbl_ref.shape[1]

    logits = lbl_ref[...]
    ex = jnp.exp(logits - jnp.max(logits, axis=0, keepdims=True))
    sm = ex / jnp.sum(ex, axis=0, keepdims=True)
    lb = jnp.zeros((1, w), F32)
    for r in range(1, layer_j + 1):
        lb = lb + sm[r:r + 1, :]

    for r in range(0, xn_ref.shape[0], IN_PROJ_SUBTILE):
        rows = slice(r, r + IN_PROJ_SUBTILE)
        xn = xn_ref[rows, :]
        if normalise:
            xn = _rmsnorm_rows(xn, gn_ref[...]).astype(BF16)

        def proj(lo, hi):
            return jnp.dot(xn, w_ref[:, lo:hi], preferred_element_type=F32)

        o_ref[rows, 4 * w:5 * w] = proj(2 * w, 3 * w).astype(BF16)
        f = lb + (1.0 - lb) * _sigmoid(proj(w, 2 * w))
        hi, lo = _split_hi_lo(jnp.log(f))
        o_ref[rows, w:2 * w] = hi
        o_ref[rows, 2 * w:3 * w] = lo
        o_ref[rows, 3 * w:4 * w] = (1.0 - f).astype(BF16)
        q = proj(0, w)
        o_ref[rows, 0:w] = (q * _sigmoid(q)).astype(BF16)
        g = proj(3 * w, 4 * w)
        o_ref[rows, 5 * w:6 * w] = (g * _sigmoid(g)).astype(BF16)
        o_ref[rows, 6 * w:] = proj(4 * w, w_ref.shape[1]).astype(BF16)


def _hgrn_in_proj(xn, gain, w, layer_j, lb_logits, tm):
    m, k = xn.shape
    n_in = w.shape[1]
    n_a, mix_width = lb_logits.shape
    n_out = n_in + 2 * mix_width
    normalise = gain is not None
    norm_specs = [pl.BlockSpec((1, k), lambda i: (0, 0))] if normalise else []
    norm_args = [gain.reshape(1, k)] if normalise else []
    return pl.pallas_call(
        functools.partial(_hgrn_in_proj_kernel, layer_j, normalise),
        grid=(m // tm,),
        in_specs=[
            pl.BlockSpec((tm, k), lambda i: (i, 0)),
            *norm_specs,
            pl.BlockSpec((k, n_in), lambda i: (0, 0), pipeline_mode=RESIDENT),
            pl.BlockSpec((n_a, mix_width), lambda i: (0, 0)),
        ],
        out_specs=pl.BlockSpec((tm, n_out), lambda i: (i, 0)),
        out_shape=jax.ShapeDtypeStruct((m, n_out), BF16),
        compiler_params=_params("parallel"),
        name="hgrn_in_proj",
    )(xn, *norm_args, w, lb_logits)


def _gla_in_proj_kernel(dk_true, wv, wx, xn_ref, w_ref, wgk_ref, bgk_ref, o_ref):
    dkp = GLA_DK_PAD
    wk = wgk_ref.shape[1]
    wqk = GLA_HEADS * dk_true
    in_v, in_g, in_tail = 2 * wqk, 2 * wqk + wv, 2 * wqk + 2 * wv
    c_q = 2 * wv
    c_x = c_q + 2 * wk
    zero_pad = jnp.zeros((IN_PROJ_SUBTILE, dkp - dk_true), BF16)

    for r in range(0, xn_ref.shape[0], IN_PROJ_SUBTILE):
        rows = slice(r, r + IN_PROJ_SUBTILE)
        xn = xn_ref[rows, :]

        def proj(lo, hi):
            return jnp.dot(xn, w_ref[:, lo:hi], preferred_element_type=F32)

        def log_decay(head):
            cols = slice(head * dkp, (head + 1) * dkp)
            x = jnp.dot(low, wgk_ref[:, cols], preferred_element_type=F32) + bgk_ref[:, cols]
            log_a = (jnp.minimum(x, 0.0) - jnp.log1p(jnp.exp(-jnp.abs(x)))) / GLA_GATE_NORMALIZER
            hi, lo = _split_hi_lo(log_a)
            o_ref[rows, c_x + wx + head * dkp:c_x + wx + (head + 1) * dkp] = hi
            o_ref[rows, c_x + wx + wk + head * dkp:c_x + wx + wk + (head + 1) * dkp] = lo

        def store_heads(col0, values, scale):
            for hd in range(GLA_HEADS):
                piece = values[:, hd * dk_true:(hd + 1) * dk_true]
                o_ref[rows, col0 + hd * dkp:col0 + hd * dkp + dk_true] = (piece * scale).astype(BF16)
                o_ref[rows, col0 + hd * dkp + dk_true:col0 + (hd + 1) * dkp] = zero_pad

        tail = proj(in_tail, w_ref.shape[1])
        low = tail[:, :LANE].astype(BF16)
        o_ref[rows, c_x:c_x + wx] = tail[:, GLA_GATE_RANK:GLA_GATE_RANK + wx].astype(BF16)
        o_ref[rows, 0:wv] = proj(in_v, in_g).astype(BF16)
        log_decay(0)
        g = proj(in_g, in_tail)
        o_ref[rows, wv:c_q] = (g * _sigmoid(g)).astype(BF16)
        log_decay(1)
        qk = proj(0, in_v)
        store_heads(c_q, qk[:, :wqk], dk_true ** -0.5)
        log_decay(2)
        store_heads(c_q + wk, qk[:, wqk:], 1.0)
        log_decay(3)


def _gla_in_proj(xn, w_stack, layer_j, w_gk_pad, b_gk_pad, mix_width, xa_width, dk_true, tm):
    m, k = xn.shape
    n_in = w_stack.shape[2]
    wk = w_gk_pad.shape[1]
    n_out = 2 * mix_width + 2 * wk + xa_width + 2 * wk
    return pl.pallas_call(
        functools.partial(_gla_in_proj_kernel, dk_true, mix_width, xa_width),
        grid=(m // tm,),
        in_specs=[
            pl.BlockSpec((tm, k), lambda i: (i, 0)),
            pl.BlockSpec((None, k, n_in), lambda i: (layer_j, 0, 0), pipeline_mode=RESIDENT),
            pl.BlockSpec((LANE, wk), lambda i: (0, 0)),
            pl.BlockSpec((1, wk), lambda i: (0, 0)),
        ],
        out_specs=pl.BlockSpec((tm, n_out), lambda i: (i, 0)),
        out_shape=jax.ShapeDtypeStruct((m, n_out), BF16),
        compiler_params=_params("parallel"),
        name="gla_in_proj",
    )(xn, w_stack, w_gk_pad, b_gk_pad)


def _mem_kv_kernel(x_ref, g_ref, w_ref, o_ref):
    xn = _rmsnorm_rows(x_ref[...], g_ref[...]).astype(BF16)
    o_ref[...] = jnp.dot(xn, w_ref[...].astype(BF16), preferred_element_type=F32).astype(o_ref.dtype)


def _mem_kv(mem2, gain, w_stack):
    m, k = mem2.shape
    depth, _, n = w_stack.shape
    return pl.pallas_call(
        _mem_kv_kernel,
        grid=(depth,),
        in_specs=[
            pl.BlockSpec((m, k), lambda l: (0, 0)),
            pl.BlockSpec((1, k), lambda l: (0, 0)),
            pl.BlockSpec((None, k, n), lambda l: (l, 0, 0)),
        ],
        out_specs=pl.BlockSpec((None, m, n), lambda l: (l, 0, 0)),
        out_shape=jax.ShapeDtypeStruct((depth, m, n), BF16),
        compiler_params=_params("parallel"),
        name="mem_kv_proj",
    )(mem2, gain.reshape(1, k), w_stack)


def _ffn_up_kernel(xn_ref, wg_ref, wu_ref, o_ref):
    wg = wg_ref[...].astype(BF16)
    wu = wu_ref[...].astype(BF16)
    for r in range(0, xn_ref.shape[0], ROW_SUBTILE):
        xn = xn_ref[r:r + ROW_SUBTILE, :]
        gate = jnp.dot(xn, wg, preferred_element_type=F32)
        up = jnp.dot(xn, wu, preferred_element_type=F32)
        o_ref[r:r + ROW_SUBTILE, :] = (gate * _sigmoid(gate) * up).astype(o_ref.dtype)


def _ffn_up(xn, w_stack, layer, tm, tn):
    m, k = xn.shape
    d_ff = w_stack.shape[2] // 2
    n_blocks = d_ff // tn
    return pl.pallas_call(
        _ffn_up_kernel,
        grid=(m // tm, n_blocks),
        in_specs=[
            pl.BlockSpec((tm, k), lambda i, j: (i, 0)),
            pl.BlockSpec((None, k, tn), lambda i, j: (layer, 0, j)),
            pl.BlockSpec((None, k, tn), lambda i, j: (layer, 0, j + n_blocks)),
        ],
        out_specs=pl.BlockSpec((tm, tn), lambda i, j: (i, j)),
        out_shape=jax.ShapeDtypeStruct((m, d_ff), BF16),
        compiler_params=_params("parallel", "arbitrary"),
        name="ffn_up",
    )(xn, w_stack, w_stack)


def _down_proj_kernel(n_casts, a_ref, w_ref, r_ref, g_ref, *refs):
    srcs, outs = refs[:n_casts], refs[n_casts:]
    out_refs, dsts = outs[:len(outs) - n_casts], outs[len(outs) - n_casts:]
    h = r_ref[...] + jnp.dot(a_ref[...], w_ref[...], preferred_element_type=F32)
    for src, dst in zip(srcs, dsts):
        dst[...] = src[...].astype(BF16)
    hn_ref = out_refs[-1]
    hn_ref[...] = _rmsnorm_rows(h, g_ref[...]).astype(hn_ref.dtype)
    if len(out_refs) == 2:
        out_refs[0][...] = h


def _down_proj(a, w, res, gain, final, tm, casts=()):
    m, k = a.shape
    n = w.shape[1]
    steps = m // tm
    rows = pl.BlockSpec((tm, n), lambda i: (i, 0))
    if final:
        out_specs, out_shape = [rows], [jax.ShapeDtypeStruct((m, n), F32)]
    else:
        out_specs = [rows, rows]
        out_shape = [jax.ShapeDtypeStruct((m, n), F32), jax.ShapeDtypeStruct((m, n), BF16)]
    cast_specs, cast_args = [], []
    for stack, idx in casts:
        kc, nc = stack.shape[1:]
        slice_rows = kc * CAST_STEPS_PER_SLICE // steps
        assert slice_rows * steps == kc * CAST_STEPS_PER_SLICE and slice_rows % BF16_SUBLANES == 0
        cast_specs.append(pl.BlockSpec(
            (None, slice_rows, nc), lambda i, idx=idx: (idx, i // CAST_STEPS_PER_SLICE, 0)))
        cast_args.append(stack)
        out_specs.append(pl.BlockSpec((slice_rows, nc), lambda i: (i // CAST_STEPS_PER_SLICE, 0)))
        out_shape.append(jax.ShapeDtypeStruct((kc, nc), BF16))
    outs = pl.pallas_call(
        functools.partial(_down_proj_kernel, len(casts)),
        grid=(steps,),
        in_specs=[
            pl.BlockSpec((tm, k), lambda i: (i, 0)),
            pl.BlockSpec((k, n), lambda i: (0, 0), pipeline_mode=RESIDENT),
            rows,
            pl.BlockSpec((1, n), lambda i: (0, 0)),
            *cast_specs,
        ],
        out_specs=out_specs,
        out_shape=out_shape,
        compiler_params=_params("arbitrary"),
        name="down_proj",
    )(a, w, res, gain.reshape(1, n), *cast_args)
    n_main = len(outs) - len(casts)
    return outs[:n_main], outs[n_main:]


def _out_proj_kernel(y_ref, xq_ref, kv_ref, w_ref, r_ref, g_ref, h_ref, hn_ref, wb_ref):
    @pl.when(pl.program_id(0) == 0)
    def _():
        wb_ref[...] = w_ref[...].astype(BF16)

    wy = y_ref.shape[1]
    d = xq_ref.shape[1] // XA_HEADS
    acc = jnp.dot(y_ref[...], wb_ref[:wy, :], preferred_element_type=F32)
    heads = []
    for hd in range(XA_HEADS):
        q = xq_ref[:, hd * d:(hd + 1) * d]
        k = kv_ref[:, hd * d:(hd + 1) * d]
        v = kv_ref[:, (XA_HEADS + hd) * d:(XA_HEADS + hd + 1) * d]
        s = lax.dot_general(q, k, NT_DIMS, preferred_element_type=F32) * (d ** -0.5)
        p = jnp.exp(s - jnp.max(s, axis=-1, keepdims=True))
        denom = jnp.sum(p, axis=-1, keepdims=True)
        o = jnp.dot(p.astype(BF16), v, preferred_element_type=F32)
        heads.append((o / denom).astype(BF16))
    acc += jnp.dot(jnp.concatenate(heads, axis=1), wb_ref[wy:, :], preferred_element_type=F32)
    h = r_ref[...] + acc
    h_ref[...] = h
    hn_ref[...] = _rmsnorm_rows(h, g_ref[...]).astype(hn_ref.dtype)


def _out_proj(y, proj, xq_block, kv_stack, w_stack, layer, res, gain, batch, tm):
    m, wy = y.shape
    _, k, n = w_stack.shape
    wx = k - wy
    n_mem, wkv = kv_stack.shape[1:]
    steps_per_batch = m // batch // tm
    return pl.pallas_call(
        _out_proj_kernel,
        grid=(m // tm,),
        in_specs=[
            pl.BlockSpec((tm, wy), lambda i: (i, 0)),
            pl.BlockSpec((tm, wx), lambda i: (i, xq_block)),
            pl.BlockSpec((None, n_mem, wkv), lambda i: (layer * batch + i // steps_per_batch, 0, 0)),
            pl.BlockSpec((None, k, n), lambda i: (layer, 0, 0), pipeline_mode=RESIDENT),
            pl.BlockSpec((tm, n), lambda i: (i, 0)),
            pl.BlockSpec((1, n), lambda i: (0, 0)),
        ],
        out_specs=[pl.BlockSpec((tm, n), lambda i: (i, 0)), pl.BlockSpec((tm, n), lambda i: (i, 0))],
        out_shape=[jax.ShapeDtypeStruct((m, n), F32), jax.ShapeDtypeStruct((m, n), BF16)],
        scratch_shapes=[pltpu.VMEM((k, n), BF16)],
        compiler_params=_params("arbitrary"),
        name="xattn_out_proj",
    )(y, proj, kv_stack, w_stack, res, gain.reshape(1, n))


def _chunk_masks():
    t = lax.broadcasted_iota(jnp.int32, (CHUNK, CHUNK), 0)
    s = lax.broadcasted_iota(jnp.int32, (CHUNK, CHUNK), 1)
    tril = (s <= t)
    return tril.astype(BF16), tril


def _chunk_cumsum(tril_b, hi, lo):
    dk = hi.shape[1]
    both = jnp.dot(tril_b, jnp.concatenate([hi, lo], axis=1), preferred_element_type=F32)
    return both[:, :dk] + both[:, dk:]


def _chunk_scores(q, k, vb, cum, sub):
    dk = q.shape[1]
    n_sub = CHUNK // sub
    blocks = [slice(sub * r, sub * (r + 1)) for r in range(n_sub)]
    mid = [cum[sub * j + sub // 2 - 1:sub * j + sub // 2, :] for j in range(n_sub)]
    last = cum[CHUNK - 1:CHUNK, :]
    zero = jnp.zeros((sub, dk), F32)

    q_slots, k_slots, k_last = [], [], []
    for j in range(n_sub):
        q_slots.append(jnp.concatenate(
            [zero if r < j else q[blocks[r], :] * jnp.exp(cum[blocks[r], :] - mid[j])
             for r in range(n_sub)], axis=0))
        k_mid = k[blocks[j], :] * jnp.exp(mid[j] - cum[blocks[j], :])
        k_slots.append(jnp.concatenate([k_mid if r == j else zero for r in range(n_sub)], axis=0))
        k_last.append(k_mid * jnp.exp(last - mid[j]))
    s_raw = lax.dot_general(jnp.concatenate(q_slots, axis=1).astype(BF16),
                            jnp.concatenate(k_slots, axis=1).astype(BF16),
                            NT_DIMS, preferred_element_type=F32)
    upd = lax.dot_general(vb, jnp.concatenate(k_last, axis=0).astype(BF16), TN_DIMS,
                          preferred_element_type=F32)
    q_dec = (q_slots[0] * jnp.exp(mid[0])).astype(BF16)
    return s_raw, upd, q_dec, jnp.exp(last), vb


def _chunk_readout(s_raw, tril, vb, q_dec, st):
    intra = jnp.dot(jnp.where(tril, s_raw, 0.0).astype(BF16), vb, preferred_element_type=F32)
    inter = lax.dot_general(q_dec, st.astype(BF16), NT_DIMS, preferred_element_type=F32)
    return intra + inter


def _gated_head_norm(o, gain, gate):
    ms = jnp.mean(o * o, axis=-1, keepdims=True)
    return o * lax.rsqrt(ms + EPS) * gain * gate


def _run_staggered(n_chunks, stages):
    n_groups = n_chunks // STAGGER_GROUP
    for slot in range(n_groups + len(stages) - 1):
        for depth in reversed(range(len(stages))):
            group = slot - depth
            if 0 <= group < n_groups:
                for c in range(group * STAGGER_GROUP, (group + 1) * STAGGER_GROUP):
                    stages[depth](c)


def _mixer_kernel(sub, q_ref, k_ref, v_ref, gate_ref, hi_ref, lo_ref, on_ref, y_ref, st_ref):
    @pl.when(pl.program_id(2) == 0)
    def _():
        st_ref[...] = jnp.zeros_like(st_ref)

    tril_b, tril = _chunk_masks()
    gain = on_ref[...]
    n_chunks = q_ref.shape[0] // CHUNK
    rows = [slice(c * CHUNK, (c + 1) * CHUNK) for c in range(n_chunks)]
    ctx = [{} for _ in range(n_chunks)]
    state = [st_ref[...]]

    def decay(c):
        ctx[c]["cum"] = _chunk_cumsum(tril_b, hi_ref[rows[c], :], lo_ref[rows[c], :])

    def scores(c):
        ctx[c]["parts"] = _chunk_scores(q_ref[rows[c], :].astype(F32), k_ref[rows[c], :].astype(F32),
                                        v_ref[rows[c], :], ctx[c].pop("cum"), sub)

    def readout(c):
        s_raw, upd, q_dec, e_last, vb = ctx[c].pop("parts")
        ctx[c]["o"] = _chunk_readout(s_raw, tril, vb, q_dec, state[0])
        state[0] = state[0] * e_last + upd

    def emit(c):
        gate = gate_ref[rows[c], :].astype(F32)
        y_ref[rows[c], :] = _gated_head_norm(ctx[c].pop("o"), gain, gate).astype(y_ref.dtype)

    _run_staggered(n_chunks, [decay, scores, readout, emit])
    st_ref[...] = state[0]


def _mixer(proj, onorm, cols, heads, dk, dv, sub, batch, t_blk, name):
    m = proj.shape[0]
    n_t = m // batch // t_blk

    def group(key, width):
        block0, rem = divmod(cols[key], width)
        assert rem == 0
        return pl.BlockSpec((t_blk, width), lambda b, h, t: (b * n_t + t, block0 + h))

    return pl.pallas_call(
        functools.partial(_mixer_kernel, sub),
        grid=(batch, heads, n_t),
        in_specs=[group("q", dk), group("k", dk), group("v", dv), group("gate", dv),
                  group("hi", dk), group("lo", dk),
                  pl.BlockSpec((1, dv), lambda b, h, t: (0, 0))],
        out_specs=pl.BlockSpec((t_blk, dv), lambda b, h, t: (b * n_t + t, h)),
        out_shape=jax.ShapeDtypeStruct((m, heads * dv), BF16),
        scratch_shapes=[pltpu.VMEM((dv, dk), F32)],
        compiler_params=_params("parallel", "parallel", "arbitrary"),
        name=name,
    )(proj, proj, proj, proj, proj, proj, onorm.reshape(1, dv))


def _gla_pad_decay_weights(w_gk, b_gk):
    heads, rank, dkp = GLA_HEADS, GLA_GATE_RANK, GLA_DK_PAD
    kw = w_gk.shape[1]
    dk = kw // heads

    def pad_heads(w):
        w = w.reshape(w.shape[0], heads, dk)
        return jnp.pad(w, ((0, 0), (0, 0), (0, dkp - dk))).reshape(w.shape[0], heads * dkp)

    w_gk_pad = jnp.pad(pad_heads(w_gk), ((0, LANE - rank), (0, 0))).astype(BF16)
    b_gk_pad = pad_heads(b_gk.reshape(1, kw)).astype(F32)
    return w_gk_pad, b_gk_pad, dk


def kernel(x, mem, norm_mix, norm_ffn, norm_mem, norm_final, hgrn_w_in, hgrn_lb_logits, hgrn_onorm,
           gla_w_in, gla_w_gk, gla_b_gk, gla_onorm, w_mem_kv, w_out, w_gate_up, w_down):
    batch, seq, d_model = x.shape
    n_mem = mem.shape[1]
    depth = norm_mix.shape[0]
    mix_width = hgrn_lb_logits.shape[1]
    xa_width = d_model - mix_width
    m = batch * seq

    h = x.reshape(m, d_model)
    kv = _mem_kv(mem.reshape(batch * n_mem, d_model), norm_mem, w_mem_kv)
    kv = kv.reshape(depth * batch, n_mem, 2 * xa_width)

    w_in_b = hgrn_w_in[0].astype(BF16)
    w_down_b = w_down[0].astype(BF16)
    gla_w = gla_w_in.astype(BF16)

    for layer in range(depth):
        j = layer // 2
        if layer % 2 == 0:
            if layer == 0:
                proj = _hgrn_in_proj(h, norm_mix[0], w_in_b, j, hgrn_lb_logits, tm=256)
            else:
                proj = _hgrn_in_proj(hn, None, w_in_b, j, hgrn_lb_logits, tm=512)
            w = mix_width
            cols = dict(q=0, hi=w, lo=2 * w, k=3 * w, v=4 * w, gate=5 * w)
            y = _mixer(proj, hgrn_onorm[j], cols, mix_width // HGRN_HEAD_DIM, HGRN_HEAD_DIM,
                       HGRN_HEAD_DIM, HGRN_SUB, batch, t_blk=4096, name="hgrn2_mixer")
            xq_col = 6 * w
        else:
            w_gk_pad, b_gk_pad, dk = _gla_pad_decay_weights(gla_w_gk[j], gla_b_gk[j])
            proj = _gla_in_proj(hn, gla_w, j, w_gk_pad, b_gk_pad, mix_width, xa_width, dk, tm=512)
            wk = GLA_HEADS * GLA_DK_PAD
            xq_col = 2 * mix_width + 2 * wk
            cols = dict(v=0, gate=mix_width, q=2 * mix_width, k=2 * mix_width + wk,
                        hi=xq_col + xa_width, lo=xq_col + xa_width + wk)
            y = _mixer(proj, gla_onorm[j], cols, GLA_HEADS, GLA_DK_PAD, mix_width // GLA_HEADS,
                       GLA_SUB, batch, t_blk=4096, name="gla_mixer")
        assert xq_col % xa_width == 0
        h, hn = _out_proj(y, proj, xq_col // xa_width, kv, w_out, layer, h, norm_ffn[layer], batch,
                          tm=512)
        act = _ffn_up(hn, w_gate_up, layer, tm=2048, tn=512)
        if layer + 1 < depth:
            nxt = layer + 1
            casts = [(w_down, nxt)] + ([(hgrn_w_in, nxt // 2)] if nxt % 2 == 0 else [])
            (h, hn), (w_down_b, *w_in_next) = _down_proj(
                act, w_down_b, h, norm_mix[nxt], final=False, tm=256, casts=casts)
            if w_in_next:
                w_in_b = w_in_next[0]
        else:
            (out,), _ = _down_proj(act, w_down_b, h, norm_final, final=True, tm=256)
    return out.reshape(batch, seq, d_model)
```

```python
import functools

import jax
import jax.numpy as jnp
from jax import lax
from jax.experimental import pallas as pl
from jax.experimental.pallas import tpu as pltpu

F32 = jnp.float32
BF16 = jnp.bfloat16

EPS = 1e-6
CHUNK = 64
STAGGER_GROUP = 8
ROW_SUBTILE = 256
IN_PROJ_SUBTILE = 256
CAST_STEPS_PER_SLICE = 2
BF16_SUBLANES = 16
HGRN_SUB = 32
GLA_SUB = 64
XA_HEADS = 4
HGRN_HEAD_DIM = 128
GLA_HEADS = 4
GLA_GATE_RANK = 16
GLA_GATE_NORMALIZER = 16.0
LANE = 128
GLA_DK_PAD = 256
V7X_VMEM_LIMIT = 56 * 1024 * 1024

NT_DIMS = (((1,), (1,)), ((), ()))
TN_DIMS = (((0,), (0,)), ((), ()))

RESIDENT = pl.Buffered(1)


def _params(*semantics):
    return pltpu.CompilerParams(dimension_semantics=semantics, vmem_limit_bytes=V7X_VMEM_LIMIT)


def _sigmoid(x):
    return 1.0 / (1.0 + jnp.exp(-x))


def _rmsnorm_rows(x, gain):
    ms = jnp.mean(x * x, axis=-1, keepdims=True)
    return x * lax.rsqrt(ms + EPS) * gain


def _split_hi_lo(g):
    hi = g.astype(BF16)
    return hi, (g - hi.astype(F32)).astype(BF16)


def _hgrn_in_proj_kernel(layer_j, normalise, xn_ref, *refs):
    gn_ref = refs[0] if normalise else None
    w_ref, lbl_ref, o_ref = refs[-3:]
    w = lbl_ref.shape[1]

    logits = lbl_ref[...]
    ex = jnp.exp(logits - jnp.max(logits, axis=0, keepdims=True))
    sm = ex / jnp.sum(ex, axis=0, keepdims=True)
    lb = jnp.zeros((1, w), F32)
    for r in range(1, layer_j + 1):
        lb = lb + sm[r:r + 1, :]

    for r in range(0, xn_ref.shape[0], IN_PROJ_SUBTILE):
        rows = slice(r, r + IN_PROJ_SUBTILE)
        xn = xn_ref[rows, :]
        if normalise:
            xn = _rmsnorm_rows(xn, gn_ref[...]).astype(BF16)

        def proj(lo, hi):
            return jnp.dot(xn, w_ref[:, lo:hi], preferred_element_type=F32)

        o_ref[rows, 4 * w:5 * w] = proj(2 * w, 3 * w).astype(BF16)
        f = lb + (1.0 - lb) * _sigmoid(proj(w, 2 * w))
        hi, lo = _split_hi_lo(jnp.log(f))
        o_ref[rows, w:2 * w] = hi
        o_ref[rows, 2 * w:3 * w] = lo
        o_ref[rows, 3 * w:4 * w] = (1.0 - f).astype(BF16)
        q = proj(0, w)
        o_ref[rows, 0:w] = (q * _sigmoid(q)).astype(BF16)
        g = proj(3 * w, 4 * w)
        o_ref[rows, 5 * w:6 * w] = (g * _sigmoid(g)).astype(BF16)
        o_ref[rows, 6 * w:] = proj(4 * w, w_ref.shape[1]).astype(BF16)


def _hgrn_in_proj(xn, gain, w, layer_j, lb_logits, tm):
    m, k = xn.shape
    n_in = w.shape[1]
    n_a, mix_width = lb_logits.shape
    n_out = n_in + 2 * mix_width
    normalise = gain is not None
    norm_specs = [pl.BlockSpec((1, k), lambda i: (0, 0))] if normalise else []
    norm_args = [gain.reshape(1, k)] if normalise else []
    return pl.pallas_call(
        functools.partial(_hgrn_in_proj_kernel, layer_j, normalise),
        grid=(m // tm,),
        in_specs=[
            pl.BlockSpec((tm, k), lambda i: (i, 0)),
            *norm_specs,
            pl.BlockSpec((k, n_in), lambda i: (0, 0), pipeline_mode=RESIDENT),
            pl.BlockSpec((n_a, mix_width), lambda i: (0, 0)),
        ],
        out_specs=pl.BlockSpec((tm, n_out), lambda i: (i, 0)),
        out_shape=jax.ShapeDtypeStruct((m, n_out), BF16),
        compiler_params=_params("parallel"),
        name="hgrn_in_proj",
    )(xn, *norm_args, w, lb_logits)


def _gla_in_proj_kernel(dk_true, wv, wx, xn_ref, w_ref, wgk_ref, bgk_ref, o_ref):
    dkp = GLA_DK_PAD
    wk = wgk_ref.shape[1]
    wqk = GLA_HEADS * dk_true
    in_v, in_g, in_tail = 2 * wqk, 2 * wqk + wv, 2 * wqk + 2 * wv
    c_q = 2 * wv
    c_x = c_q + 2 * wk
    zero_pad = jnp.zeros((IN_PROJ_SUBTILE, dkp - dk_true), BF16)

    for r in range(0, xn_ref.shape[0], IN_PROJ_SUBTILE):
        rows = slice(r, r + IN_PROJ_SUBTILE)
        xn = xn_ref[rows, :]

        def proj(lo, hi):
            return jnp.dot(xn, w_ref[:, lo:hi], preferred_element_type=F32)

        def log_decay(head):
            cols = slice(head * dkp, (head + 1) * dkp)
            x = jnp.dot(low, wgk_ref[:, cols], preferred_element_type=F32) + bgk_ref[:, cols]
            log_a = (jnp.minimum(x, 0.0) - jnp.log1p(jnp.exp(-jnp.abs(x)))) / GLA_GATE_NORMALIZER
            hi, lo = _split_hi_lo(log_a)
            o_ref[rows, c_x + wx + head * dkp:c_x + wx + (head + 1) * dkp] = hi
            o_ref[rows, c_x + wx + wk + head * dkp:c_x + wx + wk + (head + 1) * dkp] = lo

        def store_heads(col0, values, scale):
            for hd in range(GLA_HEADS):
                piece = values[:, hd * dk_true:(hd + 1) * dk_true]
                o_ref[rows, col0 + hd * dkp:col0 + hd * dkp + dk_true] = (piece * scale).astype(BF16)
                o_ref[rows, col0 + hd * dkp + dk_true:col0 + (hd + 1) * dkp] = zero_pad

        tail = proj(in_tail, w_ref.shape[1])
        low = tail[:, :LANE].astype(BF16)
        o_ref[rows, c_x:c_x + wx] = tail[:, GLA_GATE_RANK:GLA_GATE_RANK + wx].astype(BF16)
        o_ref[rows, 0:wv] = proj(in_v, in_g).astype(BF16)
        log_decay(0)
        g = proj(in_g, in_tail)
        o_ref[rows, wv:c_q] = (g * _sigmoid(g)).astype(BF16)
        log_decay(1)
        qk = proj(0, in_v)
        store_heads(c_q, qk[:, :wqk], dk_true ** -0.5)
        log_decay(2)
        store_heads(c_q + wk, qk[:, wqk:], 1.0)
        log_decay(3)


def _gla_in_proj(xn, w_stack, layer_j, w_gk_pad, b_gk_pad, mix_width, xa_width, dk_true, tm):
    m, k = xn.shape
    n_in = w_stack.shape[2]
    wk = w_gk_pad.shape[1]
    n_out = 2 * mix_width + 2 * wk + xa_width + 2 * wk
    return pl.pallas_call(
        functools.partial(_gla_in_proj_kernel, dk_true, mix_width, xa_width),
        grid=(m // tm,),
        in_specs=[
            pl.BlockSpec((tm, k), lambda i: (i, 0)),
            pl.BlockSpec((None, k, n_in), lambda i: (layer_j, 0, 0), pipeline_mode=RESIDENT),
            pl.BlockSpec((LANE, wk), lambda i: (0, 0)),
            pl.BlockSpec((1, wk), lambda i: (0, 0)),
        ],
        out_specs=pl.BlockSpec((tm, n_out), lambda i: (i, 0)),
        out_shape=jax.ShapeDtypeStruct((m, n_out), BF16),
        compiler_params=_params("parallel"),
        name="gla_in_proj",
    )(xn, w_stack, w_gk_pad, b_gk_pad)


def _mem_kv_kernel(x_ref, g_ref, w_ref, o_ref):
    xn = _rmsnorm_rows(x_ref[...], g_ref[...]).astype(BF16)
    o_ref[...] = jnp.dot(xn, w_ref[...].astype(BF16), preferred_element_type=F32).astype(o_ref.dtype)


def _mem_kv(mem2, gain, w_stack):
    m, k = mem2.shape
    depth, _, n = w_stack.shape
    return pl.pallas_call(
        _mem_kv_kernel,
        grid=(depth,),
        in_specs=[
            pl.BlockSpec((m, k), lambda l: (0, 0)),
            pl.BlockSpec((1, k), lambda l: (0, 0)),
            pl.BlockSpec((None, k, n), lambda l: (l, 0, 0)),
        ],
        out_specs=pl.BlockSpec((None, m, n), lambda l: (l, 0, 0)),
        out_shape=jax.ShapeDtypeStruct((depth, m, n), BF16),
        compiler_params=_params("parallel"),
        name="mem_kv_proj",
    )(mem2, gain.reshape(1, k), w_stack)


def _ffn_up_kernel(xn_ref, wg_ref, wu_ref, wd_ref, o_ref, wd_b_ref):
    wd_b_ref[...] = wd_ref[...].astype(BF16)
    wg = wg_ref[...].astype(BF16)
    wu = wu_ref[...].astype(BF16)
    for r in range(0, xn_ref.shape[0], ROW_SUBTILE):
        xn = xn_ref[r:r + ROW_SUBTILE, :]
        gate = jnp.dot(xn, wg, preferred_element_type=F32)
        up = jnp.dot(xn, wu, preferred_element_type=F32)
        o_ref[r:r + ROW_SUBTILE, :] = (gate * _sigmoid(gate) * up).astype(o_ref.dtype)


def _ffn_up(xn, w_stack, w_down_stack, layer, tm, tn):
    m, k = xn.shape
    d_ff = w_stack.shape[2] // 2
    n_blocks = d_ff // tn
    steps = (m // tm) * n_blocks
    kd, nd = w_down_stack.shape[1:]
    slice_rows = kd // steps
    assert slice_rows * steps == kd and slice_rows % BF16_SUBLANES == 0
    return pl.pallas_call(
        _ffn_up_kernel,
        grid=(m // tm, n_blocks),
        in_specs=[
            pl.BlockSpec((tm, k), lambda i, j: (i, 0)),
            pl.BlockSpec((None, k, tn), lambda i, j: (layer, 0, j)),
            pl.BlockSpec((None, k, tn), lambda i, j: (layer, 0, j + n_blocks)),
            pl.BlockSpec((None, slice_rows, nd), lambda i, j: (layer, i * n_blocks + j, 0)),
        ],
        out_specs=[pl.BlockSpec((tm, tn), lambda i, j: (i, j)),
                   pl.BlockSpec((slice_rows, nd), lambda i, j: (i * n_blocks + j, 0))],
        out_shape=[jax.ShapeDtypeStruct((m, d_ff), BF16), jax.ShapeDtypeStruct((kd, nd), BF16)],
        compiler_params=_params("parallel", "arbitrary"),
        name="ffn_up",
    )(xn, w_stack, w_stack, w_down_stack)


def _down_proj_kernel(n_casts, a_ref, w_ref, r_ref, g_ref, *refs):
    srcs, outs = refs[:n_casts], refs[n_casts:]
    out_refs, dsts = outs[:len(outs) - n_casts], outs[len(outs) - n_casts:]
    h = r_ref[...] + jnp.dot(a_ref[...], w_ref[...], preferred_element_type=F32)
    for src, dst in zip(srcs, dsts):
        dst[...] = src[...].astype(BF16)
    hn_ref = out_refs[-1]
    hn_ref[...] = _rmsnorm_rows(h, g_ref[...]).astype(hn_ref.dtype)
    if len(out_refs) == 2:
        out_refs[0][...] = h


def _down_proj(a, w, res, gain, final, tm, casts=()):
    m, k = a.shape
    n = w.shape[1]
    steps = m // tm
    rows = pl.BlockSpec((tm, n), lambda i: (i, 0))
    if final:
        out_specs, out_shape = [rows], [jax.ShapeDtypeStruct((m, n), F32)]
    else:
        out_specs = [rows, rows]
        out_shape = [jax.ShapeDtypeStruct((m, n), F32), jax.ShapeDtypeStruct((m, n), BF16)]
    cast_specs, cast_args = [], []
    for stack, idx in casts:
        kc, nc = stack.shape[1:]
        slice_rows = kc * CAST_STEPS_PER_SLICE // steps
        assert slice_rows * steps == kc * CAST_STEPS_PER_SLICE and slice_rows % BF16_SUBLANES == 0
        cast_specs.append(pl.BlockSpec(
            (None, slice_rows, nc), lambda i, idx=idx: (idx, i // CAST_STEPS_PER_SLICE, 0)))
        cast_args.append(stack)
        out_specs.append(pl.BlockSpec((slice_rows, nc), lambda i: (i // CAST_STEPS_PER_SLICE, 0)))
        out_shape.append(jax.ShapeDtypeStruct((kc, nc), BF16))
    outs = pl.pallas_call(
        functools.partial(_down_proj_kernel, len(casts)),
        grid=(steps,),
        in_specs=[
            pl.BlockSpec((tm, k), lambda i: (i, 0)),
            pl.BlockSpec((k, n), lambda i: (0, 0), pipeline_mode=RESIDENT),
            rows,
            pl.BlockSpec((1, n), lambda i: (0, 0)),
            *cast_specs,
        ],
        out_specs=out_specs,
        out_shape=out_shape,
        compiler_params=_params("arbitrary"),
        name="down_proj",
    )(a, w, res, gain.reshape(1, n), *cast_args)
    n_main = len(outs) - len(casts)
    return outs[:n_main], outs[n_main:]


def _out_proj_kernel(y_ref, xq_ref, kv_ref, w_ref, r_ref, g_ref, h_ref, hn_ref, wb_ref):
    @pl.when(pl.program_id(0) == 0)
    def _():
        wb_ref[...] = w_ref[...].astype(BF16)

    wy = y_ref.shape[1]
    d = xq_ref.shape[1] // XA_HEADS
    acc = jnp.dot(y_ref[...], wb_ref[:wy, :], preferred_element_type=F32)
    heads = []
    for hd in range(XA_HEADS):
        q = xq_ref[:, hd * d:(hd + 1) * d]
        k = kv_ref[:, hd * d:(hd + 1) * d]
        v = kv_ref[:, (XA_HEADS + hd) * d:(XA_HEADS + hd + 1) * d]
        s = lax.dot_general(q, k, NT_DIMS, preferred_element_type=F32) * (d ** -0.5)
        p = jnp.exp(s - jnp.max(s, axis=-1, keepdims=True))
        denom = jnp.sum(p, axis=-1, keepdims=True)
        o = jnp.dot(p.astype(BF16), v, preferred_element_type=F32)
        heads.append((o / denom).astype(BF16))
    acc += jnp.dot(jnp.concatenate(heads, axis=1), wb_ref[wy:, :], preferred_element_type=F32)
    h = r_ref[...] + acc
    h_ref[...] = h
    hn_ref[...] = _rmsnorm_rows(h, g_ref[...]).astype(hn_ref.dtype)


def _out_proj(y, proj, xq_block, kv_stack, w_stack, layer, res, gain, batch, tm):
    m, wy = y.shape
    _, k, n = w_stack.shape
    wx = k - wy
    n_mem, wkv = kv_stack.shape[1:]
    steps_per_batch = m // batch // tm
    return pl.pallas_call(
        _out_proj_kernel,
        grid=(m // tm,),
        in_specs=[
            pl.BlockSpec((tm, wy), lambda i: (i, 0)),
            pl.BlockSpec((tm, wx), lambda i: (i, xq_block)),
            pl.BlockSpec((None, n_mem, wkv), lambda i: (layer * batch + i // steps_per_batch, 0, 0)),
            pl.BlockSpec((None, k, n), lambda i: (layer, 0, 0), pipeline_mode=RESIDENT),
            pl.BlockSpec((tm, n), lambda i: (i, 0)),
            pl.BlockSpec((1, n), lambda i: (0, 0)),
        ],
        out_specs=[pl.BlockSpec((tm, n), lambda i: (i, 0)), pl.BlockSpec((tm, n), lambda i: (i, 0))],
        out_shape=[jax.ShapeDtypeStruct((m, n), F32), jax.ShapeDtypeStruct((m, n), BF16)],
        scratch_shapes=[pltpu.VMEM((k, n), BF16)],
        compiler_params=_params("arbitrary"),
        name="xattn_out_proj",
    )(y, proj, kv_stack, w_stack, res, gain.reshape(1, n))


def _chunk_masks():
    t = lax.broadcasted_iota(jnp.int32, (CHUNK, CHUNK), 0)
    s = lax.broadcasted_iota(jnp.int32, (CHUNK, CHUNK), 1)
    tril = (s <= t)
    return tril.astype(BF16), tril


def _chunk_cumsum(tril_b, hi, lo):
    dk = hi.shape[1]
    both = jnp.dot(tril_b, jnp.concatenate([hi, lo], axis=1), preferred_element_type=F32)
    return both[:, :dk] + both[:, dk:]


def _chunk_scores(q, k, vb, cum, sub):
    dk = q.shape[1]
    n_sub = CHUNK // sub
    blocks = [slice(sub * r, sub * (r + 1)) for r in range(n_sub)]
    mid = [cum[sub * j + sub // 2 - 1:sub * j + sub // 2, :] for j in range(n_sub)]
    last = cum[CHUNK - 1:CHUNK, :]
    zero = jnp.zeros((sub, dk), F32)

    q_slots, k_slots, k_last = [], [], []
    for j in range(n_sub):
        q_slots.append(jnp.concatenate(
            [zero if r < j else q[blocks[r], :] * jnp.exp(cum[blocks[r], :] - mid[j])
             for r in range(n_sub)], axis=0))
        k_mid = k[blocks[j], :] * jnp.exp(mid[j] - cum[blocks[j], :])
        k_slots.append(jnp.concatenate([k_mid if r == j else zero for r in range(n_sub)], axis=0))
        k_last.append(k_mid * jnp.exp(last - mid[j]))
    s_raw = lax.dot_general(jnp.concatenate(q_slots, axis=1).astype(BF16),
                            jnp.concatenate(k_slots, axis=1).astype(BF16),
                            NT_DIMS, preferred_element_type=F32)
    upd = lax.dot_general(vb, jnp.concatenate(k_last, axis=0).astype(BF16), TN_DIMS,
                          preferred_element_type=F32)
    q_dec = (q_slots[0] * jnp.exp(mid[0])).astype(BF16)
    return s_raw, upd, q_dec, jnp.exp(last), vb


def _chunk_readout(s_raw, tril, vb, q_dec, st):
    intra = jnp.dot(jnp.where(tril, s_raw, 0.0).astype(BF16), vb, preferred_element_type=F32)
    inter = lax.dot_general(q_dec, st.astype(BF16), NT_DIMS, preferred_element_type=F32)
    return intra + inter


def _gated_head_norm(o, gain, gate):
    ms = jnp.mean(o * o, axis=-1, keepdims=True)
    return o * lax.rsqrt(ms + EPS) * gain * gate


def _run_staggered(n_chunks, stages):
    n_groups = n_chunks // STAGGER_GROUP
    for slot in range(n_groups + len(stages) - 1):
        for depth in reversed(range(len(stages))):
            group = slot - depth
            if 0 <= group < n_groups:
                for c in range(group * STAGGER_GROUP, (group + 1) * STAGGER_GROUP):
                    stages[depth](c)


def _mixer_kernel(sub, q_ref, k_ref, v_ref, gate_ref, hi_ref, lo_ref, on_ref, y_ref, st_ref):
    @pl.when(pl.program_id(2) == 0)
    def _():
        st_ref[...] = jnp.zeros_like(st_ref)

    tril_b, tril = _chunk_masks()
    gain = on_ref[...]
    n_chunks = q_ref.shape[0] // CHUNK
    rows = [slice(c * CHUNK, (c + 1) * CHUNK) for c in range(n_chunks)]
    ctx = [{} for _ in range(n_chunks)]
    state = [st_ref[...]]

    def decay(c):
        ctx[c]["cum"] = _chunk_cumsum(tril_b, hi_ref[rows[c], :], lo_ref[rows[c], :])

    def scores(c):
        ctx[c]["parts"] = _chunk_scores(q_ref[rows[c], :].astype(F32), k_ref[rows[c], :].astype(F32),
                                        v_ref[rows[c], :], ctx[c].pop("cum"), sub)

    def readout(c):
        s_raw, upd, q_dec, e_last, vb = ctx[c].pop("parts")
        ctx[c]["o"] = _chunk_readout(s_raw, tril, vb, q_dec, state[0])
        state[0] = state[0] * e_last + upd

    def emit(c):
        gate = gate_ref[rows[c], :].astype(F32)
        y_ref[rows[c], :] = _gated_head_norm(ctx[c].pop("o"), gain, gate).astype(y_ref.dtype)

    _run_staggered(n_chunks, [decay, scores, readout, emit])
    st_ref[...] = state[0]


def _mixer(proj, onorm, cols, heads, dk, dv, sub, batch, t_blk, name):
    m = proj.shape[0]
    n_t = m // batch // t_blk

    def group(key, width):
        block0, rem = divmod(cols[key], width)
        assert rem == 0
        return pl.BlockSpec((t_blk, width), lambda b, h, t: (b * n_t + t, block0 + h))

    return pl.pallas_call(
        functools.partial(_mixer_kernel, sub),
        grid=(batch, heads, n_t),
        in_specs=[group("q", dk), group("k", dk), group("v", dv), group("gate", dv),
                  group("hi", dk), group("lo", dk),
                  pl.BlockSpec((1, dv), lambda b, h, t: (0, 0))],
        out_specs=pl.BlockSpec((t_blk, dv), lambda b, h, t: (b * n_t + t, h)),
        out_shape=jax.ShapeDtypeStruct((m, heads * dv), BF16),
        scratch_shapes=[pltpu.VMEM((dv, dk), F32)],
        compiler_params=_params("parallel", "parallel", "arbitrary"),
        name=name,
    )(proj, proj, proj, proj, proj, proj, onorm.reshape(1, dv))


def _gla_pad_decay_weights(w_gk, b_gk):
    heads, rank, dkp = GLA_HEADS, GLA_GATE_RANK, GLA_DK_PAD
    kw = w_gk.shape[1]
    dk = kw // heads

    def pad_heads(w):
        w = w.reshape(w.shape[0], heads, dk)
        return jnp.pad(w, ((0, 0), (0, 0), (0, dkp - dk))).reshape(w.shape[0], heads * dkp)

    w_gk_pad = jnp.pad(pad_heads(w_gk), ((0, LANE - rank), (0, 0))).astype(BF16)
    b_gk_pad = pad_heads(b_gk.reshape(1, kw)).astype(F32)
    return w_gk_pad, b_gk_pad, dk


def kernel(x, mem, norm_mix, norm_ffn, norm_mem, norm_final, hgrn_w_in, hgrn_lb_logits, hgrn_onorm,
           gla_w_in, gla_w_gk, gla_b_gk, gla_onorm, w_mem_kv, w_out, w_gate_up, w_down):
    batch, seq, d_model = x.shape
    n_mem = mem.shape[1]
    depth = norm_mix.shape[0]
    mix_width = hgrn_lb_logits.shape[1]
    xa_width = d_model - mix_width
    m = batch * seq

    h = x.reshape(m, d_model)
    kv = _mem_kv(mem.reshape(batch * n_mem, d_model), norm_mem, w_mem_kv)
    kv = kv.reshape(depth * batch, n_mem, 2 * xa_width)

    w_in_b = hgrn_w_in[0].astype(BF16)
    gla_w = gla_w_in.astype(BF16)

    for layer in range(depth):
        j = layer // 2
        if layer % 2 == 0:
            if layer == 0:
                proj = _hgrn_in_proj(h, norm_mix[0], w_in_b, j, hgrn_lb_logits, tm=256)
            else:
                proj = _hgrn_in_proj(hn, None, w_in_b, j, hgrn_lb_logits, tm=512)
            w = mix_width
            cols = dict(q=0, hi=w, lo=2 * w, k=3 * w, v=4 * w, gate=5 * w)
            y = _mixer(proj, hgrn_onorm[j], cols, mix_width // HGRN_HEAD_DIM, HGRN_HEAD_DIM,
                       HGRN_HEAD_DIM, HGRN_SUB, batch, t_blk=4096, name="hgrn2_mixer")
            xq_col = 6 * w
        else:
            w_gk_pad, b_gk_pad, dk = _gla_pad_decay_weights(gla_w_gk[j], gla_b_gk[j])
            proj = _gla_in_proj(hn, gla_w, j, w_gk_pad, b_gk_pad, mix_width, xa_width, dk, tm=512)
            wk = GLA_HEADS * GLA_DK_PAD
            xq_col = 2 * mix_width + 2 * wk
            cols = dict(v=0, gate=mix_width, q=2 * mix_width, k=2 * mix_width + wk,
                        hi=xq_col + xa_width, lo=xq_col + xa_width + wk)
            y = _mixer(proj, gla_onorm[j], cols, GLA_HEADS, GLA_DK_PAD, mix_width // GLA_HEADS,
                       GLA_SUB, batch, t_blk=4096, name="gla_mixer")
        assert xq_col % xa_width == 0
        h, hn = _out_proj(y, proj, xq_col // xa_width, kv, w_out, layer, h, norm_ffn[layer], batch,
                          tm=512)
        act, w_down_b = _ffn_up(hn, w_gate_up, w_down, layer, tm=2048, tn=512)
        if layer + 1 < depth:
            nxt = layer + 1
            casts = [(hgrn_w_in, nxt // 2)] if nxt % 2 == 0 else []
            (h, hn), w_in_next = _down_proj(
                act, w_down_b, h, norm_mix[nxt], final=False, tm=256, casts=casts)
            if w_in_next:
                w_in_b = w_in_next[0]
        else:
            (out,), _ = _down_proj(act, w_down_b, h, norm_final, final=True, tm=256)
    return out.reshape(batch, seq, d_model)
```

```python
import functools

import jax
import jax.numpy as jnp
from jax import lax
from jax.experimental import pallas as pl
from jax.experimental.pallas import tpu as pltpu

F32 = jnp.float32
BF16 = jnp.bfloat16

EPS = 1e-6
CHUNK = 64
STAGGER_GROUP = 8
ROW_SUBTILE = 256
IN_PROJ_SUBTILE = 256
CAST_STEPS_PER_SLICE = 2
BF16_SUBLANES = 16
HGRN_SUB = 32
GLA_SUB = 64
XA_HEADS = 4
HGRN_HEAD_DIM = 128
GLA_HEADS = 4
GLA_GATE_RANK = 16
GLA_GATE_NORMALIZER = 16.0
LANE = 128
GLA_DK_PAD = 256
V7X_VMEM_LIMIT = 56 * 1024 * 1024

NT_DIMS = (((1,), (1,)), ((), ()))
TN_DIMS = (((0,), (0,)), ((), ()))

RESIDENT = pl.Buffered(1)


def _params(*semantics):
    return pltpu.CompilerParams(dimension_semantics=semantics, vmem_limit_bytes=V7X_VMEM_LIMIT)


def _sigmoid(x):
    return 1.0 / (1.0 + jnp.exp(-x))


def _rmsnorm_rows(x, gain):
    ms = jnp.mean(x * x, axis=-1, keepdims=True)
    return x * lax.rsqrt(ms + EPS) * gain


def _split_hi_lo(g):
    hi = g.astype(BF16)
    return hi, (g - hi.astype(F32)).astype(BF16)


def _hgrn_in_proj_kernel(layer_j, normalise, xn_ref, *refs):
    gn_ref = refs[0] if normalise else None
    w_ref, lbl_ref, o_ref = refs[-3:]
    w = lbl_ref.shape[1]

    logits = lbl_ref[...]
    ex = jnp.exp(logits - jnp.max(logits, axis=0, keepdims=True))
    sm = ex / jnp.sum(ex, axis=0, keepdims=True)
    lb = jnp.zeros((1, w), F32)
    for r in range(1, layer_j + 1):
        lb = lb + sm[r:r + 1, :]

    for r in range(0, xn_ref.shape[0], IN_PROJ_SUBTILE):
        rows = slice(r, r + IN_PROJ_SUBTILE)
        xn = xn_ref[rows, :]
        if normalise:
            xn = _rmsnorm_rows(xn, gn_ref[...]).astype(BF16)

        def proj(lo, hi):
            return jnp.dot(xn, w_ref[:, lo:hi], preferred_element_type=F32)

        o_ref[rows, 4 * w:5 * w] = proj(2 * w, 3 * w).astype(BF16)
        f = lb + (1.0 - lb) * _sigmoid(proj(w, 2 * w))
        hi, lo = _split_hi_lo(jnp.log(f))
        o_ref[rows, w:2 * w] = hi
        o_ref[rows, 2 * w:3 * w] = lo
        o_ref[rows, 3 * w:4 * w] = (1.0 - f).astype(BF16)
        q = proj(0, w)
        o_ref[rows, 0:w] = (q * _sigmoid(q)).astype(BF16)
        g = proj(3 * w, 4 * w)
        o_ref[rows, 5 * w:6 * w] = (g * _sigmoid(g)).astype(BF16)
        o_ref[rows, 6 * w:] = proj(4 * w, w_ref.shape[1]).astype(BF16)


def _hgrn_in_proj(xn, gain, w, layer_j, lb_logits, tm):
    m, k = xn.shape
    n_in = w.shape[1]
    n_a, mix_width = lb_logits.shape
    n_out = n_in + 2 * mix_width
    normalise = gain is not None
    norm_specs = [pl.BlockSpec((1, k), lambda i: (0, 0))] if normalise else []
    norm_args = [gain.reshape(1, k)] if normalise else []
    return pl.pallas_call(
        functools.partial(_hgrn_in_proj_kernel, layer_j, normalise),
        grid=(m // tm,),
        in_specs=[
            pl.BlockSpec((tm, k), lambda i: (i, 0)),
            *norm_specs,
            pl.BlockSpec((k, n_in), lambda i: (0, 0), pipeline_mode=RESIDENT),
            pl.BlockSpec((n_a, mix_width), lambda i: (0, 0)),
        ],
        out_specs=pl.BlockSpec((tm, n_out), lambda i: (i, 0)),
        out_shape=jax.ShapeDtypeStruct((m, n_out), BF16),
        compiler_params=_params("parallel"),
        name="hgrn_in_proj",
    )(xn, *norm_args, w, lb_logits)


def _gla_in_proj_kernel(dk_true, wv, wx, xn_ref, w_ref, wgk_ref, bgk_ref, o_ref):
    dkp = GLA_DK_PAD
    wk = wgk_ref.shape[1]
    wqk = GLA_HEADS * dk_true
    in_v, in_g, in_tail = 2 * wqk, 2 * wqk + wv, 2 * wqk + 2 * wv
    c_q = 2 * wv
    c_x = c_q + 2 * wk
    zero_pad = jnp.zeros((IN_PROJ_SUBTILE, dkp - dk_true), BF16)

    for r in range(0, xn_ref.shape[0], IN_PROJ_SUBTILE):
        rows = slice(r, r + IN_PROJ_SUBTILE)
        xn = xn_ref[rows, :]

        def proj(lo, hi):
            return jnp.dot(xn, w_ref[:, lo:hi], preferred_element_type=F32)

        def log_decay(head):
            cols = slice(head * dkp, (head + 1) * dkp)
            x = jnp.dot(low, wgk_ref[:, cols], preferred_element_type=F32) + bgk_ref[:, cols]
            log_a = (jnp.minimum(x, 0.0) - jnp.log1p(jnp.exp(-jnp.abs(x)))) / GLA_GATE_NORMALIZER
            hi, lo = _split_hi_lo(log_a)
            o_ref[rows, c_x + wx + head * dkp:c_x + wx + (head + 1) * dkp] = hi
            o_ref[rows, c_x + wx + wk + head * dkp:c_x + wx + wk + (head + 1) * dkp] = lo

        def store_heads(col0, values, scale):
            for hd in range(GLA_HEADS):
                piece = values[:, hd * dk_true:(hd + 1) * dk_true]
                o_ref[rows, col0 + hd * dkp:col0 + hd * dkp + dk_true] = (piece * scale).astype(BF16)
                o_ref[rows, col0 + hd * dkp + dk_true:col0 + (hd + 1) * dkp] = zero_pad

        tail = proj(in_tail, w_ref.shape[1])
        low = tail[:, :LANE].astype(BF16)
        o_ref[rows, c_x:c_x + wx] = tail[:, GLA_GATE_RANK:GLA_GATE_RANK + wx].astype(BF16)
        o_ref[rows, 0:wv] = proj(in_v, in_g).astype(BF16)
        log_decay(0)
        g = proj(in_g, in_tail)
        o_ref[rows, wv:c_q] = (g * _sigmoid(g)).astype(BF16)
        log_decay(1)
        qk = proj(0, in_v)
        store_heads(c_q, qk[:, :wqk], dk_true ** -0.5)
        log_decay(2)
        store_heads(c_q + wk, qk[:, wqk:], 1.0)
        log_decay(3)


def _gla_in_proj(xn, w_stack, layer_j, w_gk_pad, b_gk_pad, mix_width, xa_width, dk_true, tm):
    m, k = xn.shape
    n_in = w_stack.shape[2]
    wk = w_gk_pad.shape[1]
    n_out = 2 * mix_width + 2 * wk + xa_width + 2 * wk
    return pl.pallas_call(
        functools.partial(_gla_in_proj_kernel, dk_true, mix_width, xa_width),
        grid=(m // tm,),
        in_specs=[
            pl.BlockSpec((tm, k), lambda i: (i, 0)),
            pl.BlockSpec((None, k, n_in), lambda i: (layer_j, 0, 0), pipeline_mode=RESIDENT),
            pl.BlockSpec((LANE, wk), lambda i: (0, 0)),
            pl.BlockSpec((1, wk), lambda i: (0, 0)),
        ],
        out_specs=pl.BlockSpec((tm, n_out), lambda i: (i, 0)),
        out_shape=jax.ShapeDtypeStruct((m, n_out), BF16),
        compiler_params=_params("parallel"),
        name="gla_in_proj",
    )(xn, w_stack, w_gk_pad, b_gk_pad)


def _mem_kv_kernel(x_ref, g_ref, w_ref, o_ref):
    xn = _rmsnorm_rows(x_ref[...], g_ref[...]).astype(BF16)
    o_ref[...] = jnp.dot(xn, w_ref[...].astype(BF16), preferred_element_type=F32).astype(o_ref.dtype)


def _mem_kv(mem2, gain, w_stack):
    m, k = mem2.shape
    depth, _, n = w_stack.shape
    return pl.pallas_call(
        _mem_kv_kernel,
        grid=(depth,),
        in_specs=[
            pl.BlockSpec((m, k), lambda l: (0, 0)),
            pl.BlockSpec((1, k), lambda l: (0, 0)),
            pl.BlockSpec((None, k, n), lambda l: (l, 0, 0)),
        ],
        out_specs=pl.BlockSpec((None, m, n), lambda l: (l, 0, 0)),
        out_shape=jax.ShapeDtypeStruct((depth, m, n), BF16),
        compiler_params=_params("parallel"),
        name="mem_kv_proj",
    )(mem2, gain.reshape(1, k), w_stack)


def _ffn_up_kernel(xn_ref, wg_ref, wu_ref, wd_ref, o_ref, wd_b_ref):
    wd_b_ref[...] = wd_ref[...].astype(BF16)
    wg = wg_ref[...].astype(BF16)
    wu = wu_ref[...].astype(BF16)
    for r in range(0, xn_ref.shape[0], ROW_SUBTILE):
        xn = xn_ref[r:r + ROW_SUBTILE, :]
        gate = jnp.dot(xn, wg, preferred_element_type=F32)
        up = jnp.dot(xn, wu, preferred_element_type=F32)
        o_ref[r:r + ROW_SUBTILE, :] = (gate * _sigmoid(gate) * up).astype(o_ref.dtype)


def _ffn_up(xn, w_stack, w_down_stack, layer, tm, tn):
    m, k = xn.shape
    d_ff = w_stack.shape[2] // 2
    n_blocks = d_ff // tn
    steps = (m // tm) * n_blocks
    kd, nd = w_down_stack.shape[1:]
    slice_rows = kd // steps
    assert slice_rows * steps == kd and slice_rows % BF16_SUBLANES == 0
    return pl.pallas_call(
        _ffn_up_kernel,
        grid=(m // tm, n_blocks),
        in_specs=[
            pl.BlockSpec((tm, k), lambda i, j: (i, 0)),
            pl.BlockSpec((None, k, tn), lambda i, j: (layer, 0, j)),
            pl.BlockSpec((None, k, tn), lambda i, j: (layer, 0, j + n_blocks)),
            pl.BlockSpec((None, slice_rows, nd), lambda i, j: (layer, i * n_blocks + j, 0)),
        ],
        out_specs=[pl.BlockSpec((tm, tn), lambda i, j: (i, j)),
                   pl.BlockSpec((slice_rows, nd), lambda i, j: (i * n_blocks + j, 0))],
        out_shape=[jax.ShapeDtypeStruct((m, d_ff), BF16), jax.ShapeDtypeStruct((kd, nd), BF16)],
        compiler_params=_params("parallel", "arbitrary"),
        name="ffn_up",
    )(xn, w_stack, w_stack, w_down_stack)


def _down_proj_kernel(n_casts, a_ref, w_ref, r_ref, g_ref, *refs):
    srcs, outs = refs[:n_casts], refs[n_casts:]
    out_refs, dsts = outs[:len(outs) - n_casts], outs[len(outs) - n_casts:]
    h = r_ref[...] + jnp.dot(a_ref[...], w_ref[...], preferred_element_type=F32)
    for src, dst in zip(srcs, dsts):
        dst[...] = src[...].astype(BF16)
    hn_ref = out_refs[-1]
    hn_ref[...] = _rmsnorm_rows(h, g_ref[...]).astype(hn_ref.dtype)
    if len(out_refs) == 2:
        out_refs[0][...] = h


def _down_proj(a, w, res, gain, final, tm, casts=()):
    m, k = a.shape
    n = w.shape[1]
    steps = m // tm
    rows = pl.BlockSpec((tm, n), lambda i: (i, 0))
    if final:
        out_specs, out_shape = [rows], [jax.ShapeDtypeStruct((m, n), F32)]
    else:
        out_specs = [rows, rows]
        out_shape = [jax.ShapeDtypeStruct((m, n), F32), jax.ShapeDtypeStruct((m, n), BF16)]
    cast_specs, cast_args = [], []
    for stack, idx in casts:
        kc, nc = stack.shape[1:]
        slice_rows = kc * CAST_STEPS_PER_SLICE // steps
        assert slice_rows * steps == kc * CAST_STEPS_PER_SLICE and slice_rows % BF16_SUBLANES == 0
        cast_specs.append(pl.BlockSpec(
            (None, slice_rows, nc), lambda i, idx=idx: (idx, i // CAST_STEPS_PER_SLICE, 0)))
        cast_args.append(stack)
        out_specs.append(pl.BlockSpec((slice_rows, nc), lambda i: (i // CAST_STEPS_PER_SLICE, 0)))
        out_shape.append(jax.ShapeDtypeStruct((kc, nc), BF16))
    outs = pl.pallas_call(
        functools.partial(_down_proj_kernel, len(casts)),
        grid=(steps,),
        in_specs=[
            pl.BlockSpec((tm, k), lambda i: (i, 0)),
            pl.BlockSpec((k, n), lambda i: (0, 0), pipeline_mode=RESIDENT),
            rows,
            pl.BlockSpec((1, n), lambda i: (0, 0)),
            *cast_specs,
        ],
        out_specs=out_specs,
        out_shape=out_shape,
        compiler_params=_params("arbitrary"),
        name="down_proj",
    )(a, w, res, gain.reshape(1, n), *cast_args)
    n_main = len(outs) - len(casts)
    return outs[:n_main], outs[n_main:]


def _out_proj_kernel(y_ref, xq_ref, kv_ref, w_ref, r_ref, g_ref, h_ref, hn_ref, wb_ref):
    @pl.when(pl.program_id(0) == 0)
    def _():
        wb_ref[...] = w_ref[...].astype(BF16)

    wy = y_ref.shape[1]
    d = xq_ref.shape[1] // XA_HEADS
    acc = jnp.dot(y_ref[...], wb_ref[:wy, :], preferred_element_type=F32)
    heads = []
    for hd in range(XA_HEADS):
        q = xq_ref[:, hd * d:(hd + 1) * d]
        k = kv_ref[:, hd * d:(hd + 1) * d]
        v = kv_ref[:, (XA_HEADS + hd) * d:(XA_HEADS + hd + 1) * d]
        s = lax.dot_general(q, k, NT_DIMS, preferred_element_type=F32) * (d ** -0.5)
        p = jnp.exp(s - jnp.max(s, axis=-1, keepdims=True))
        denom = jnp.sum(p, axis=-1, keepdims=True)
        o = jnp.dot(p.astype(BF16), v, preferred_element_type=F32)
        heads.append((o / denom).astype(BF16))
    acc += jnp.dot(jnp.concatenate(heads, axis=1), wb_ref[wy:, :], preferred_element_type=F32)
    h = r_ref[...] + acc
    h_ref[...] = h
    hn_ref[...] = _rmsnorm_rows(h, g_ref[...]).astype(hn_ref.dtype)


def _out_proj(y, proj, xq_block, kv_stack, w_stack, layer, res, gain, batch, tm):
    m, wy = y.shape
    _, k, n = w_stack.shape
    wx = k - wy
    n_mem, wkv = kv_stack.shape[1:]
    steps_per_batch = m // batch // tm
    return pl.pallas_call(
        _out_proj_kernel,
        grid=(m // tm,),
        in_specs=[
            pl.BlockSpec((tm, wy), lambda i: (i, 0)),
            pl.BlockSpec((tm, wx), lambda i: (i, xq_block)),
            pl.BlockSpec((None, n_mem, wkv), lambda i: (layer * batch + i // steps_per_batch, 0, 0)),
            pl.BlockSpec((None, k, n), lambda i: (layer, 0, 0), pipeline_mode=RESIDENT),
            pl.BlockSpec((tm, n), lambda i: (i, 0)),
            pl.BlockSpec((1, n), lambda i: (0, 0)),
        ],
        out_specs=[pl.BlockSpec((tm, n), lambda i: (i, 0)), pl.BlockSpec((tm, n), lambda i: (i, 0))],
        out_shape=[jax.ShapeDtypeStruct((m, n), F32), jax.ShapeDtypeStruct((m, n), BF16)],
        scratch_shapes=[pltpu.VMEM((k, n), BF16)],
        compiler_params=_params("arbitrary"),
        name="xattn_out_proj",
    )(y, proj, kv_stack, w_stack, res, gain.reshape(1, n))


def _chunk_masks():
    t = lax.broadcasted_iota(jnp.int32, (CHUNK, CHUNK), 0)
    s = lax.broadcasted_iota(jnp.int32, (CHUNK, CHUNK), 1)
    tril = (s <= t)
    return tril.astype(BF16), tril


def _chunk_cumsum(tril_b, hi, lo):
    dk = hi.shape[1]
    both = jnp.dot(tril_b, jnp.concatenate([hi, lo], axis=1), preferred_element_type=F32)
    return both[:, :dk] + both[:, dk:]


def _chunk_scores(q, k, vb, cum, sub):
    dk = q.shape[1]
    n_sub = CHUNK // sub
    blocks = [slice(sub * r, sub * (r + 1)) for r in range(n_sub)]
    mid = [cum[sub * j + sub // 2 - 1:sub * j + sub // 2, :] for j in range(n_sub)]
    last = cum[CHUNK - 1:CHUNK, :]
    zero = jnp.zeros((sub, dk), F32)

    q_slots, k_slots, k_last = [], [], []
    for j in range(n_sub):
        q_slots.append(jnp.concatenate(
            [zero if r < j else q[blocks[r], :] * jnp.exp(cum[blocks[r], :] - mid[j])
             for r in range(n_sub)], axis=0))
        k_mid = k[blocks[j], :] * jnp.exp(mid[j] - cum[blocks[j], :])
        k_slots.append(jnp.concatenate([k_mid if r == j else zero for r in range(n_sub)], axis=0))
        k_last.append(k_mid * jnp.exp(last - mid[j]))
    s_raw = lax.dot_general(jnp.concatenate(q_slots, axis=1).astype(BF16),
                            jnp.concatenate(k_slots, axis=1).astype(BF16),
                            NT_DIMS, preferred_element_type=F32)
    upd = lax.dot_general(vb, jnp.concatenate(k_last, axis=0).astype(BF16), TN_DIMS,
                          preferred_element_type=F32)
    q_dec = (q_slots[0] * jnp.exp(mid[0])).astype(BF16)
    return s_raw, upd, q_dec, jnp.exp(last), vb


def _chunk_readout(s_raw, tril, vb, q_dec, st):
    intra = jnp.dot(jnp.where(tril, s_raw, 0.0).astype(BF16), vb, preferred_element_type=F32)
    inter = lax.dot_general(q_dec, st.astype(BF16), NT_DIMS, preferred_element_type=F32)
    return intra + inter


def _gated_head_norm(o, gain, gate):
    ms = jnp.mean(o * o, axis=-1, keepdims=True)
    return o * lax.rsqrt(ms + EPS) * gain * gate


def _run_staggered(n_chunks, stages):
    n_groups = n_chunks // STAGGER_GROUP
    for slot in range(n_groups + len(stages) - 1):
        for depth in reversed(range(len(stages))):
            group = slot - depth
            if 0 <= group < n_groups:
                for c in range(group * STAGGER_GROUP, (group + 1) * STAGGER_GROUP):
                    stages[depth](c)


def _mixer_kernel(sub, q_ref, k_ref, v_ref, gate_ref, hi_ref, lo_ref, on_ref, y_ref, st_ref):
    @pl.when(pl.program_id(2) == 0)
    def _():
        st_ref[...] = jnp.zeros_like(st_ref)

    tril_b, tril = _chunk_masks()
    gain = on_ref[...]
    n_chunks = q_ref.shape[0] // CHUNK
    rows = [slice(c * CHUNK, (c + 1) * CHUNK) for c in range(n_chunks)]
    ctx = [{} for _ in range(n_chunks)]
    state = [st_ref[...]]

    def decay(c):
        ctx[c]["cum"] = _chunk_cumsum(tril_b, hi_ref[rows[c], :], lo_ref[rows[c], :])

    def scores(c):
        ctx[c]["parts"] = _chunk_scores(q_ref[rows[c], :].astype(F32), k_ref[rows[c], :].astype(F32),
                                        v_ref[rows[c], :], ctx[c].pop("cum"), sub)

    def readout(c):
        s_raw, upd, q_dec, e_last, vb = ctx[c].pop("parts")
        ctx[c]["o"] = _chunk_readout(s_raw, tril, vb, q_dec, state[0])
        state[0] = state[0] * e_last + upd

    def emit(c):
        gate = gate_ref[rows[c], :].astype(F32)
        y_ref[rows[c], :] = _gated_head_norm(ctx[c].pop("o"), gain, gate).astype(y_ref.dtype)

    _run_staggered(n_chunks, [decay, scores, readout, emit])
    st_ref[...] = state[0]


def _mixer(proj, onorm, cols, heads, dk, dv, sub, batch, t_blk, name):
    m = proj.shape[0]
    n_t = m // batch // t_blk

    def group(key, width):
        block0, rem = divmod(cols[key], width)
        assert rem == 0
        return pl.BlockSpec((t_blk, width), lambda b, h, t: (b * n_t + t, block0 + h))

    return pl.pallas_call(
        functools.partial(_mixer_kernel, sub),
        grid=(batch, heads, n_t),
        in_specs=[group("q", dk), group("k", dk), group("v", dv), group("gate", dv),
                  group("hi", dk), group("lo", dk),
                  pl.BlockSpec((1, dv), lambda b, h, t: (0, 0))],
        out_specs=pl.BlockSpec((t_blk, dv), lambda b, h, t: (b * n_t + t, h)),
        out_shape=jax.ShapeDtypeStruct((m, heads * dv), BF16),
        scratch_shapes=[pltpu.VMEM((dv, dk), F32)],
        compiler_params=_params("parallel", "parallel", "arbitrary"),
        name=name,
    )(proj, proj, proj, proj, proj, proj, onorm.reshape(1, dv))


def _gla_pad_decay_weights(w_gk, b_gk):
    heads, rank, dkp = GLA_HEADS, GLA_GATE_RANK, GLA_DK_PAD
    kw = w_gk.shape[1]
    dk = kw // heads

    def pad_heads(w):
        w = w.reshape(w.shape[0], heads, dk)
        return jnp.pad(w, ((0, 0), (0, 0), (0, dkp - dk))).reshape(w.shape[0], heads * dkp)

    w_gk_pad = jnp.pad(pad_heads(w_gk), ((0, LANE - rank), (0, 0))).astype(BF16)
    b_gk_pad = pad_heads(b_gk.reshape(1, kw)).astype(F32)
    return w_gk_pad, b_gk_pad, dk


def kernel(x, mem, norm_mix, norm_ffn, norm_mem, norm_final, hgrn_w_in, hgrn_lb_logits, hgrn_onorm,
           gla_w_in, gla_w_gk, gla_b_gk, gla_onorm, w_mem_kv, w_out, w_gate_up, w_down):
    batch, seq, d_model = x.shape
    n_mem = mem.shape[1]
    depth = norm_mix.shape[0]
    mix_width = hgrn_lb_logits.shape[1]
    xa_width = d_model - mix_width
    m = batch * seq

    h = x.reshape(m, d_model)
    kv = _mem_kv(mem.reshape(batch * n_mem, d_model), norm_mem, w_mem_kv)
    kv = kv.reshape(depth * batch, n_mem, 2 * xa_width)

    w_in_b = hgrn_w_in[0].astype(BF16)
    gla_w = gla_w_in.astype(BF16)

    for layer in range(depth):
        j = layer // 2
        if layer % 2 == 0:
            if layer == 0:
                proj = _hgrn_in_proj(h, norm_mix[0], w_in_b, j, hgrn_lb_logits, tm=256)
            else:
                proj = _hgrn_in_proj(hn, None, w_in_b, j, hgrn_lb_logits, tm=512)
            w = mix_width
            cols = dict(q=0, hi=w, lo=2 * w, k=3 * w, v=4 * w, gate=5 * w)
            y = _mixer(proj, hgrn_onorm[j], cols, mix_width // HGRN_HEAD_DIM, HGRN_HEAD_DIM,
                       HGRN_HEAD_DIM, HGRN_SUB, batch, t_blk=8192, name="hgrn2_mixer")
            xq_col = 6 * w
        else:
            w_gk_pad, b_gk_pad, dk = _gla_pad_decay_weights(gla_w_gk[j], gla_b_gk[j])
            proj = _gla_in_proj(hn, gla_w, j, w_gk_pad, b_gk_pad, mix_width, xa_width, dk, tm=512)
            wk = GLA_HEADS * GLA_DK_PAD
            xq_col = 2 * mix_width + 2 * wk
            cols = dict(v=0, gate=mix_width, q=2 * mix_width, k=2 * mix_width + wk,
                        hi=xq_col + xa_width, lo=xq_col + xa_width + wk)
            y = _mixer(proj, gla_onorm[j], cols, GLA_HEADS, GLA_DK_PAD, mix_width // GLA_HEADS,
                       GLA_SUB, batch, t_blk=4096, name="gla_mixer")
        assert xq_col % xa_width == 0
        h, hn = _out_proj(y, proj, xq_col // xa_width, kv, w_out, layer, h, norm_ffn[layer], batch,
                          tm=512)
        act, w_down_b = _ffn_up(hn, w_gate_up, w_down, layer, tm=2048, tn=512)
        if layer + 1 < depth:
            nxt = layer + 1
            casts = [(hgrn_w_in, nxt // 2)] if nxt % 2 == 0 else []
            (h, hn), w_in_next = _down_proj(
                act, w_down_b, h, norm_mix[nxt], final=False, tm=256, casts=casts)
            if w_in_next:
                w_in_b = w_in_next[0]
        else:
            (out,), _ = _down_proj(act, w_down_b, h, norm_final, final=True, tm=256)
    return out.reshape(batch, seq, d_model)
```

```python
import functools

import jax
import jax.numpy as jnp
from jax import lax
from jax.experimental import pallas as pl
from jax.experimental.pallas import tpu as pltpu

F32 = jnp.float32
BF16 = jnp.bfloat16

EPS = 1e-6
CHUNK = 64
STAGGER_GROUP = 8
ROW_SUBTILE = 256
IN_PROJ_SUBTILE = 256
DOWN_PROJ_SUBTILE = 256
CAST_STEPS_PER_SLICE = 2
BF16_SUBLANES = 16
HGRN_SUB = 32
GLA_SUB = 64
XA_HEADS = 4
HGRN_HEAD_DIM = 128
GLA_HEADS = 4
GLA_GATE_RANK = 16
GLA_GATE_NORMALIZER = 16.0
LANE = 128
GLA_DK_PAD = 256
V7X_VMEM_LIMIT = 56 * 1024 * 1024

NT_DIMS = (((1,), (1,)), ((), ()))
TN_DIMS = (((0,), (0,)), ((), ()))

RESIDENT = pl.Buffered(1)


def _params(*semantics):
    return pltpu.CompilerParams(dimension_semantics=semantics, vmem_limit_bytes=V7X_VMEM_LIMIT)


def _sigmoid(x):
    return 1.0 / (1.0 + jnp.exp(-x))


def _rmsnorm_rows(x, gain):
    ms = jnp.mean(x * x, axis=-1, keepdims=True)
    return x * lax.rsqrt(ms + EPS) * gain


def _split_hi_lo(g):
    hi = g.astype(BF16)
    return hi, (g - hi.astype(F32)).astype(BF16)


def _hgrn_in_proj_kernel(layer_j, normalise, xn_ref, *refs):
    gn_ref = refs[0] if normalise else None
    w_ref, lbl_ref, o_ref = refs[-3:]
    w = lbl_ref.shape[1]

    logits = lbl_ref[...]
    ex = jnp.exp(logits - jnp.max(logits, axis=0, keepdims=True))
    sm = ex / jnp.sum(ex, axis=0, keepdims=True)
    lb = jnp.zeros((1, w), F32)
    for r in range(1, layer_j + 1):
        lb = lb + sm[r:r + 1, :]

    for r in range(0, xn_ref.shape[0], IN_PROJ_SUBTILE):
        rows = slice(r, r + IN_PROJ_SUBTILE)
        xn = xn_ref[rows, :]
        if normalise:
            xn = _rmsnorm_rows(xn, gn_ref[...]).astype(BF16)

        def proj(lo, hi):
            return jnp.dot(xn, w_ref[:, lo:hi], preferred_element_type=F32)

        o_ref[rows, 4 * w:5 * w] = proj(2 * w, 3 * w).astype(BF16)
        f = lb + (1.0 - lb) * _sigmoid(proj(w, 2 * w))
        hi, lo = _split_hi_lo(jnp.log(f))
        o_ref[rows, w:2 * w] = hi
        o_ref[rows, 2 * w:3 * w] = lo
        o_ref[rows, 3 * w:4 * w] = (1.0 - f).astype(BF16)
        q = proj(0, w)
        o_ref[rows, 0:w] = (q * _sigmoid(q)).astype(BF16)
        g = proj(3 * w, 4 * w)
        o_ref[rows, 5 * w:6 * w] = (g * _sigmoid(g)).astype(BF16)
        o_ref[rows, 6 * w:] = proj(4 * w, w_ref.shape[1]).astype(BF16)


def _hgrn_in_proj(xn, gain, w, layer_j, lb_logits, tm):
    m, k = xn.shape
    n_in = w.shape[1]
    n_a, mix_width = lb_logits.shape
    n_out = n_in + 2 * mix_width
    normalise = gain is not None
    norm_specs = [pl.BlockSpec((1, k), lambda i: (0, 0))] if normalise else []
    norm_args = [gain.reshape(1, k)] if normalise else []
    return pl.pallas_call(
        functools.partial(_hgrn_in_proj_kernel, layer_j, normalise),
        grid=(m // tm,),
        in_specs=[
            pl.BlockSpec((tm, k), lambda i: (i, 0)),
            *norm_specs,
            pl.BlockSpec((k, n_in), lambda i: (0, 0), pipeline_mode=RESIDENT),
            pl.BlockSpec((n_a, mix_width), lambda i: (0, 0)),
        ],
        out_specs=pl.BlockSpec((tm, n_out), lambda i: (i, 0)),
        out_shape=jax.ShapeDtypeStruct((m, n_out), BF16),
        compiler_params=_params("parallel"),
        name="hgrn_in_proj",
    )(xn, *norm_args, w, lb_logits)


def _gla_in_proj_kernel(dk_true, wv, wx, xn_ref, w_ref, wgk_ref, bgk_ref, o_ref):
    dkp = GLA_DK_PAD
    wk = wgk_ref.shape[1]
    wqk = GLA_HEADS * dk_true
    in_v, in_g, in_tail = 2 * wqk, 2 * wqk + wv, 2 * wqk + 2 * wv
    c_q = 2 * wv
    c_x = c_q + 2 * wk
    zero_pad = jnp.zeros((IN_PROJ_SUBTILE, dkp - dk_true), BF16)

    for r in range(0, xn_ref.shape[0], IN_PROJ_SUBTILE):
        rows = slice(r, r + IN_PROJ_SUBTILE)
        xn = xn_ref[rows, :]

        def proj(lo, hi):
            return jnp.dot(xn, w_ref[:, lo:hi], preferred_element_type=F32)

        def log_decay(head):
            cols = slice(head * dkp, (head + 1) * dkp)
            x = jnp.dot(low, wgk_ref[:, cols], preferred_element_type=F32) + bgk_ref[:, cols]
            log_a = (jnp.minimum(x, 0.0) - jnp.log1p(jnp.exp(-jnp.abs(x)))) / GLA_GATE_NORMALIZER
            hi, lo = _split_hi_lo(log_a)
            o_ref[rows, c_x + wx + head * dkp:c_x + wx + (head + 1) * dkp] = hi
            o_ref[rows, c_x + wx + wk + head * dkp:c_x + wx + wk + (head + 1) * dkp] = lo

        def store_heads(col0, values, scale):
            for hd in range(GLA_HEADS):
                piece = values[:, hd * dk_true:(hd + 1) * dk_true]
                o_ref[rows, col0 + hd * dkp:col0 + hd * dkp + dk_true] = (piece * scale).astype(BF16)
                o_ref[rows, col0 + hd * dkp + dk_true:col0 + (hd + 1) * dkp] = zero_pad

        tail = proj(in_tail, w_ref.shape[1])
        low = tail[:, :LANE].astype(BF16)
        o_ref[rows, c_x:c_x + wx] = tail[:, GLA_GATE_RANK:GLA_GATE_RANK + wx].astype(BF16)
        o_ref[rows, 0:wv] = proj(in_v, in_g).astype(BF16)
        log_decay(0)
        g = proj(in_g, in_tail)
        o_ref[rows, wv:c_q] = (g * _sigmoid(g)).astype(BF16)
        log_decay(1)
        qk = proj(0, in_v)
        store_heads(c_q, qk[:, :wqk], dk_true ** -0.5)
        log_decay(2)
        store_heads(c_q + wk, qk[:, wqk:], 1.0)
        log_decay(3)


def _gla_in_proj(xn, w_stack, layer_j, w_gk_pad, b_gk_pad, mix_width, xa_width, dk_true, tm):
    m, k = xn.shape
    n_in = w_stack.shape[2]
    wk = w_gk_pad.shape[1]
    n_out = 2 * mix_width + 2 * wk + xa_width + 2 * wk
    return pl.pallas_call(
        functools.partial(_gla_in_proj_kernel, dk_true, mix_width, xa_width),
        grid=(m // tm,),
        in_specs=[
            pl.BlockSpec((tm, k), lambda i: (i, 0)),
            pl.BlockSpec((None, k, n_in), lambda i: (layer_j, 0, 0), pipeline_mode=RESIDENT),
            pl.BlockSpec((LANE, wk), lambda i: (0, 0)),
            pl.BlockSpec((1, wk), lambda i: (0, 0)),
        ],
        out_specs=pl.BlockSpec((tm, n_out), lambda i: (i, 0)),
        out_shape=jax.ShapeDtypeStruct((m, n_out), BF16),
        compiler_params=_params("parallel"),
        name="gla_in_proj",
    )(xn, w_stack, w_gk_pad, b_gk_pad)


def _mem_kv_kernel(x_ref, g_ref, w_ref, o_ref):
    xn = _rmsnorm_rows(x_ref[...], g_ref[...]).astype(BF16)
    o_ref[...] = jnp.dot(xn, w_ref[...].astype(BF16), preferred_element_type=F32).astype(o_ref.dtype)


def _mem_kv(mem2, gain, w_stack):
    m, k = mem2.shape
    depth, _, n = w_stack.shape
    return pl.pallas_call(
        _mem_kv_kernel,
        grid=(depth,),
        in_specs=[
            pl.BlockSpec((m, k), lambda l: (0, 0)),
            pl.BlockSpec((1, k), lambda l: (0, 0)),
            pl.BlockSpec((None, k, n), lambda l: (l, 0, 0)),
        ],
        out_specs=pl.BlockSpec((None, m, n), lambda l: (l, 0, 0)),
        out_shape=jax.ShapeDtypeStruct((depth, m, n), BF16),
        compiler_params=_params("parallel"),
        name="mem_kv_proj",
    )(mem2, gain.reshape(1, k), w_stack)


def _ffn_up_kernel(xn_ref, wg_ref, wu_ref, wd_ref, o_ref, wd_b_ref):
    wd_b_ref[...] = wd_ref[...].astype(BF16)
    wg = wg_ref[...].astype(BF16)
    wu = wu_ref[...].astype(BF16)
    for r in range(0, xn_ref.shape[0], ROW_SUBTILE):
        xn = xn_ref[r:r + ROW_SUBTILE, :]
        gate = jnp.dot(xn, wg, preferred_element_type=F32)
        up = jnp.dot(xn, wu, preferred_element_type=F32)
        o_ref[r:r + ROW_SUBTILE, :] = (gate * _sigmoid(gate) * up).astype(o_ref.dtype)


def _ffn_up(xn, w_stack, w_down_stack, layer, tm, tn):
    m, k = xn.shape
    d_ff = w_stack.shape[2] // 2
    n_blocks = d_ff // tn
    steps = (m // tm) * n_blocks
    kd, nd = w_down_stack.shape[1:]
    slice_rows = kd // steps
    assert slice_rows * steps == kd and slice_rows % BF16_SUBLANES == 0
    return pl.pallas_call(
        _ffn_up_kernel,
        grid=(m // tm, n_blocks),
        in_specs=[
            pl.BlockSpec((tm, k), lambda i, j: (i, 0)),
            pl.BlockSpec((None, k, tn), lambda i, j: (layer, 0, j)),
            pl.BlockSpec((None, k, tn), lambda i, j: (layer, 0, j + n_blocks)),
            pl.BlockSpec((None, slice_rows, nd), lambda i, j: (layer, i * n_blocks + j, 0)),
        ],
        out_specs=[pl.BlockSpec((tm, tn), lambda i, j: (i, j)),
                   pl.BlockSpec((slice_rows, nd), lambda i, j: (i * n_blocks + j, 0))],
        out_shape=[jax.ShapeDtypeStruct((m, d_ff), BF16), jax.ShapeDtypeStruct((kd, nd), BF16)],
        compiler_params=_params("parallel", "arbitrary"),
        name="ffn_up",
    )(xn, w_stack, w_stack, w_down_stack)


def _down_proj_kernel(n_casts, a_ref, w_ref, r_ref, g_ref, *refs):
    srcs, outs = refs[:n_casts], refs[n_casts:]
    out_refs, dsts = outs[:len(outs) - n_casts], outs[len(outs) - n_casts:]
    for src, dst in zip(srcs, dsts):
        dst[...] = src[...].astype(BF16)
    hn_ref = out_refs[-1]
    for r in range(0, a_ref.shape[0], DOWN_PROJ_SUBTILE):
        rows = slice(r, r + DOWN_PROJ_SUBTILE)
        h = r_ref[rows, :] + jnp.dot(a_ref[rows, :], w_ref[...], preferred_element_type=F32)
        hn_ref[rows, :] = _rmsnorm_rows(h, g_ref[...]).astype(hn_ref.dtype)
        if len(out_refs) == 2:
            out_refs[0][rows, :] = h


def _down_proj(a, w, res, gain, final, tm, casts=()):
    m, k = a.shape
    n = w.shape[1]
    steps = m // tm
    rows = pl.BlockSpec((tm, n), lambda i: (i, 0))
    if final:
        out_specs, out_shape = [rows], [jax.ShapeDtypeStruct((m, n), F32)]
    else:
        out_specs = [rows, rows]
        out_shape = [jax.ShapeDtypeStruct((m, n), F32), jax.ShapeDtypeStruct((m, n), BF16)]
    cast_specs, cast_args = [], []
    for stack, idx in casts:
        kc, nc = stack.shape[1:]
        slice_rows = kc * CAST_STEPS_PER_SLICE // steps
        assert slice_rows * steps == kc * CAST_STEPS_PER_SLICE and slice_rows % BF16_SUBLANES == 0
        cast_specs.append(pl.BlockSpec(
            (None, slice_rows, nc), lambda i, idx=idx: (idx, i // CAST_STEPS_PER_SLICE, 0)))
        cast_args.append(stack)
        out_specs.append(pl.BlockSpec((slice_rows, nc), lambda i: (i // CAST_STEPS_PER_SLICE, 0)))
        out_shape.append(jax.ShapeDtypeStruct((kc, nc), BF16))
    outs = pl.pallas_call(
        functools.partial(_down_proj_kernel, len(casts)),
        grid=(steps,),
        in_specs=[
            pl.BlockSpec((tm, k), lambda i: (i, 0)),
            pl.BlockSpec((k, n), lambda i: (0, 0), pipeline_mode=RESIDENT),
            rows,
            pl.BlockSpec((1, n), lambda i: (0, 0)),
            *cast_specs,
        ],
        out_specs=out_specs,
        out_shape=out_shape,
        compiler_params=_params("arbitrary"),
        name="down_proj",
    )(a, w, res, gain.reshape(1, n), *cast_args)
    n_main = len(outs) - len(casts)
    return outs[:n_main], outs[n_main:]


def _out_proj_kernel(y_ref, xq_ref, kv_ref, w_ref, r_ref, g_ref, h_ref, hn_ref, wb_ref):
    @pl.when(pl.program_id(0) == 0)
    def _():
        wb_ref[...] = w_ref[...].astype(BF16)

    wy = y_ref.shape[1]
    d = xq_ref.shape[1] // XA_HEADS
    acc = jnp.dot(y_ref[...], wb_ref[:wy, :], preferred_element_type=F32)
    heads = []
    for hd in range(XA_HEADS):
        q = xq_ref[:, hd * d:(hd + 1) * d]
        k = kv_ref[:, hd * d:(hd + 1) * d]
        v = kv_ref[:, (XA_HEADS + hd) * d:(XA_HEADS + hd + 1) * d]
        s = lax.dot_general(q, k, NT_DIMS, preferred_element_type=F32) * (d ** -0.5)
        p = jnp.exp(s - jnp.max(s, axis=-1, keepdims=True))
        denom = jnp.sum(p, axis=-1, keepdims=True)
        o = jnp.dot(p.astype(BF16), v, preferred_element_type=F32)
        heads.append((o / denom).astype(BF16))
    acc += jnp.dot(jnp.concatenate(heads, axis=1), wb_ref[wy:, :], preferred_element_type=F32)
    h = r_ref[...] + acc
    h_ref[...] = h
    hn_ref[...] = _rmsnorm_rows(h, g_ref[...]).astype(hn_ref.dtype)


def _out_proj(y, proj, xq_block, kv_stack, w_stack, layer, res, gain, batch, tm):
    m, wy = y.shape
    _, k, n = w_stack.shape
    wx = k - wy
    n_mem, wkv = kv_stack.shape[1:]
    steps_per_batch = m // batch // tm
    return pl.pallas_call(
        _out_proj_kernel,
        grid=(m // tm,),
        in_specs=[
            pl.BlockSpec((tm, wy), lambda i: (i, 0)),
            pl.BlockSpec((tm, wx), lambda i: (i, xq_block)),
            pl.BlockSpec((None, n_mem, wkv), lambda i: (layer * batch + i // steps_per_batch, 0, 0)),
            pl.BlockSpec((None, k, n), lambda i: (layer, 0, 0), pipeline_mode=RESIDENT),
            pl.BlockSpec((tm, n), lambda i: (i, 0)),
            pl.BlockSpec((1, n), lambda i: (0, 0)),
        ],
        out_specs=[pl.BlockSpec((tm, n), lambda i: (i, 0)), pl.BlockSpec((tm, n), lambda i: (i, 0))],
        out_shape=[jax.ShapeDtypeStruct((m, n), F32), jax.ShapeDtypeStruct((m, n), BF16)],
        scratch_shapes=[pltpu.VMEM((k, n), BF16)],
        compiler_params=_params("arbitrary"),
        name="xattn_out_proj",
    )(y, proj, kv_stack, w_stack, res, gain.reshape(1, n))


def _chunk_masks():
    t = lax.broadcasted_iota(jnp.int32, (CHUNK, CHUNK), 0)
    s = lax.broadcasted_iota(jnp.int32, (CHUNK, CHUNK), 1)
    tril = (s <= t)
    return tril.astype(BF16), tril


def _chunk_cumsum(tril_b, hi, lo):
    dk = hi.shape[1]
    both = jnp.dot(tril_b, jnp.concatenate([hi, lo], axis=1), preferred_element_type=F32)
    return both[:, :dk] + both[:, dk:]


def _chunk_scores(q, k, vb, cum, sub):
    dk = q.shape[1]
    n_sub = CHUNK // sub
    blocks = [slice(sub * r, sub * (r + 1)) for r in range(n_sub)]
    mid = [cum[sub * j + sub // 2 - 1:sub * j + sub // 2, :] for j in range(n_sub)]
    last = cum[CHUNK - 1:CHUNK, :]
    zero = jnp.zeros((sub, dk), F32)

    q_slots, k_slots, k_last = [], [], []
    for j in range(n_sub):
        q_slots.append(jnp.concatenate(
            [zero if r < j else q[blocks[r], :] * jnp.exp(cum[blocks[r], :] - mid[j])
             for r in range(n_sub)], axis=0))
        k_mid = k[blocks[j], :] * jnp.exp(mid[j] - cum[blocks[j], :])
        k_slots.append(jnp.concatenate([k_mid if r == j else zero for r in range(n_sub)], axis=0))
        k_last.append(k_mid * jnp.exp(last - mid[j]))
    s_raw = lax.dot_general(jnp.concatenate(q_slots, axis=1).astype(BF16),
                            jnp.concatenate(k_slots, axis=1).astype(BF16),
                            NT_DIMS, preferred_element_type=F32)
    upd = lax.dot_general(vb, jnp.concatenate(k_last, axis=0).astype(BF16), TN_DIMS,
                          preferred_element_type=F32)
    q_dec = (q_slots[0] * jnp.exp(mid[0])).astype(BF16)
    return s_raw, upd, q_dec, jnp.exp(last), vb


def _chunk_readout(s_raw, tril, vb, q_dec, st):
    intra = jnp.dot(jnp.where(tril, s_raw, 0.0).astype(BF16), vb, preferred_element_type=F32)
    inter = lax.dot_general(q_dec, st.astype(BF16), NT_DIMS, preferred_element_type=F32)
    return intra + inter


def _gated_head_norm(o, gain, gate):
    ms = jnp.mean(o * o, axis=-1, keepdims=True)
    return o * lax.rsqrt(ms + EPS) * gain * gate


def _run_staggered(n_chunks, stages):
    n_groups = n_chunks // STAGGER_GROUP
    for slot in range(n_groups + len(stages) - 1):
        for depth in reversed(range(len(stages))):
            group = slot - depth
            if 0 <= group < n_groups:
                for c in range(group * STAGGER_GROUP, (group + 1) * STAGGER_GROUP):
                    stages[depth](c)


def _mixer_kernel(sub, q_ref, k_ref, v_ref, gate_ref, hi_ref, lo_ref, on_ref, y_ref, st_ref):
    @pl.when(pl.program_id(2) == 0)
    def _():
        st_ref[...] = jnp.zeros_like(st_ref)

    tril_b, tril = _chunk_masks()
    gain = on_ref[...]
    n_chunks = q_ref.shape[0] // CHUNK
    rows = [slice(c * CHUNK, (c + 1) * CHUNK) for c in range(n_chunks)]
    ctx = [{} for _ in range(n_chunks)]
    state = [st_ref[...]]

    def decay(c):
        ctx[c]["cum"] = _chunk_cumsum(tril_b, hi_ref[rows[c], :], lo_ref[rows[c], :])

    def scores(c):
        ctx[c]["parts"] = _chunk_scores(q_ref[rows[c], :].astype(F32), k_ref[rows[c], :].astype(F32),
                                        v_ref[rows[c], :], ctx[c].pop("cum"), sub)

    def readout(c):
        s_raw, upd, q_dec, e_last, vb = ctx[c].pop("parts")
        ctx[c]["o"] = _chunk_readout(s_raw, tril, vb, q_dec, state[0])
        state[0] = state[0] * e_last + upd

    def emit(c):
        gate = gate_ref[rows[c], :].astype(F32)
        y_ref[rows[c], :] = _gated_head_norm(ctx[c].pop("o"), gain, gate).astype(y_ref.dtype)

    _run_staggered(n_chunks, [decay, scores, readout, emit])
    st_ref[...] = state[0]


def _mixer(proj, onorm, cols, heads, dk, dv, sub, batch, t_blk, name):
    m = proj.shape[0]
    n_t = m // batch // t_blk

    def group(key, width):
        block0, rem = divmod(cols[key], width)
        assert rem == 0
        return pl.BlockSpec((t_blk, width), lambda b, h, t: (b * n_t + t, block0 + h))

    return pl.pallas_call(
        functools.partial(_mixer_kernel, sub),
        grid=(batch, heads, n_t),
        in_specs=[group("q", dk), group("k", dk), group("v", dv), group("gate", dv),
                  group("hi", dk), group("lo", dk),
                  pl.BlockSpec((1, dv), lambda b, h, t: (0, 0))],
        out_specs=pl.BlockSpec((t_blk, dv), lambda b, h, t: (b * n_t + t, h)),
        out_shape=jax.ShapeDtypeStruct((m, heads * dv), BF16),
        scratch_shapes=[pltpu.VMEM((dv, dk), F32)],
        compiler_params=_params("parallel", "parallel", "arbitrary"),
        name=name,
    )(proj, proj, proj, proj, proj, proj, onorm.reshape(1, dv))


def _gla_pad_decay_weights(w_gk, b_gk):
    heads, rank, dkp = GLA_HEADS, GLA_GATE_RANK, GLA_DK_PAD
    kw = w_gk.shape[1]
    dk = kw // heads

    def pad_heads(w):
        w = w.reshape(w.shape[0], heads, dk)
        return jnp.pad(w, ((0, 0), (0, 0), (0, dkp - dk))).reshape(w.shape[0], heads * dkp)

    w_gk_pad = jnp.pad(pad_heads(w_gk), ((0, LANE - rank), (0, 0))).astype(BF16)
    b_gk_pad = pad_heads(b_gk.reshape(1, kw)).astype(F32)
    return w_gk_pad, b_gk_pad, dk


def kernel(x, mem, norm_mix, norm_ffn, norm_mem, norm_final, hgrn_w_in, hgrn_lb_logits, hgrn_onorm,
           gla_w_in, gla_w_gk, gla_b_gk, gla_onorm, w_mem_kv, w_out, w_gate_up, w_down):
    batch, seq, d_model = x.shape
    n_mem = mem.shape[1]
    depth = norm_mix.shape[0]
    mix_width = hgrn_lb_logits.shape[1]
    xa_width = d_model - mix_width
    m = batch * seq

    h = x.reshape(m, d_model)
    kv = _mem_kv(mem.reshape(batch * n_mem, d_model), norm_mem, w_mem_kv)
    kv = kv.reshape(depth * batch, n_mem, 2 * xa_width)

    w_in_b = hgrn_w_in[0].astype(BF16)
    gla_w = gla_w_in.astype(BF16)

    for layer in range(depth):
        j = layer // 2
        if layer % 2 == 0:
            if layer == 0:
                proj = _hgrn_in_proj(h, norm_mix[0], w_in_b, j, hgrn_lb_logits, tm=256)
            else:
                proj = _hgrn_in_proj(hn, None, w_in_b, j, hgrn_lb_logits, tm=512)
            w = mix_width
            cols = dict(q=0, hi=w, lo=2 * w, k=3 * w, v=4 * w, gate=5 * w)
            y = _mixer(proj, hgrn_onorm[j], cols, mix_width // HGRN_HEAD_DIM, HGRN_HEAD_DIM,
                       HGRN_HEAD_DIM, HGRN_SUB, batch, t_blk=8192, name="hgrn2_mixer")
            xq_col = 6 * w
        else:
            w_gk_pad, b_gk_pad, dk = _gla_pad_decay_weights(gla_w_gk[j], gla_b_gk[j])
            proj = _gla_in_proj(hn, gla_w, j, w_gk_pad, b_gk_pad, mix_width, xa_width, dk, tm=512)
            wk = GLA_HEADS * GLA_DK_PAD
            xq_col = 2 * mix_width + 2 * wk
            cols = dict(v=0, gate=mix_width, q=2 * mix_width, k=2 * mix_width + wk,
                        hi=xq_col + xa_width, lo=xq_col + xa_width + wk)
            y = _mixer(proj, gla_onorm[j], cols, GLA_HEADS, GLA_DK_PAD, mix_width // GLA_HEADS,
                       GLA_SUB, batch, t_blk=4096, name="gla_mixer")
        assert xq_col % xa_width == 0
        h, hn = _out_proj(y, proj, xq_col // xa_width, kv, w_out, layer, h, norm_ffn[layer], batch,
                          tm=512)
        act, w_down_b = _ffn_up(hn, w_gate_up, w_down, layer, tm=2048, tn=512)
        if layer + 1 < depth:
            nxt = layer + 1
            casts = [(hgrn_w_in, nxt // 2)] if nxt % 2 == 0 else []
            (h, hn), w_in_next = _down_proj(
                act, w_down_b, h, norm_mix[nxt], final=False, tm=256, casts=casts)
            if w_in_next:
                w_in_b = w_in_next[0]
        else:
            (out,), _ = _down_proj(act, w_down_b, h, norm_final, final=True, tm=512)
    return out.reshape(batch, seq, d_model)
```

```python
import functools

import jax
import jax.numpy as jnp
from jax import lax
from jax.experimental import pallas as pl
from jax.experimental.pallas import tpu as pltpu

F32 = jnp.float32
BF16 = jnp.bfloat16

EPS = 1e-6
CHUNK = 64
STAGGER_GROUP = 8
ROW_SUBTILE = 256
IN_PROJ_SUBTILE = 256
DOWN_PROJ_SUBTILE = 256
OUT_PROJ_SUBTILE = 256
CAST_STEPS_PER_SLICE = 2
BF16_SUBLANES = 16
HGRN_SUB = 32
GLA_SUB = 64
XA_HEADS = 4
HGRN_HEAD_DIM = 128
GLA_HEADS = 4
GLA_GATE_RANK = 16
GLA_GATE_NORMALIZER = 16.0
LANE = 128
GLA_DK_PAD = 256
V7X_VMEM_LIMIT = 56 * 1024 * 1024

NT_DIMS = (((1,), (1,)), ((), ()))
TN_DIMS = (((0,), (0,)), ((), ()))

RESIDENT = pl.Buffered(1)


def _params(*semantics):
    return pltpu.CompilerParams(dimension_semantics=semantics, vmem_limit_bytes=V7X_VMEM_LIMIT)


def _sigmoid(x):
    return 1.0 / (1.0 + jnp.exp(-x))


def _rmsnorm_rows(x, gain):
    ms = jnp.mean(x * x, axis=-1, keepdims=True)
    return x * lax.rsqrt(ms + EPS) * gain


def _split_hi_lo(g):
    hi = g.astype(BF16)
    return hi, (g - hi.astype(F32)).astype(BF16)


def _hgrn_in_proj_kernel(layer_j, normalise, xn_ref, *refs):
    gn_ref = refs[0] if normalise else None
    w_ref, lbl_ref, o_ref = refs[-3:]
    w = lbl_ref.shape[1]

    logits = lbl_ref[...]
    ex = jnp.exp(logits - jnp.max(logits, axis=0, keepdims=True))
    sm = ex / jnp.sum(ex, axis=0, keepdims=True)
    lb = jnp.zeros((1, w), F32)
    for r in range(1, layer_j + 1):
        lb = lb + sm[r:r + 1, :]

    for r in range(0, xn_ref.shape[0], IN_PROJ_SUBTILE):
        rows = slice(r, r + IN_PROJ_SUBTILE)
        xn = xn_ref[rows, :]
        if normalise:
            xn = _rmsnorm_rows(xn, gn_ref[...]).astype(BF16)

        def proj(lo, hi):
            return jnp.dot(xn, w_ref[:, lo:hi], preferred_element_type=F32)

        o_ref[rows, 4 * w:5 * w] = proj(2 * w, 3 * w).astype(BF16)
        f = lb + (1.0 - lb) * _sigmoid(proj(w, 2 * w))
        hi, lo = _split_hi_lo(jnp.log(f))
        o_ref[rows, w:2 * w] = hi
        o_ref[rows, 2 * w:3 * w] = lo
        o_ref[rows, 3 * w:4 * w] = (1.0 - f).astype(BF16)
        q = proj(0, w)
        o_ref[rows, 0:w] = (q * _sigmoid(q)).astype(BF16)
        g = proj(3 * w, 4 * w)
        o_ref[rows, 5 * w:6 * w] = (g * _sigmoid(g)).astype(BF16)
        o_ref[rows, 6 * w:] = proj(4 * w, w_ref.shape[1]).astype(BF16)


def _hgrn_in_proj(xn, gain, w, layer_j, lb_logits, tm):
    m, k = xn.shape
    n_in = w.shape[1]
    n_a, mix_width = lb_logits.shape
    n_out = n_in + 2 * mix_width
    normalise = gain is not None
    norm_specs = [pl.BlockSpec((1, k), lambda i: (0, 0))] if normalise else []
    norm_args = [gain.reshape(1, k)] if normalise else []
    return pl.pallas_call(
        functools.partial(_hgrn_in_proj_kernel, layer_j, normalise),
        grid=(m // tm,),
        in_specs=[
            pl.BlockSpec((tm, k), lambda i: (i, 0)),
            *norm_specs,
            pl.BlockSpec((k, n_in), lambda i: (0, 0), pipeline_mode=RESIDENT),
            pl.BlockSpec((n_a, mix_width), lambda i: (0, 0)),
        ],
        out_specs=pl.BlockSpec((tm, n_out), lambda i: (i, 0)),
        out_shape=jax.ShapeDtypeStruct((m, n_out), BF16),
        compiler_params=_params("parallel"),
        name="hgrn_in_proj",
    )(xn, *norm_args, w, lb_logits)


def _gla_in_proj_kernel(dk_true, wv, wx, xn_ref, w_ref, wgk_ref, bgk_ref, o_ref):
    dkp = GLA_DK_PAD
    wk = wgk_ref.shape[1]
    wqk = GLA_HEADS * dk_true
    in_v, in_g, in_tail = 2 * wqk, 2 * wqk + wv, 2 * wqk + 2 * wv
    c_q = 2 * wv
    c_x = c_q + 2 * wk
    zero_pad = jnp.zeros((IN_PROJ_SUBTILE, dkp - dk_true), BF16)

    for r in range(0, xn_ref.shape[0], IN_PROJ_SUBTILE):
        rows = slice(r, r + IN_PROJ_SUBTILE)
        xn = xn_ref[rows, :]

        def proj(lo, hi):
            return jnp.dot(xn, w_ref[:, lo:hi], preferred_element_type=F32)

        def log_decay(head):
            cols = slice(head * dkp, (head + 1) * dkp)
            x = jnp.dot(low, wgk_ref[:, cols], preferred_element_type=F32) + bgk_ref[:, cols]
            log_a = (jnp.minimum(x, 0.0) - jnp.log1p(jnp.exp(-jnp.abs(x)))) / GLA_GATE_NORMALIZER
            hi, lo = _split_hi_lo(log_a)
            o_ref[rows, c_x + wx + head * dkp:c_x + wx + (head + 1) * dkp] = hi
            o_ref[rows, c_x + wx + wk + head * dkp:c_x + wx + wk + (head + 1) * dkp] = lo

        def store_heads(col0, values, scale):
            for hd in range(GLA_HEADS):
                piece = values[:, hd * dk_true:(hd + 1) * dk_true]
                o_ref[rows, col0 + hd * dkp:col0 + hd * dkp + dk_true] = (piece * scale).astype(BF16)
                o_ref[rows, col0 + hd * dkp + dk_true:col0 + (hd + 1) * dkp] = zero_pad

        tail = proj(in_tail, w_ref.shape[1])
        low = tail[:, :LANE].astype(BF16)
        o_ref[rows, c_x:c_x + wx] = tail[:, GLA_GATE_RANK:GLA_GATE_RANK + wx].astype(BF16)
        o_ref[rows, 0:wv] = proj(in_v, in_g).astype(BF16)
        log_decay(0)
        g = proj(in_g, in_tail)
        o_ref[rows, wv:c_q] = (g * _sigmoid(g)).astype(BF16)
        log_decay(1)
        qk = proj(0, in_v)
        store_heads(c_q, qk[:, :wqk], dk_true ** -0.5)
        log_decay(2)
        store_heads(c_q + wk, qk[:, wqk:], 1.0)
        log_decay(3)


def _gla_in_proj(xn, w_stack, layer_j, w_gk_pad, b_gk_pad, mix_width, xa_width, dk_true, tm):
    m, k = xn.shape
    n_in = w_stack.shape[2]
    wk = w_gk_pad.shape[1]
    n_out = 2 * mix_width + 2 * wk + xa_width + 2 * wk
    return pl.pallas_call(
        functools.partial(_gla_in_proj_kernel, dk_true, mix_width, xa_width),
        grid=(m // tm,),
        in_specs=[
            pl.BlockSpec((tm, k), lambda i: (i, 0)),
            pl.BlockSpec((None, k, n_in), lambda i: (layer_j, 0, 0), pipeline_mode=RESIDENT),
            pl.BlockSpec((LANE, wk), lambda i: (0, 0)),
            pl.BlockSpec((1, wk), lambda i: (0, 0)),
        ],
        out_specs=pl.BlockSpec((tm, n_out), lambda i: (i, 0)),
        out_shape=jax.ShapeDtypeStruct((m, n_out), BF16),
        compiler_params=_params("parallel"),
        name="gla_in_proj",
    )(xn, w_stack, w_gk_pad, b_gk_pad)


def _mem_kv_kernel(x_ref, g_ref, w_ref, o_ref):
    xn = _rmsnorm_rows(x_ref[...], g_ref[...]).astype(BF16)
    o_ref[...] = jnp.dot(xn, w_ref[...].astype(BF16), preferred_element_type=F32).astype(o_ref.dtype)


def _mem_kv(mem2, gain, w_stack):
    m, k = mem2.shape
    depth, _, n = w_stack.shape
    return pl.pallas_call(
        _mem_kv_kernel,
        grid=(depth,),
        in_specs=[
            pl.BlockSpec((m, k), lambda l: (0, 0)),
            pl.BlockSpec((1, k), lambda l: (0, 0)),
            pl.BlockSpec((None, k, n), lambda l: (l, 0, 0)),
        ],
        out_specs=pl.BlockSpec((None, m, n), lambda l: (l, 0, 0)),
        out_shape=jax.ShapeDtypeStruct((depth, m, n), BF16),
        compiler_params=_params("parallel"),
        name="mem_kv_proj",
    )(mem2, gain.reshape(1, k), w_stack)


def _ffn_up_kernel(xn_ref, wg_ref, wu_ref, wd_ref, o_ref, wd_b_ref):
    wd_b_ref[...] = wd_ref[...].astype(BF16)
    wg = wg_ref[...].astype(BF16)
    wu = wu_ref[...].astype(BF16)
    for r in range(0, xn_ref.shape[0], ROW_SUBTILE):
        xn = xn_ref[r:r + ROW_SUBTILE, :]
        gate = jnp.dot(xn, wg, preferred_element_type=F32)
        up = jnp.dot(xn, wu, preferred_element_type=F32)
        o_ref[r:r + ROW_SUBTILE, :] = (gate * _sigmoid(gate) * up).astype(o_ref.dtype)


def _ffn_up(xn, w_stack, w_down_stack, layer, tm, tn):
    m, k = xn.shape
    d_ff = w_stack.shape[2] // 2
    n_blocks = d_ff // tn
    steps = (m // tm) * n_blocks
    kd, nd = w_down_stack.shape[1:]
    slice_rows = kd // steps
    assert slice_rows * steps == kd and slice_rows % BF16_SUBLANES == 0
    return pl.pallas_call(
        _ffn_up_kernel,
        grid=(m // tm, n_blocks),
        in_specs=[
            pl.BlockSpec((tm, k), lambda i, j: (i, 0)),
            pl.BlockSpec((None, k, tn), lambda i, j: (layer, 0, j)),
            pl.BlockSpec((None, k, tn), lambda i, j: (layer, 0, j + n_blocks)),
            pl.BlockSpec((None, slice_rows, nd), lambda i, j: (layer, i * n_blocks + j, 0)),
        ],
        out_specs=[pl.BlockSpec((tm, tn), lambda i, j: (i, j)),
                   pl.BlockSpec((slice_rows, nd), lambda i, j: (i * n_blocks + j, 0))],
        out_shape=[jax.ShapeDtypeStruct((m, d_ff), BF16), jax.ShapeDtypeStruct((kd, nd), BF16)],
        compiler_params=_params("parallel", "arbitrary"),
        name="ffn_up",
    )(xn, w_stack, w_stack, w_down_stack)


def _down_proj_kernel(n_casts, a_ref, w_ref, r_ref, g_ref, *refs):
    srcs, outs = refs[:n_casts], refs[n_casts:]
    out_refs, dsts = outs[:len(outs) - n_casts], outs[len(outs) - n_casts:]
    for src, dst in zip(srcs, dsts):
        dst[...] = src[...].astype(BF16)
    hn_ref = out_refs[-1]
    for r in range(0, a_ref.shape[0], DOWN_PROJ_SUBTILE):
        rows = slice(r, r + DOWN_PROJ_SUBTILE)
        h = r_ref[rows, :] + jnp.dot(a_ref[rows, :], w_ref[...], preferred_element_type=F32)
        hn_ref[rows, :] = _rmsnorm_rows(h, g_ref[...]).astype(hn_ref.dtype)
        if len(out_refs) == 2:
            out_refs[0][rows, :] = h


def _down_proj(a, w, res, gain, final, tm, casts=()):
    m, k = a.shape
    n = w.shape[1]
    steps = m // tm
    rows = pl.BlockSpec((tm, n), lambda i: (i, 0))
    if final:
        out_specs, out_shape = [rows], [jax.ShapeDtypeStruct((m, n), F32)]
    else:
        out_specs = [rows, rows]
        out_shape = [jax.ShapeDtypeStruct((m, n), F32), jax.ShapeDtypeStruct((m, n), BF16)]
    cast_specs, cast_args = [], []
    for stack, idx in casts:
        kc, nc = stack.shape[1:]
        slice_rows = kc * CAST_STEPS_PER_SLICE // steps
        assert slice_rows * steps == kc * CAST_STEPS_PER_SLICE and slice_rows % BF16_SUBLANES == 0
        cast_specs.append(pl.BlockSpec(
            (None, slice_rows, nc), lambda i, idx=idx: (idx, i // CAST_STEPS_PER_SLICE, 0)))
        cast_args.append(stack)
        out_specs.append(pl.BlockSpec((slice_rows, nc), lambda i: (i // CAST_STEPS_PER_SLICE, 0)))
        out_shape.append(jax.ShapeDtypeStruct((kc, nc), BF16))
    outs = pl.pallas_call(
        functools.partial(_down_proj_kernel, len(casts)),
        grid=(steps,),
        in_specs=[
            pl.BlockSpec((tm, k), lambda i: (i, 0)),
            pl.BlockSpec((k, n), lambda i: (0, 0), pipeline_mode=RESIDENT),
            rows,
            pl.BlockSpec((1, n), lambda i: (0, 0)),
            *cast_specs,
        ],
        out_specs=out_specs,
        out_shape=out_shape,
        compiler_params=_params("arbitrary"),
        name="down_proj",
    )(a, w, res, gain.reshape(1, n), *cast_args)
    n_main = len(outs) - len(casts)
    return outs[:n_main], outs[n_main:]


def _out_proj_kernel(y_ref, xq_ref, kv_ref, w_ref, r_ref, g_ref, h_ref, hn_ref, wb_ref):
    @pl.when(pl.program_id(0) == 0)
    def _():
        wb_ref[...] = w_ref[...].astype(BF16)

    wy = y_ref.shape[1]
    d = xq_ref.shape[1] // XA_HEADS
    heads = []
    for hd in range(XA_HEADS):
        q = xq_ref[:, hd * d:(hd + 1) * d]
        k = kv_ref[:, hd * d:(hd + 1) * d]
        v = kv_ref[:, (XA_HEADS + hd) * d:(XA_HEADS + hd + 1) * d]
        s = lax.dot_general(q, k, NT_DIMS, preferred_element_type=F32) * (d ** -0.5)
        p = jnp.exp(s - jnp.max(s, axis=-1, keepdims=True))
        denom = jnp.sum(p, axis=-1, keepdims=True)
        o = jnp.dot(p.astype(BF16), v, preferred_element_type=F32)
        heads.append((o / denom).astype(BF16))
    xa = jnp.concatenate(heads, axis=1)
    for r in range(0, y_ref.shape[0], OUT_PROJ_SUBTILE):
        rows = slice(r, r + OUT_PROJ_SUBTILE)
        acc = jnp.dot(y_ref[rows, :], wb_ref[:wy, :], preferred_element_type=F32)
        acc += jnp.dot(xa[rows, :], wb_ref[wy:, :], preferred_element_type=F32)
        h = r_ref[rows, :] + acc
        h_ref[rows, :] = h
        hn_ref[rows, :] = _rmsnorm_rows(h, g_ref[...]).astype(hn_ref.dtype)


def _out_proj(y, proj, xq_block, kv_stack, w_stack, layer, res, gain, batch, tm):
    m, wy = y.shape
    _, k, n = w_stack.shape
    wx = k - wy
    n_mem, wkv = kv_stack.shape[1:]
    steps_per_batch = m // batch // tm
    return pl.pallas_call(
        _out_proj_kernel,
        grid=(m // tm,),
        in_specs=[
            pl.BlockSpec((tm, wy), lambda i: (i, 0)),
            pl.BlockSpec((tm, wx), lambda i: (i, xq_block)),
            pl.BlockSpec((None, n_mem, wkv), lambda i: (layer * batch + i // steps_per_batch, 0, 0)),
            pl.BlockSpec((None, k, n), lambda i: (layer, 0, 0), pipeline_mode=RESIDENT),
            pl.BlockSpec((tm, n), lambda i: (i, 0)),
            pl.BlockSpec((1, n), lambda i: (0, 0)),
        ],
        out_specs=[pl.BlockSpec((tm, n), lambda i: (i, 0)), pl.BlockSpec((tm, n), lambda i: (i, 0))],
        out_shape=[jax.ShapeDtypeStruct((m, n), F32), jax.ShapeDtypeStruct((m, n), BF16)],
        scratch_shapes=[pltpu.VMEM((k, n), BF16)],
        compiler_params=_params("arbitrary"),
        name="xattn_out_proj",
    )(y, proj, kv_stack, w_stack, res, gain.reshape(1, n))


def _chunk_masks():
    t = lax.broadcasted_iota(jnp.int32, (CHUNK, CHUNK), 0)
    s = lax.broadcasted_iota(jnp.int32, (CHUNK, CHUNK), 1)
    tril = (s <= t)
    return tril.astype(BF16), tril


def _chunk_cumsum(tril_b, hi, lo):
    dk = hi.shape[1]
    both = jnp.dot(tril_b, jnp.concatenate([hi, lo], axis=1), preferred_element_type=F32)
    return both[:, :dk] + both[:, dk:]


def _chunk_scores(q, k, vb, cum, sub):
    dk = q.shape[1]
    n_sub = CHUNK // sub
    blocks = [slice(sub * r, sub * (r + 1)) for r in range(n_sub)]
    mid = [cum[sub * j + sub // 2 - 1:sub * j + sub // 2, :] for j in range(n_sub)]
    last = cum[CHUNK - 1:CHUNK, :]
    zero = jnp.zeros((sub, dk), F32)

    q_slots, k_slots, k_last = [], [], []
    for j in range(n_sub):
        q_slots.append(jnp.concatenate(
            [zero if r < j else q[blocks[r], :] * jnp.exp(cum[blocks[r], :] - mid[j])
             for r in range(n_sub)], axis=0))
        k_mid = k[blocks[j], :] * jnp.exp(mid[j] - cum[blocks[j], :])
        k_slots.append(jnp.concatenate([k_mid if r == j else zero for r in range(n_sub)], axis=0))
        k_last.append(k_mid * jnp.exp(last - mid[j]))
    s_raw = lax.dot_general(jnp.concatenate(q_slots, axis=1).astype(BF16),
                            jnp.concatenate(k_slots, axis=1).astype(BF16),
                            NT_DIMS, preferred_element_type=F32)
    upd = lax.dot_general(vb, jnp.concatenate(k_last, axis=0).astype(BF16), TN_DIMS,
                          preferred_element_type=F32)
    q_dec = (q_slots[0] * jnp.exp(mid[0])).astype(BF16)
    return s_raw, upd, q_dec, jnp.exp(last), vb


def _chunk_readout(s_raw, tril, vb, q_dec, st):
    intra = jnp.dot(jnp.where(tril, s_raw, 0.0).astype(BF16), vb, preferred_element_type=F32)
    inter = lax.dot_general(q_dec, st.astype(BF16), NT_DIMS, preferred_element_type=F32)
    return intra + inter


def _gated_head_norm(o, gain, gate):
    ms = jnp.mean(o * o, axis=-1, keepdims=True)
    return o * lax.rsqrt(ms + EPS) * gain * gate


def _run_staggered(n_chunks, stages):
    n_groups = n_chunks // STAGGER_GROUP
    for slot in range(n_groups + len(stages) - 1):
        for depth in reversed(range(len(stages))):
            group = slot - depth
            if 0 <= group < n_groups:
                for c in range(group * STAGGER_GROUP, (group + 1) * STAGGER_GROUP):
                    stages[depth](c)


def _mixer_kernel(sub, q_ref, k_ref, v_ref, gate_ref, hi_ref, lo_ref, on_ref, y_ref, st_ref):
    @pl.when(pl.program_id(2) == 0)
    def _():
        st_ref[...] = jnp.zeros_like(st_ref)

    tril_b, tril = _chunk_masks()
    gain = on_ref[...]
    n_chunks = q_ref.shape[0] // CHUNK
    rows = [slice(c * CHUNK, (c + 1) * CHUNK) for c in range(n_chunks)]
    ctx = [{} for _ in range(n_chunks)]
    state = [st_ref[...]]

    def decay(c):
        ctx[c]["cum"] = _chunk_cumsum(tril_b, hi_ref[rows[c], :], lo_ref[rows[c], :])

    def scores(c):
        ctx[c]["parts"] = _chunk_scores(q_ref[rows[c], :].astype(F32), k_ref[rows[c], :].astype(F32),
                                        v_ref[rows[c], :], ctx[c].pop("cum"), sub)

    def readout(c):
        s_raw, upd, q_dec, e_last, vb = ctx[c].pop("parts")
        ctx[c]["o"] = _chunk_readout(s_raw, tril, vb, q_dec, state[0])
        state[0] = state[0] * e_last + upd

    def emit(c):
        gate = gate_ref[rows[c], :].astype(F32)
        y_ref[rows[c], :] = _gated_head_norm(ctx[c].pop("o"), gain, gate).astype(y_ref.dtype)

    _run_staggered(n_chunks, [decay, scores, readout, emit])
    st_ref[...] = state[0]


def _mixer(proj, onorm, cols, heads, dk, dv, sub, batch, t_blk, name):
    m = proj.shape[0]
    n_t = m // batch // t_blk

    def group(key, width):
        block0, rem = divmod(cols[key], width)
        assert rem == 0
        return pl.BlockSpec((t_blk, width), lambda b, h, t: (b * n_t + t, block0 + h))

    return pl.pallas_call(
        functools.partial(_mixer_kernel, sub),
        grid=(batch, heads, n_t),
        in_specs=[group("q", dk), group("k", dk), group("v", dv), group("gate", dv),
                  group("hi", dk), group("lo", dk),
                  pl.BlockSpec((1, dv), lambda b, h, t: (0, 0))],
        out_specs=pl.BlockSpec((t_blk, dv), lambda b, h, t: (b * n_t + t, h)),
        out_shape=jax.ShapeDtypeStruct((m, heads * dv), BF16),
        scratch_shapes=[pltpu.VMEM((dv, dk), F32)],
        compiler_params=_params("parallel", "parallel", "arbitrary"),
        name=name,
    )(proj, proj, proj, proj, proj, proj, onorm.reshape(1, dv))


def _gla_pad_decay_weights(w_gk, b_gk):
    heads, rank, dkp = GLA_HEADS, GLA_GATE_RANK, GLA_DK_PAD
    kw = w_gk.shape[1]
    dk = kw // heads

    def pad_heads(w):
        w = w.reshape(w.shape[0], heads, dk)
        return jnp.pad(w, ((0, 0), (0, 0), (0, dkp - dk))).reshape(w.shape[0], heads * dkp)

    w_gk_pad = jnp.pad(pad_heads(w_gk), ((0, LANE - rank), (0, 0))).astype(BF16)
    b_gk_pad = pad_heads(b_gk.reshape(1, kw)).astype(F32)
    return w_gk_pad, b_gk_pad, dk


def kernel(x, mem, norm_mix, norm_ffn, norm_mem, norm_final, hgrn_w_in, hgrn_lb_logits, hgrn_onorm,
           gla_w_in, gla_w_gk, gla_b_gk, gla_onorm, w_mem_kv, w_out, w_gate_up, w_down):
    batch, seq, d_model = x.shape
    n_mem = mem.shape[1]
    depth = norm_mix.shape[0]
    mix_width = hgrn_lb_logits.shape[1]
    xa_width = d_model - mix_width
    m = batch * seq

    h = x.reshape(m, d_model)
    kv = _mem_kv(mem.reshape(batch * n_mem, d_model), norm_mem, w_mem_kv)
    kv = kv.reshape(depth * batch, n_mem, 2 * xa_width)

    w_in_b = hgrn_w_in[0].astype(BF16)
    gla_w = gla_w_in.astype(BF16)

    for layer in range(depth):
        j = layer // 2
        if layer % 2 == 0:
            if layer == 0:
                proj = _hgrn_in_proj(h, norm_mix[0], w_in_b, j, hgrn_lb_logits, tm=256)
            else:
                proj = _hgrn_in_proj(hn, None, w_in_b, j, hgrn_lb_logits, tm=512)
            w = mix_width
            cols = dict(q=0, hi=w, lo=2 * w, k=3 * w, v=4 * w, gate=5 * w)
            y = _mixer(proj, hgrn_onorm[j], cols, mix_width // HGRN_HEAD_DIM, HGRN_HEAD_DIM,
                       HGRN_HEAD_DIM, HGRN_SUB, batch, t_blk=8192, name="hgrn2_mixer")
            xq_col = 6 * w
        else:
            w_gk_pad, b_gk_pad, dk = _gla_pad_decay_weights(gla_w_gk[j], gla_b_gk[j])
            proj = _gla_in_proj(hn, gla_w, j, w_gk_pad, b_gk_pad, mix_width, xa_width, dk, tm=512)
            wk = GLA_HEADS * GLA_DK_PAD
            xq_col = 2 * mix_width + 2 * wk
            cols = dict(v=0, gate=mix_width, q=2 * mix_width, k=2 * mix_width + wk,
                        hi=xq_col + xa_width, lo=xq_col + xa_width + wk)
            y = _mixer(proj, gla_onorm[j], cols, GLA_HEADS, GLA_DK_PAD, mix_width // GLA_HEADS,
                       GLA_SUB, batch, t_blk=4096, name="gla_mixer")
        assert xq_col % xa_width == 0
        h, hn = _out_proj(y, proj, xq_col // xa_width, kv, w_out, layer, h, norm_ffn[layer], batch,
                          tm=512)
        act, w_down_b = _ffn_up(hn, w_gate_up, w_down, layer, tm=2048, tn=512)
        if layer + 1 < depth:
            nxt = layer + 1
            casts = [(hgrn_w_in, nxt // 2)] if nxt % 2 == 0 else []
            (h, hn), w_in_next = _down_proj(
                act, w_down_b, h, norm_mix[nxt], final=False, tm=256, casts=casts)
            if w_in_next:
                w_in_b = w_in_next[0]
        else:
            (out,), _ = _down_proj(act, w_down_b, h, norm_final, final=True, tm=512)
    return out.reshape(batch, seq, d_model)
```
